```python
import jax, jax.numpy as jnp
from jax import lax
import numpy as np

D_MODEL = 4096
BATCH = 4
SEQ = 4096
DEPTH = 1

HEAD_DIM = 128
ATT_WIDTH = D_MODEL // 2
ATT_HEADS = ATT_WIDTH // HEAD_DIM
KV_HEADS = 4
GQA_GROUP = ATT_HEADS // KV_HEADS
IDX_HEADS = 16
IDX_DIM = 64
DSA_TOPK_MAX = 256
QUERY_BLOCK = 128
SSD_WIDTH = D_MODEL - ATT_WIDTH
SSD_HEAD_DIM = 64
SSD_HEADS = SSD_WIDTH // SSD_HEAD_DIM
SSD_GROUPS = 8
SSD_HEADS_PER_GROUP = SSD_HEADS // SSD_GROUPS
SSD_STATE = 128
SSD_CONV = 4
SSD_CHUNK = 128
XBC_WIDTH = SSD_WIDTH + 2 * SSD_GROUPS * SSD_STATE
MIX_WIDTH = ATT_WIDTH + SSD_WIDTH
N_EXPERT_GROUPS = 8
EXPERTS_PER_GROUP = 8
N_EXPERTS = N_EXPERT_GROUPS * EXPERTS_PER_GROUP
EXPERT_TOP_K = 2
D_EXPERT = 1024
MOE_BLOCK = 128
DEEPNORM_ALPHA = (2 * DEPTH) ** 0.25
DEEPNORM_BETA = (8 * DEPTH) ** -0.25
LN_EPS = 1e-5
RMS_EPS = 1e-5

PROJ_SIZES = (ATT_WIDTH, KV_HEADS * HEAD_DIM, KV_HEADS * HEAD_DIM, IDX_HEADS * IDX_DIM, IDX_DIM, IDX_HEADS,
              SSD_WIDTH, XBC_WIDTH, SSD_HEADS)
PROJ_WIDTH = sum(PROJ_SIZES)
PROJ_SPLITS = tuple(sum(PROJ_SIZES[:i + 1]) for i in range(len(PROJ_SIZES) - 1))

kernel_name = "hymba_dsa_ssd_hiermoe_deepnorm"


def _layer_norm(x, g, b):
    xf = x.astype(jnp.float32)
    mu = jnp.mean(xf, axis=-1, keepdims=True)
    var = jnp.mean(jnp.square(xf - mu), axis=-1, keepdims=True)
    y = (xf - mu) * lax.rsqrt(var + LN_EPS)
    return (y * g.astype(jnp.float32) + b.astype(jnp.float32)).astype(x.dtype)


def _dsa_attention(q, k, v, q_idx, k_idx, w_idx, kn_g, kn_b):
    bsz, seq, _ = q.shape
    top_k = min(DSA_TOPK_MAX, seq // 4)
    n_blk = seq // QUERY_BLOCK
    k = k.reshape(bsz, seq, KV_HEADS, HEAD_DIM)
    v = v.reshape(bsz, seq, KV_HEADS, HEAD_DIM)
    k_idx = _layer_norm(k_idx, kn_g, kn_b)
    w_idx = w_idx.astype(jnp.float32) * (IDX_HEADS ** -0.5 * IDX_DIM ** -0.5)

    def to_blocks(a, *tail):
        return jnp.moveaxis(a.reshape(bsz, n_blk, QUERY_BLOCK, *tail), 1, 0)

    q_b = to_blocks(q, KV_HEADS, GQA_GROUP, HEAD_DIM)
    qi_b = to_blocks(q_idx, IDX_HEADS, IDX_DIM)
    wi_b = to_blocks(w_idx, IDX_HEADS)
    key_pos = jnp.arange(seq)
    scale = HEAD_DIM ** -0.5

    def attend_block(args):
        blk, qb, qib, wib = args
        q_pos = blk * QUERY_BLOCK + jnp.arange(QUERY_BLOCK)
        dots = jnp.einsum('bthd,bsd->bths', qib, k_idx).astype(jnp.float32)
        score = jnp.einsum('bths,bth->bts', jax.nn.relu(dots), wib)
        score = jnp.where(key_pos[None, None, :] <= q_pos[None, :, None], score, -jnp.inf)
        sel_score, sel_idx = lax.top_k(score, top_k)
        valid = jnp.isfinite(sel_score)
        k_sel = jax.vmap(lambda kb, ib: kb[ib])(k, sel_idx)
        v_sel = jax.vmap(lambda vb, ib: vb[ib])(v, sel_idx)
        logits = jnp.einsum('btkgd,btnkd->btkgn', qb, k_sel).astype(jnp.float32) * scale
        logits = jnp.where(valid[:, :, None, None, :], logits, -jnp.inf)
        probs = jax.nn.softmax(logits, axis=-1).astype(v.dtype)
        out = jnp.einsum('btkgn,btnkd->btkgd', probs, v_sel)
        return out.reshape(bsz, QUERY_BLOCK, ATT_WIDTH)

    out = lax.map(attend_block, (jnp.arange(n_blk), q_b, qi_b, wi_b))
    return jnp.moveaxis(out, 0, 1).reshape(bsz, seq, ATT_WIDTH)


def _ssd_mixer(z, xbc, dt_raw, conv_w, conv_b, dt_bias, a_log, d_skip, norm_g):
    bsz, seq, _ = xbc.shape
    xbc = lax.conv_general_dilated(xbc, conv_w[:, None, :].astype(xbc.dtype), window_strides=(1,),
                                   padding=((SSD_CONV - 1, 0),), dimension_numbers=('NWC', 'WIO', 'NWC'),
                                   feature_group_count=XBC_WIDTH)
    xbc = jax.nn.silu(xbc + conv_b.astype(xbc.dtype)).astype(jnp.float32)
    xs, bm, cm = jnp.split(xbc, (SSD_WIDTH, SSD_WIDTH + SSD_GROUPS * SSD_STATE), axis=-1)
    dt = jax.nn.softplus(dt_raw.astype(jnp.float32) + dt_bias.astype(jnp.float32))
    a = -jnp.exp(a_log.astype(jnp.float32)).reshape(SSD_GROUPS, SSD_HEADS_PER_GROUP)
    nc, cq = seq // SSD_CHUNK, SSD_CHUNK
    xc = xs.reshape(bsz, nc, cq, SSD_GROUPS, SSD_HEADS_PER_GROUP, SSD_HEAD_DIM)
    dtc = dt.reshape(bsz, nc, cq, SSD_GROUPS, SSD_HEADS_PER_GROUP)
    bc = bm.reshape(bsz, nc, cq, SSD_GROUPS, SSD_STATE)
    cc = cm.reshape(bsz, nc, cq, SSD_GROUPS, SSD_STATE)
    acs = jnp.moveaxis(jnp.cumsum(dtc * a, axis=2), 2, -1)
    xdt = xc * dtc[..., None]
    causal = jnp.tril(jnp.ones((cq, cq), dtype=bool))
    seg = acs[..., :, None] - acs[..., None, :]
    lmat = jnp.exp(jnp.where(causal, seg, -jnp.inf))
    cb = jnp.einsum('bclgn,bcsgn->bcgls', cc, bc)
    y_diag = jnp.einsum('bcgls,bcgjls,bcsgjp->bclgjp', cb, lmat, xdt)
    decay_states = jnp.exp(acs[..., -1:] - acs)
    states = jnp.einsum('bclgn,bcgjl,bclgjp->bcgjpn', bc, decay_states, xdt)
    chunk_decay = jnp.exp(acs[..., -1])

    def step(h, inp):
        dec, st = inp
        return dec[..., None, None] * h + st, h

    h0 = jnp.zeros((bsz, SSD_GROUPS, SSD_HEADS_PER_GROUP, SSD_HEAD_DIM, SSD_STATE), jnp.float32)
    _, prev = lax.scan(step, h0, (jnp.moveaxis(chunk_decay, 1, 0), jnp.moveaxis(states, 1, 0)))
    prev = jnp.moveaxis(prev, 0, 1)
    y_off = jnp.einsum('bclgn,bcgjpn,bcgjl->bclgjp', cc, prev, jnp.exp(acs))
    d = d_skip.astype(jnp.float32).reshape(SSD_GROUPS, SSD_HEADS_PER_GROUP)[..., None]
    y = (y_diag + y_off + xc * d).reshape(bsz, seq, SSD_WIDTH)
    gated = (y * jax.nn.silu(z.astype(jnp.float32))).reshape(bsz, seq, SSD_GROUPS, SSD_WIDTH // SSD_GROUPS)
    gated = gated * lax.rsqrt(jnp.mean(jnp.square(gated), axis=-1, keepdims=True) + RMS_EPS)
    return (gated.reshape(bsz, seq, SSD_WIDTH) * norm_g.astype(jnp.float32)).astype(z.dtype)


def _hier_moe(h, w_rg, b_rg, w_re, b_re, w_gate, w_up, w_down):
    bsz, seq, dm = h.shape
    n_tok = bsz * seq
    hf = h.reshape(n_tok, dm)
    g_logits = jnp.dot(hf, w_rg).astype(jnp.float32) + b_rg.astype(jnp.float32)
    g_prob = jax.nn.softmax(g_logits, axis=-1)
    g_w, g_sel = lax.top_k(g_prob, 1)
    e_logits = (jnp.dot(hf, w_re).astype(jnp.float32) + b_re.astype(jnp.float32))
    e_logits = e_logits.reshape(n_tok, N_EXPERT_GROUPS, EXPERTS_PER_GROUP)
    e_logits = jnp.take_along_axis(e_logits, g_sel[:, :, None], axis=1)[:, 0]
    e_val, e_loc = lax.top_k(e_logits, EXPERT_TOP_K)
    gate = g_w * jax.nn.softmax(e_val, axis=-1)
    expert_id = g_sel * EXPERTS_PER_GROUP + e_loc

    m = n_tok * EXPERT_TOP_K
    e_flat = expert_id.reshape(m).astype(jnp.int32)
    tok_flat = jnp.repeat(jnp.arange(n_tok, dtype=jnp.int32), EXPERT_TOP_K)
    g_flat = gate.reshape(m)
    order = jnp.argsort(e_flat)
    se, stok, sg = e_flat[order], tok_flat[order], g_flat[order]
    counts = jnp.bincount(e_flat, length=N_EXPERTS)
    starts = jnp.cumsum(counts) - counts
    pcounts = (counts + MOE_BLOCK - 1) // MOE_BLOCK * MOE_BLOCK
    pends = jnp.cumsum(pcounts)
    pstarts = pends - pcounts
    dest = pstarts[se] + jnp.arange(m, dtype=jnp.int32) - starts[se]
    n_blocks = -(-m // MOE_BLOCK) + N_EXPERTS
    n_rows = n_blocks * MOE_BLOCK
    row_tok = jnp.zeros((n_rows,), jnp.int32).at[dest].set(stok)
    row_gate = jnp.zeros((n_rows,), jnp.float32).at[dest].set(sg)
    block_e = jnp.minimum(jnp.searchsorted(pends, jnp.arange(n_blocks) * MOE_BLOCK, side='right'),
                          N_EXPERTS - 1).astype(jnp.int32)

    def expert_block(args):
        toks, e = args
        xb = hf[toks]
        act = jax.nn.silu(jnp.dot(xb, w_gate[e])) * jnp.dot(xb, w_up[e])
        return jnp.dot(act, w_down[e])

    out = lax.map(expert_block, (row_tok.reshape(n_blocks, MOE_BLOCK), block_e)).reshape(n_rows, dm)
    y = jax.ops.segment_sum(out * row_gate[:, None].astype(out.dtype), row_tok, num_segments=n_tok)
    return y.reshape(bsz, seq, dm).astype(h.dtype)


def setup_inputs(seed: int = 0) -> dict:
    key = jax.random.key(seed)
    ks = jax.random.split(key, 24)
    f32 = jnp.float32
    nrm = lambda k, shape, s: jax.random.normal(k, shape, f32) * s
    x = jax.random.normal(ks[0], (BATCH, SEQ, D_MODEL), f32)
    col_scale = jnp.concatenate([
        jnp.ones((ATT_WIDTH + KV_HEADS * HEAD_DIM,), f32),
        jnp.full((KV_HEADS * HEAD_DIM,), DEEPNORM_BETA, f32),
        jnp.ones((IDX_HEADS * IDX_DIM + IDX_DIM + IDX_HEADS + SSD_WIDTH,), f32),
        jnp.full((SSD_WIDTH,), DEEPNORM_BETA, f32),
        jnp.ones((2 * SSD_GROUPS * SSD_STATE + SSD_HEADS,), f32)])
    w_in = nrm(ks[1], (DEPTH, D_MODEL, PROJ_WIDTH), D_MODEL ** -0.5) * col_scale
    idx_kn_g = 1.0 + nrm(ks[2], (DEPTH, IDX_DIM), 0.02)
    idx_kn_b = nrm(ks[3], (DEPTH, IDX_DIM), 0.02)
    conv_w = nrm(ks[4], (DEPTH, SSD_CONV, XBC_WIDTH), SSD_CONV ** -0.5)
    conv_b = nrm(ks[5], (DEPTH, XBC_WIDTH), 0.02)
    u = jax.random.uniform(ks[6], (DEPTH, SSD_HEADS), f32)
    dt0 = jnp.exp(u * (jnp.log(0.1) - jnp.log(0.001)) + jnp.log(0.001))
    dt_bias = dt0 + jnp.log(-jnp.expm1(-dt0))
    a_log = jnp.log(jax.random.uniform(ks[7], (DEPTH, SSD_HEADS), f32, 1.0, 16.0))
    d_skip = 1.0 + nrm(ks[8], (DEPTH, SSD_HEADS), 0.02)
    ssd_norm_g = 1.0 + nrm(ks[9], (DEPTH, SSD_WIDTH), 0.02)
    w_out = nrm(ks[10], (DEPTH, MIX_WIDTH, D_MODEL), MIX_WIDTH ** -0.5 * DEEPNORM_BETA)
    ln1_g = 1.0 + nrm(ks[11], (DEPTH, D_MODEL), 0.02)
    ln1_b = nrm(ks[12], (DEPTH, D_MODEL), 0.02)
    w_rg = nrm(ks[13], (DEPTH, D_MODEL, N_EXPERT_GROUPS), D_MODEL ** -0.5)
    b_rg = nrm(ks[14], (DEPTH, N_EXPERT_GROUPS), 0.01)
    w_re = nrm(ks[15], (DEPTH, D_MODEL, N_EXPERTS), D_MODEL ** -0.5)
    b_re = nrm(ks[16], (DEPTH, N_EXPERTS), 0.01)
    w_gate = nrm(ks[17], (DEPTH, N_EXPERTS, D_MODEL, D_EXPERT), D_MODEL ** -0.5 * DEEPNORM_BETA)
    w_up = nrm(ks[18], (DEPTH, N_EXPERTS, D_MODEL, D_EXPERT), D_MODEL ** -0.5 * DEEPNORM_BETA)
    w_down = nrm(ks[19], (DEPTH, N_EXPERTS, D_EXPERT, D_MODEL), D_EXPERT ** -0.5 * DEEPNORM_BETA)
    ln2_g = 1.0 + nrm(ks[20], (DEPTH, D_MODEL), 0.02)
    ln2_b = nrm(ks[21], (DEPTH, D_MODEL), 0.02)
    return {"x": x, "w_in": w_in, "idx_kn_g": idx_kn_g, "idx_kn_b": idx_kn_b, "conv_w": conv_w,
            "conv_b": conv_b, "dt_bias": dt_bias, "a_log": a_log, "d_skip": d_skip,
            "ssd_norm_g": ssd_norm_g, "w_out": w_out, "ln1_g": ln1_g, "ln1_b": ln1_b,
            "w_rg": w_rg, "b_rg": b_rg, "w_re": w_re, "b_re": b_re, "w_gate": w_gate,
            "w_up": w_up, "w_down": w_down, "ln2_g": ln2_g, "ln2_b": ln2_b}


def reference(x, w_in, idx_kn_g, idx_kn_b, conv_w, conv_b, dt_bias, a_log, d_skip, ssd_norm_g, w_out,
              ln1_g, ln1_b, w_rg, b_rg, w_re, b_re, w_gate, w_up, w_down, ln2_g, ln2_b):
    for l in range(DEPTH):
        proj = jnp.einsum('bsd,dp->bsp', x, w_in[l])
        q, k, v, q_idx, k_idx, w_idx, z, xbc, dt_raw = jnp.split(proj, PROJ_SPLITS, axis=-1)
        att = _dsa_attention(q, k, v, q_idx, k_idx, w_idx, idx_kn_g[l], idx_kn_b[l])
        ssd = _ssd_mixer(z, xbc, dt_raw, conv_w[l], conv_b[l], dt_bias[l], a_log[l], d_skip[l], ssd_norm_g[l])
        mixed = jnp.einsum('bsm,md->bsd', jnp.concatenate([att, ssd.astype(att.dtype)], axis=-1), w_out[l])
        x = _layer_norm(DEEPNORM_ALPHA * x + mixed, ln1_g[l], ln1_b[l])
        ffn = _hier_moe(x, w_rg[l], b_rg[l], w_re[l], b_re[l], w_gate[l], w_up[l], w_down[l])
        x = _layer_norm(DEEPNORM_ALPHA * x + ffn, ln2_g[l], ln2_b[l])
    return x
```

```python
import functools

import jax
import jax.numpy as jnp
from jax import lax
from jax.experimental import pallas as pl
from jax.experimental.pallas import tpu as pltpu

F32 = jnp.float32
BF16 = jnp.bfloat16
I32 = jnp.int32

HEAD_DIM = 128
KV_HEADS = 4
GQA_GROUP = 4
ATT_HEADS = KV_HEADS * GQA_GROUP
IDX_HEADS = 16
IDX_DIM = 64
DSA_TOPK_MAX = 256
QUERY_BLOCK = 128
SSD_HEAD_DIM = 64
SSD_GROUPS = 8
SSD_HEADS_PER_GROUP = 4
SSD_HEADS = SSD_GROUPS * SSD_HEADS_PER_GROUP
SSD_STATE = 128
SSD_CONV = 4
SSD_CHUNK = 128
N_EXPERT_GROUPS = 8
EXPERTS_PER_GROUP = 8
N_EXPERTS = 64
LN_EPS = 1e-5
RMS_EPS = 1e-5

LANES = 128
SUBLANES = 8
VMEM_LIMIT = 56 * 1024 * 1024

KEY_CHUNK = 512
MOE_ROWS = 512
MOE_SUB = 128
MOE_F_CHUNK = 256
MOE_N_CHUNK = 512
NEG_BIG = -1e30
INT_MIN = -2 ** 31


def _params(sem):
    return pltpu.CompilerParams(dimension_semantics=sem, vmem_limit_bytes=VMEM_LIMIT)


def _mm_kernel(a_ref, b_ref, o_ref, abf_ref):
    @pl.when(pl.program_id(1) == 0)
    def _():
        abf_ref[...] = a_ref[...].astype(BF16)

    o_ref[...] = jnp.dot(abf_ref[...], b_ref[...], preferred_element_type=F32).astype(o_ref.dtype)


def _matmul(a, b, out_dtype, tm, tn):
    m, k = a.shape
    n = b.shape[1]
    tm = min(tm, m)
    assert m % tm == 0 and n % tn == 0
    return pl.pallas_call(
        _mm_kernel,
        grid=(m // tm, n // tn),
        in_specs=[pl.BlockSpec((tm, k), lambda i, j: (i, 0)),
                  pl.BlockSpec((k, tn), lambda i, j: (0, j))],
        out_specs=pl.BlockSpec((tm, tn), lambda i, j: (i, j)),
        out_shape=jax.ShapeDtypeStruct((m, n), out_dtype),
        scratch_shapes=[pltpu.VMEM((tm, k), BF16)],
        compiler_params=_params(("parallel", "arbitrary")),
        name="matmul",
    )(a, b)


def _attn_kernel(q_ref, k_ref, v_ref, qi_ref, kw_ref, g_ref, b_ref, o_ref,
                 kln_ref, key_ref, bias_ref, *, top_k):
    i = pl.program_id(1)
    tq = QUERY_BLOCK
    ck = KEY_CHUNK

    @pl.when(i == 0)
    def _():
        kx = kw_ref[:, 0:IDX_DIM]
        mu = jnp.mean(kx, axis=-1, keepdims=True)
        var = jnp.mean(jnp.square(kx - mu), axis=-1, keepdims=True)
        y = (kx - mu) * lax.rsqrt(var + LN_EPS)
        kln_ref[...] = (y * g_ref[...] + b_ref[...]).astype(BF16)

    q_start = i * tq
    n_chunks = (q_start + tq + ck - 1) // ck
    q_pos = q_start + lax.broadcasted_iota(I32, (tq, 1), 0)
    w = kw_ref[pl.ds(pl.multiple_of(q_start, tq), tq), IDX_DIM:IDX_DIM + IDX_HEADS]
    w = w * (IDX_HEADS ** -0.5 * IDX_DIM ** -0.5)
    qi = qi_ref[...].astype(BF16)

    def chunk_off(c):
        return pl.multiple_of(c * ck, ck)

    def key_pos(c):
        return c * ck + lax.broadcasted_iota(I32, (1, ck), 1)

    def score_chunk(c, carry):
        off = chunk_off(c)
        kc = kln_ref[pl.ds(off, ck), :]
        acc = jnp.zeros((tq, ck), F32)
        for h in range(IDX_HEADS):
            d = lax.dot_general(qi[:, h * IDX_DIM:(h + 1) * IDX_DIM], kc,
                                (((1,), (1,)), ((), ())), preferred_element_type=F32)
            acc = acc + jnp.maximum(d, 0.0) * w[:, h:h + 1]
        acc = jnp.where(key_pos(c) <= q_pos, acc, -jnp.inf)
        bits = pltpu.bitcast(acc, I32)
        key_ref[:, pl.ds(off, ck)] = bits ^ ((bits >> 31) & 0x7FFFFFFF)
        return carry

    lax.fori_loop(0, n_chunks, score_chunk, 0)

    def bit_body(b, cand):
        trial = cand | lax.shift_left(jnp.int32(1), jnp.int32(31) - jnp.asarray(b, I32))
        trial_b = jnp.broadcast_to(trial ^ INT_MIN, (tq, LANES))

        def cnt_chunk(c, cnt):
            kc = key_ref[:, pl.ds(chunk_off(c), ck)]
            for s in range(ck // LANES):
                cnt = cnt + jnp.where(kc[:, s * LANES:(s + 1) * LANES] >= trial_b, 1.0, 0.0)
            return cnt

        cnt = lax.fori_loop(0, n_chunks, cnt_chunk, jnp.zeros((tq, LANES), F32))
        total = jnp.sum(cnt, axis=1, keepdims=True)
        return jnp.where(total >= float(top_k), trial, cand)

    cand = lax.fori_loop(0, 32, bit_body, jnp.zeros((tq, 1), I32))
    thr = cand ^ INT_MIN

    def bias_chunk(c, carry):
        off = chunk_off(c)
        sel = (key_ref[:, pl.ds(off, ck)] >= thr) & (key_pos(c) <= q_pos)
        bias_ref[:, pl.ds(off, ck)] = jnp.where(sel, 0.0, NEG_BIG)
        return carry

    lax.fori_loop(0, n_chunks, bias_chunk, 0)

    scale = HEAD_DIM ** -0.5
    for h in range(ATT_HEADS):
        g = h // GQA_GROUP
        qh = q_ref[:, h * HEAD_DIM:(h + 1) * HEAD_DIM]

        def att_chunk(c, carry, g=g, qh=qh):
            m, l, acc = carry
            off = chunk_off(c)
            kc = k_ref[pl.ds(off, ck), g * HEAD_DIM:(g + 1) * HEAD_DIM]
            vc = v_ref[pl.ds(off, ck), g * HEAD_DIM:(g + 1) * HEAD_DIM]
            s = lax.dot_general(qh, kc, (((1,), (1,)), ((), ())), preferred_element_type=F32)
            s = s * scale + bias_ref[:, pl.ds(off, ck)]
            m_new = jnp.maximum(m, jnp.max(s, axis=1, keepdims=True))
            p = jnp.exp(s - m_new)
            alpha = jnp.exp(m - m_new)
            l = alpha * l + jnp.sum(p, axis=1, keepdims=True)
            acc = alpha * acc + jnp.dot(p.astype(BF16), vc, preferred_element_type=F32)
            return m_new, l, acc

        init = (jnp.full((tq, 1), NEG_BIG, F32), jnp.zeros((tq, 1), F32), jnp.zeros((tq, HEAD_DIM), F32))
        _, l, acc = lax.fori_loop(0, n_chunks, att_chunk, init)
        o_ref[:, h * HEAD_DIM:(h + 1) * HEAD_DIM] = (acc / l).astype(o_ref.dtype)


def _dsa_attention(qkv, idx, kn_g, kn_b, bsz, seq):
    top_k = min(DSA_TOPK_MAX, seq // 4)
    nq = seq // QUERY_BLOCK
    att_w = ATT_HEADS * HEAD_DIM
    kv_w = KV_HEADS * HEAD_DIM
    qi_w = IDX_HEADS * IDX_DIM
    assert seq % KEY_CHUNK == 0 and att_w % kv_w == 0 and qi_w % LANES == 0
    return pl.pallas_call(
        functools.partial(_attn_kernel, top_k=top_k),
        grid=(bsz, nq),
        in_specs=[pl.BlockSpec((QUERY_BLOCK, att_w), lambda b, i: (b * nq + i, 0)),
                  pl.BlockSpec((seq, kv_w), lambda b, i: (b, att_w // kv_w)),
                  pl.BlockSpec((seq, kv_w), lambda b, i: (b, att_w // kv_w + 1)),
                  pl.BlockSpec((QUERY_BLOCK, qi_w), lambda b, i: (b * nq + i, 0)),
                  pl.BlockSpec((seq, LANES), lambda b, i: (b, qi_w // LANES)),
                  pl.BlockSpec((1, IDX_DIM), lambda b, i: (0, 0)),
                  pl.BlockSpec((1, IDX_DIM), lambda b, i: (0, 0))],
        out_specs=pl.BlockSpec((QUERY_BLOCK, att_w), lambda b, i: (b * nq + i, 0)),
        out_shape=jax.ShapeDtypeStruct((bsz * seq, att_w), BF16),
        scratch_shapes=[pltpu.VMEM((seq, IDX_DIM), BF16),
                        pltpu.VMEM((QUERY_BLOCK, seq), I32),
                        pltpu.VMEM((QUERY_BLOCK, seq), F32)],
        compiler_params=_params(("parallel", "arbitrary")),
        name="dsa_attention",
    )(qkv, qkv, qkv, idx, idx, kn_g.reshape(1, IDX_DIM), kn_b.reshape(1, IDX_DIM))


def _silu(x):
    return x / (1.0 + jnp.exp(-x))


def _ssd_kernel(xbc_ref, z_ref, dt_ref, cw_ref, cb_ref, dtb_ref, alog_ref, dsk_ref, ng_ref, o_ref,
                xpad_ref, act_ref, y_ref, h_ref):
    c = pl.program_id(1)
    cq = SSD_CHUNK
    width = SSD_HEADS * SSD_HEAD_DIM
    b_off = width
    c_off = width + SSD_GROUPS * SSD_STATE

    @pl.when(c == 0)
    def _():
        xpad_ref[0:SUBLANES, :] = jnp.zeros((SUBLANES, xpad_ref.shape[1]), F32)
        h_ref[...] = jnp.zeros(h_ref.shape, F32)

    xpad_ref[SUBLANES:SUBLANES + cq, :] = xbc_ref[...]
    col = 512
    for j in range(xpad_ref.shape[1] // col):
        cs = slice(j * col, (j + 1) * col)
        acc = cb_ref[:, cs] + jnp.zeros((cq, col), F32)
        for t in range(SSD_CONV):
            r0 = SUBLANES - (SSD_CONV - 1) + t
            acc = acc + xpad_ref[r0:r0 + cq, cs] * cw_ref[t:t + 1, cs]
        act_ref[:, cs] = _silu(acc)
    xpad_ref[0:SUBLANES, :] = xpad_ref[cq:cq + SUBLANES, :]

    xdt_in = dt_ref[:, 0:LANES] + dtb_ref[...]
    dt = jnp.maximum(xdt_in, 0.0) + jnp.log1p(jnp.exp(-jnp.abs(xdt_in)))
    da = dt * (-jnp.exp(alog_ref[...]))
    row = lax.broadcasted_iota(I32, (cq, cq), 0)
    coli = lax.broadcasted_iota(I32, (cq, cq), 1)
    causal = row >= coli
    tril = causal.astype(F32)
    acs = jnp.dot(tril, da, preferred_element_type=F32, precision=lax.Precision.HIGHEST)
    acs_t = acs.T
    dec_in = jnp.exp(acs)
    a_last = acs[cq - 1:cq, :]
    dec_out = jnp.exp(a_last - acs)
    dec_chunk = jnp.exp(a_last)

    for g in range(SSD_GROUPS):
        bg = act_ref[:, b_off + g * SSD_STATE:b_off + (g + 1) * SSD_STATE].astype(BF16)
        cg = act_ref[:, c_off + g * SSD_STATE:c_off + (g + 1) * SSD_STATE].astype(BF16)
        cbm = lax.dot_general(cg, bg, (((1,), (1,)), ((), ())), preferred_element_type=F32)
        for j in range(SSD_HEADS_PER_GROUP):
            hd = g * SSD_HEADS_PER_GROUP + j
            xs = slice(hd * SSD_HEAD_DIM, (hd + 1) * SSD_HEAD_DIM)
            seg = acs[:, hd:hd + 1] - acs_t[hd:hd + 1, :]
            lmat = jnp.exp(jnp.where(causal, seg, -jnp.inf))
            xh = act_ref[:, xs]
            xdt = xh * dt[:, hd:hd + 1]
            y = jnp.dot((cbm * lmat).astype(BF16), xdt.astype(BF16), preferred_element_type=F32)
            hprev = h_ref[hd]
            yoff = lax.dot_general(cg, hprev.astype(BF16), (((1,), (1,)), ((), ())),
                                   preferred_element_type=F32)
            y = y + yoff * dec_in[:, hd:hd + 1] + xh * dsk_ref[:, xs]
            y_ref[:, xs] = y
            st = lax.dot_general((xdt * dec_out[:, hd:hd + 1]).astype(BF16), bg,
                                 (((0,), (0,)), ((), ())), preferred_element_type=F32)
            h_ref[hd] = hprev * dec_chunk[:, hd:hd + 1] + st

    gw = width // SSD_GROUPS
    for g in range(SSD_GROUPS):
        gs = slice(g * gw, (g + 1) * gw)
        gated = y_ref[:, gs] * _silu(z_ref[:, gs])
        ms = jnp.mean(jnp.square(gated), axis=-1, keepdims=True)
        o_ref[:, gs] = (gated * lax.rsqrt(ms + RMS_EPS) * ng_ref[:, gs]).astype(o_ref.dtype)


def _ssd_mixer(ssd_in, conv_w, conv_b, dt_bias, a_log, d_skip, norm_g, bsz, seq):
    nc = seq // SSD_CHUNK
    width = SSD_HEADS * SSD_HEAD_DIM
    xbc_w = width + 2 * SSD_GROUPS * SSD_STATE
    dt_w = ssd_in.shape[1] - xbc_w - width
    assert xbc_w % width == 0 and (xbc_w + width) % dt_w == 0 and dt_w >= LANES
    pad = LANES - SSD_HEADS
    dtb = jnp.pad(dt_bias.astype(F32), (0, pad)).reshape(1, LANES)
    alog = jnp.pad(a_log.astype(F32), (0, pad)).reshape(1, LANES)
    dsk = jnp.repeat(d_skip.astype(F32), SSD_HEAD_DIM).reshape(1, width)
    const = lambda b, c: (0, 0)
    return pl.pallas_call(
        _ssd_kernel,
        grid=(bsz, nc),
        in_specs=[pl.BlockSpec((SSD_CHUNK, xbc_w), lambda b, c: (b * nc + c, 0)),
                  pl.BlockSpec((SSD_CHUNK, width), lambda b, c: (b * nc + c, xbc_w // width)),
                  pl.BlockSpec((SSD_CHUNK, dt_w), lambda b, c: (b * nc + c, (xbc_w + width) // dt_w)),
                  pl.BlockSpec((SSD_CONV, xbc_w), const),
                  pl.BlockSpec((1, xbc_w), const),
                  pl.BlockSpec((1, LANES), const),
                  pl.BlockSpec((1, LANES), const),
                  pl.BlockSpec((1, width), const),
                  pl.BlockSpec((1, width), const)],
        out_specs=pl.BlockSpec((SSD_CHUNK, width), lambda b, c: (b * nc + c, 0)),
        out_shape=jax.ShapeDtypeStruct((bsz * seq, width), BF16),
        scratch_shapes=[pltpu.VMEM((SSD_CHUNK + SUBLANES, xbc_w), F32),
                        pltpu.VMEM((SSD_CHUNK, xbc_w), F32),
                        pltpu.VMEM((SSD_CHUNK, width), F32),
                        pltpu.VMEM((SSD_HEADS, SSD_HEAD_DIM, SSD_STATE), F32)],
        compiler_params=_params(("parallel", "arbitrary")),
        name="ssd_mixer",
    )(ssd_in, ssd_in, ssd_in, conv_w.astype(F32), conv_b.astype(F32).reshape(1, xbc_w), dtb, alog, dsk,
      norm_g.astype(F32).reshape(1, width))


def _layer_norm_rows(x, g, b):
    mu = jnp.mean(x, axis=-1, keepdims=True)
    var = jnp.mean(jnp.square(x - mu), axis=-1, keepdims=True)
    return (x - mu) * lax.rsqrt(var + LN_EPS) * g + b


def _ln_router_kernel(x_ref, mix_ref, g_ref, b_ref, wr_ref, br_ref, o_ref, r_ref, *, alpha):
    x1 = _layer_norm_rows(alpha * x_ref[...] + mix_ref[...], g_ref[...], b_ref[...])
    o_ref[...] = x1
    logits = jnp.dot(x1, wr_ref[...], preferred_element_type=F32, precision=lax.Precision.HIGHEST)
    logits = logits + br_ref[...]
    rows = logits.shape[0]
    lane = lax.broadcasted_iota(I32, (rows, LANES), 1).astype(F32)
    ng, epg = float(N_EXPERT_GROUPS), float(EXPERTS_PER_GROUP)
    far = float(LANES)

    gmask = lane < ng
    gl = jnp.where(gmask, logits, -jnp.inf)
    ge = jnp.exp(gl - jnp.max(gl, axis=1, keepdims=True))
    gprob = ge / jnp.sum(ge, axis=1, keepdims=True)
    gprob = jnp.where(gmask, gprob, -1.0)
    gw = jnp.max(gprob, axis=1, keepdims=True)
    gsel = jnp.min(jnp.where(gprob == gw, lane, far), axis=1, keepdims=True)

    e_lo = ng + gsel * epg
    emask = (lane >= e_lo) & (lane < e_lo + epg)
    el = jnp.where(emask, logits, -jnp.inf)
    v0 = jnp.max(el, axis=1, keepdims=True)
    i0 = jnp.min(jnp.where(emask & (el == v0), lane, far), axis=1, keepdims=True)
    emask1 = emask & (lane != i0)
    el1 = jnp.where(emask1, logits, -jnp.inf)
    v1 = jnp.max(el1, axis=1, keepdims=True)
    i1 = jnp.min(jnp.where(emask1 & (el1 == v1), lane, far), axis=1, keepdims=True)
    e = jnp.exp(v1 - v0)
    p0 = 1.0 / (1.0 + e)
    p1 = e / (1.0 + e)
    out = jnp.where(lane == 0.0, i0 - ng,
                    jnp.where(lane == 1.0, i1 - ng,
                              jnp.where(lane == 2.0, gw * p0, jnp.where(lane == 3.0, gw * p1, 0.0))))
    r_ref[...] = out


def _ln_router(x, mixed, g, b, w_rg, b_rg, w_re, b_re, alpha, tr=256):
    t, d = x.shape
    ncol = N_EXPERT_GROUPS + N_EXPERTS
    wr = jnp.pad(jnp.concatenate([w_rg, w_re], axis=1).astype(F32), ((0, 0), (0, LANES - ncol)))
    br = jnp.pad(jnp.concatenate([b_rg, b_re]).astype(F32), (0, LANES - ncol)).reshape(1, LANES)
    const = lambda i: (0, 0)
    return pl.pallas_call(
        functools.partial(_ln_router_kernel, alpha=alpha),
        grid=(t // tr,),
        in_specs=[pl.BlockSpec((tr, d), lambda i: (i, 0)),
                  pl.BlockSpec((tr, d), lambda i: (i, 0)),
                  pl.BlockSpec((1, d), const), pl.BlockSpec((1, d), const),
                  pl.BlockSpec((d, LANES), const), pl.BlockSpec((1, LANES), const)],
        out_specs=[pl.BlockSpec((tr, d), lambda i: (i, 0)),
                   pl.BlockSpec((tr, LANES), lambda i: (i, 0))],
        out_shape=[jax.ShapeDtypeStruct((t, d), F32), jax.ShapeDtypeStruct((t, LANES), F32)],
        compiler_params=_params(("parallel",)),
        name="ln1_router",
    )(x, mixed, g.astype(F32).reshape(1, d), b.astype(F32).reshape(1, d), wr, br)


def _row_copy(src_ref, dst_ref, sem, tok, row):
    return pltpu.make_async_copy(src_ref.at[pl.ds(tok, 1)], dst_ref.at[pl.ds(row, 1)], sem)


def _gather_kernel(tok_ref, nval_ref, x_ref, o_ref, zero_ref, gsem, zsem):
    s = pl.program_id(0)
    n = pl.num_programs(0)
    sub = MOE_SUB

    @pl.when(s == 0)
    def _():
        zero_ref[...] = jnp.zeros(zero_ref.shape, zero_ref.dtype)

    def zero_copy(blk):
        return pltpu.make_async_copy(zero_ref, o_ref.at[pl.ds(pl.multiple_of(blk * sub, sub), sub)],
                                     zsem.at[blk % 2])

    def start(blk):
        @pl.when(nval_ref[blk] > 0)
        def _():
            def body(r, carry):
                row = blk * sub + r
                _row_copy(x_ref, o_ref, gsem.at[blk % 2], tok_ref[row], row).start()
                return carry
            lax.fori_loop(0, sub, body, 0)

        @pl.when(nval_ref[blk] == 0)
        def _():
            zero_copy(blk).start()

    def wait(blk):
        @pl.when(nval_ref[blk] > 0)
        def _():
            def body(r, carry):
                row = blk * sub + r
                _row_copy(x_ref, o_ref, gsem.at[blk % 2], tok_ref[row], row).wait()
                return carry
            lax.fori_loop(0, sub, body, 0)

        @pl.when(nval_ref[blk] == 0)
        def _():
            zero_copy(blk).wait()

    start(s)

    @pl.when(s > 0)
    def _():
        wait(s - 1)

    @pl.when(s == n - 1)
    def _():
        wait(s)


def _gather_rows(x1, row_tok, sub_valid):
    t, d = x1.shape
    n_rows = row_tok.shape[0]
    return pl.pallas_call(
        _gather_kernel,
        grid_spec=pltpu.PrefetchScalarGridSpec(
            num_scalar_prefetch=2,
            grid=(n_rows // MOE_SUB,),
            in_specs=[pl.BlockSpec(memory_space=pl.ANY)],
            out_specs=pl.BlockSpec(memory_space=pl.ANY),
            scratch_shapes=[pltpu.VMEM((MOE_SUB, d), F32),
                            pltpu.SemaphoreType.DMA((2,)),
                            pltpu.SemaphoreType.DMA((2,))]),
        out_shape=jax.ShapeDtypeStruct((n_rows, d), F32),
        compiler_params=_params(("arbitrary",)),
        name="moe_gather",
    )(row_tok, sub_valid, x1)


def _moe_kernel(be_ref, bv_ref, x_ref, wg_ref, wu_ref, wd_ref, o_ref, xbf_ref, act_ref, *, nf):
    s = pl.program_id(0)
    p = pl.program_id(1)
    valid = bv_ref[s]
    nsub = (valid + MOE_SUB - 1) // MOE_SUB
    n_r = MOE_ROWS // MOE_SUB

    @pl.when(p == 0)
    def _():
        xbf_ref[...] = x_ref[...].astype(BF16)

    @pl.when(p < nf)
    def _():
        wg = wg_ref[...].astype(BF16)
        wu = wu_ref[...].astype(BF16)
        for r in range(n_r):
            @pl.when(r < nsub)
            def _(r=r):
                xr = xbf_ref[r * MOE_SUB:(r + 1) * MOE_SUB, :]
                gate = jnp.dot(xr, wg, preferred_element_type=F32)
                up = jnp.dot(xr, wu, preferred_element_type=F32)
                act_ref[p, r * MOE_SUB:(r + 1) * MOE_SUB, :] = (_silu(gate) * up).astype(BF16)

    @pl.when(p >= nf)
    def _():
        wd = wd_ref[...].astype(BF16)
        for r in range(n_r):
            rs = slice(r * MOE_SUB, (r + 1) * MOE_SUB)

            @pl.when(r < nsub)
            def _(rs=rs):
                acc = jnp.zeros((MOE_SUB, o_ref.shape[1]), F32)
                for q in range(nf):
                    acc = acc + jnp.dot(act_ref[q, rs, :], wd[q * MOE_F_CHUNK:(q + 1) * MOE_F_CHUNK, :],
                                        preferred_element_type=F32)
                o_ref[rs, :] = acc

            @pl.when(r >= nsub)
            def _(rs=rs):
                o_ref[rs, :] = jnp.zeros((MOE_SUB, o_ref.shape[1]), F32)


def _moe_ffn(xs, blk_e, blk_valid, w_gate, w_up, w_down):
    n_rows, d = xs.shape
    n_e, _, f = w_gate.shape
    nf = f // MOE_F_CHUNK
    nn = d // MOE_N_CHUNK
    n_blk = n_rows // MOE_ROWS

    def gu_map(s, p, be, bv):
        return (be[s], 0, jnp.minimum(p, nf - 1))

    def dn_map(s, p, be, bv):
        return (be[s], 0, jnp.maximum(p - nf, 0))

    return pl.pallas_call(
        functools.partial(_moe_kernel, nf=nf),
        grid_spec=pltpu.PrefetchScalarGridSpec(
            num_scalar_prefetch=2,
            grid=(n_blk, nf + nn),
            in_specs=[pl.BlockSpec((MOE_ROWS, d), lambda s, p, be, bv: (s, 0)),
                      pl.BlockSpec((None, d, MOE_F_CHUNK), gu_map),
                      pl.BlockSpec((None, d, MOE_F_CHUNK), gu_map),
                      pl.BlockSpec((None, f, MOE_N_CHUNK), dn_map)],
            out_specs=pl.BlockSpec((MOE_ROWS, MOE_N_CHUNK), lambda s, p, be, bv: (s, jnp.maximum(p - nf, 0))),
            scratch_shapes=[pltpu.VMEM((MOE_ROWS, d), BF16),
                            pltpu.VMEM((nf, MOE_ROWS, MOE_F_CHUNK), BF16)]),
        out_shape=jax.ShapeDtypeStruct((n_rows, d), F32),
        compiler_params=_params(("arbitrary", "arbitrary")),
        name="moe_ffn",
    )(blk_e, blk_valid, xs, w_gate, w_up, w_down)


def _combine_kernel(dest_ref, y_ref, x_ref, r_ref, g_ref, b_ref, o_ref, buf_ref, sem, *, alpha, tt):
    s = pl.program_id(0)
    n = pl.num_programs(0)

    def copy(blk, r):
        slot = blk % 2
        return pltpu.make_async_copy(y_ref.at[pl.ds(dest_ref[blk * 2 * tt + r], 1)],
                                     buf_ref.at[slot, pl.ds(r, 1)], sem.at[slot])

    def start(blk):
        def body(r, carry):
            copy(blk, r).start()
            return carry
        lax.fori_loop(0, 2 * tt, body, 0)

    def wait(blk):
        def body(r, carry):
            copy(blk, r).wait()
            return carry
        lax.fori_loop(0, 2 * tt, body, 0)

    @pl.when(s == 0)
    def _():
        start(s)

    @pl.when(s + 1 < n)
    def _():
        start(s + 1)

    wait(s)
    slot = s % 2
    g0 = r_ref[:, 2:3]
    g1 = r_ref[:, 3:4]
    y = g0 * buf_ref[slot, 0:tt, :] + g1 * buf_ref[slot, tt:2 * tt, :]
    o_ref[...] = _layer_norm_rows(alpha * x_ref[...] + y, g_ref[...], b_ref[...])


def _combine_ln(y_rows, dest, x1, route, g, b, alpha, tt=128):
    t, d = x1.shape
    const = lambda i, dr: (0, 0)
    return pl.pallas_call(
        functools.partial(_combine_kernel, alpha=alpha, tt=tt),
        grid_spec=pltpu.PrefetchScalarGridSpec(
            num_scalar_prefetch=1,
            grid=(t // tt,),
            in_specs=[pl.BlockSpec(memory_space=pl.ANY),
                      pl.BlockSpec((tt, d), lambda i, dr: (i, 0)),
                      pl.BlockSpec((tt, LANES), lambda i, dr: (i, 0)),
                      pl.BlockSpec((1, d), const), pl.BlockSpec((1, d), const)],
            out_specs=pl.BlockSpec((tt, d), lambda i, dr: (i, 0)),
            scratch_shapes=[pltpu.VMEM((2, 2 * tt, d), F32),
                            pltpu.SemaphoreType.DMA((2,))]),
        out_shape=jax.ShapeDtypeStruct((t, d), F32),
        compiler_params=_params(("arbitrary",)),
        name="moe_combine_ln2",
    )(dest, y_rows, x1, route, g.astype(F32).reshape(1, d), b.astype(F32).reshape(1, d))


def _dispatch_plan(expert_id, tt):
    t = expert_id.shape[0]
    m = 2 * t
    e_flat = expert_id.reshape(m)
    onehot = (e_flat[:, None] == jnp.arange(N_EXPERTS, dtype=I32)[None, :]).astype(I32)
    counts = jnp.sum(onehot, axis=0)
    rank = jnp.sum((jnp.cumsum(onehot, axis=0) - onehot) * onehot, axis=1)
    pcounts = (counts + MOE_ROWS - 1) // MOE_ROWS * MOE_ROWS
    pends = jnp.cumsum(pcounts)
    pstarts = pends - pcounts
    dest = (pstarts[e_flat] + rank).astype(I32)
    n_rows = m + N_EXPERTS * MOE_ROWS
    n_blk = n_rows // MOE_ROWS
    tok = jnp.arange(m, dtype=I32) // 2
    row_tok = jnp.zeros((n_rows,), I32).at[dest].set(tok)
    blk_start = jnp.arange(n_blk, dtype=I32) * MOE_ROWS
    blk_e = jnp.minimum(jnp.searchsorted(pends, blk_start, side='right'), N_EXPERTS - 1).astype(I32)
    blk_valid = jnp.clip(counts[blk_e] - (blk_start - pstarts[blk_e]), 0, MOE_ROWS)
    blk_valid = jnp.where(blk_start < pends[-1], blk_valid, 0).astype(I32)
    sub_start = jnp.arange(n_rows // MOE_SUB, dtype=I32) * MOE_SUB
    sub_valid = jnp.clip(blk_valid[sub_start // MOE_ROWS] - sub_start % MOE_ROWS, 0, MOE_SUB).astype(I32)
    dest_tiles = dest.reshape(t // tt, tt, 2).transpose(0, 2, 1).reshape(m)
    return row_tok, sub_valid, blk_e, blk_valid, dest_tiles


def kernel(x, w_in, idx_kn_g, idx_kn_b, conv_w, conv_b, dt_bias, a_log, d_skip, ssd_norm_g, w_out,
           ln1_g, ln1_b, w_rg, b_rg, w_re, b_re, w_gate, w_up, w_down, ln2_g, ln2_b):
    bsz, seq, d = x.shape
    depth = w_in.shape[0]
    alpha = (2 * depth) ** 0.25
    att_w = ATT_HEADS * HEAD_DIM
    kv_w = KV_HEADS * HEAD_DIM
    qi_w = IDX_HEADS * IDX_DIM
    ssd_w = SSD_HEADS * SSD_HEAD_DIM
    xbc_w = ssd_w + 2 * SSD_GROUPS * SSD_STATE
    sizes = (att_w, kv_w, kv_w, qi_w, IDX_DIM, IDX_HEADS, ssd_w, xbc_w, SSD_HEADS)
    offs = [0]
    for sz in sizes:
        offs.append(offs[-1] + sz)
    tt = 128
    xf = x.reshape(bsz * seq, d)
    for l in range(depth):
        w = w_in[l].astype(BF16)
        zpad = lambda n: jnp.zeros((d, n), BF16)
        w_qkv = w[:, offs[0]:offs[3]]
        w_idx = jnp.concatenate([w[:, offs[3]:offs[6]], zpad(LANES - IDX_DIM - IDX_HEADS)], axis=1)
        w_ssd = jnp.concatenate([w[:, offs[7]:offs[8]], w[:, offs[6]:offs[7]], w[:, offs[8]:offs[9]],
                                 zpad(2 * LANES - SSD_HEADS)], axis=1)
        qkv = _matmul(xf, w_qkv, BF16, 512, 512)
        idx = _matmul(xf, w_idx, F32, 512, 384)
        ssd_in = _matmul(xf, w_ssd, F32, 512, 256)
        att = _dsa_attention(qkv, idx, idx_kn_g[l], idx_kn_b[l], bsz, seq)
        ssd = _ssd_mixer(ssd_in, conv_w[l], conv_b[l], dt_bias[l], a_log[l], d_skip[l], ssd_norm_g[l], bsz, seq)
        mixed = _matmul(jnp.concatenate([att, ssd], axis=1), w_out[l].astype(BF16), F32, 1024, 512)
        x1, route = _ln_router(xf, mixed, ln1_g[l], ln1_b[l], w_rg[l], b_rg[l], w_re[l], b_re[l], alpha)
        expert_id = route[:, 0:2].astype(I32)
        row_tok, sub_valid, blk_e, blk_valid, dest_tiles = _dispatch_plan(expert_id, tt)
        xs = _gather_rows(x1, row_tok, sub_valid)
        y_rows = _moe_ffn(xs, blk_e, blk_valid, w_gate[l], w_up[l], w_down[l])
        xf = _combine_ln(y_rows, dest_tiles, x1, route, ln2_g[l], ln2_b[l], alpha, tt)
    return xf.reshape(bsz, seq, d)
```

```python
import functools

import jax
import jax.numpy as jnp
from jax import lax
from jax.experimental import pallas as pl
from jax.experimental.pallas import tpu as pltpu

F32 = jnp.float32
BF16 = jnp.bfloat16
I32 = jnp.int32

HEAD_DIM = 128
KV_HEADS = 4
GQA_GROUP = 4
ATT_HEADS = KV_HEADS * GQA_GROUP
IDX_HEADS = 16
IDX_DIM = 64
DSA_TOPK_MAX = 256
QUERY_BLOCK = 128
SSD_HEAD_DIM = 64
SSD_GROUPS = 8
SSD_HEADS_PER_GROUP = 4
SSD_HEADS = SSD_GROUPS * SSD_HEADS_PER_GROUP
SSD_STATE = 128
SSD_CONV = 4
SSD_CHUNK = 128
N_EXPERT_GROUPS = 8
EXPERTS_PER_GROUP = 8
N_EXPERTS = 64
LN_EPS = 1e-5
RMS_EPS = 1e-5

LANES = 128
SUBLANES = 8
VMEM_LIMIT = 56 * 1024 * 1024

KEY_CHUNK = 512
MOE_ROWS = 1024
MOE_SUB = 256
MOE_F_CHUNK = 256
MOE_N_CHUNK = 512
NEG_BIG = -1e30
INT_MIN = -2 ** 31


def _params(sem):
    return pltpu.CompilerParams(dimension_semantics=sem, vmem_limit_bytes=VMEM_LIMIT)


def _mm_kernel(a_ref, b_ref, o_ref, abf_ref):
    @pl.when(pl.program_id(1) == 0)
    def _():
        abf_ref[...] = a_ref[...].astype(BF16)

    o_ref[...] = jnp.dot(abf_ref[...], b_ref[...], preferred_element_type=F32).astype(o_ref.dtype)


def _matmul(a, b, out_dtype, tm, tn):
    m, k = a.shape
    n = b.shape[1]
    tm = min(tm, m)
    assert m % tm == 0 and n % tn == 0
    return pl.pallas_call(
        _mm_kernel,
        grid=(m // tm, n // tn),
        in_specs=[pl.BlockSpec((tm, k), lambda i, j: (i, 0)),
                  pl.BlockSpec((k, tn), lambda i, j: (0, j))],
        out_specs=pl.BlockSpec((tm, tn), lambda i, j: (i, j)),
        out_shape=jax.ShapeDtypeStruct((m, n), out_dtype),
        scratch_shapes=[pltpu.VMEM((tm, k), BF16)],
        compiler_params=_params(("parallel", "arbitrary")),
        name="matmul",
    )(a, b)


def _attn_kernel(q_ref, k_ref, v_ref, qi_ref, kw_ref, g_ref, b_ref, o_ref,
                 kln_ref, key_ref, bias_ref, *, top_k):
    i = pl.program_id(1)
    tq = QUERY_BLOCK
    ck = KEY_CHUNK

    @pl.when(i == 0)
    def _():
        kx = kw_ref[:, 0:IDX_DIM]
        mu = jnp.mean(kx, axis=-1, keepdims=True)
        var = jnp.mean(jnp.square(kx - mu), axis=-1, keepdims=True)
        y = (kx - mu) * lax.rsqrt(var + LN_EPS)
        kln_ref[...] = (y * g_ref[...] + b_ref[...]).astype(BF16)

    q_start = i * tq
    n_chunks = (q_start + tq + ck - 1) // ck
    q_pos = q_start + lax.broadcasted_iota(I32, (tq, 1), 0)
    w = kw_ref[pl.ds(pl.multiple_of(q_start, tq), tq), IDX_DIM:IDX_DIM + IDX_HEADS]
    w = w * (IDX_HEADS ** -0.5 * IDX_DIM ** -0.5)
    qi = jnp.concatenate([qi_ref[:, h * IDX_DIM:(h + 1) * IDX_DIM] for h in range(IDX_HEADS)], axis=0).astype(BF16)

    def chunk_off(c):
        return pl.multiple_of(c * ck, ck)

    def key_pos(c):
        return c * ck + lax.broadcasted_iota(I32, (1, ck), 1)

    def score_chunk(c, carry):
        off = chunk_off(c)
        kc = kln_ref[pl.ds(off, ck), :]
        d = lax.dot_general(qi, kc, (((1,), (1,)), ((), ())), preferred_element_type=F32)
        acc = jnp.zeros((tq, ck), F32)
        for h in range(IDX_HEADS):
            acc = acc + jnp.maximum(d[h * tq:(h + 1) * tq, :], 0.0) * w[:, h:h + 1]
        acc = jnp.where(key_pos(c) <= q_pos, acc, -jnp.inf)
        bits = pltpu.bitcast(acc, I32)
        key_ref[:, pl.ds(off, ck)] = bits ^ ((bits >> 31) & 0x7FFFFFFF)
        return carry

    lax.fori_loop(0, n_chunks, score_chunk, 0)

    def bit_body(b, cand):
        trial = cand | lax.shift_left(jnp.int32(1), jnp.int32(31) - jnp.asarray(b, I32))
        trial_b = jnp.broadcast_to(trial ^ INT_MIN, (tq, LANES))

        def cnt_chunk(c, cnt):
            kc = key_ref[:, pl.ds(chunk_off(c), ck)]
            for s in range(ck // LANES):
                cnt = cnt + jnp.where(kc[:, s * LANES:(s + 1) * LANES] >= trial_b, 1.0, 0.0)
            return cnt

        cnt = lax.fori_loop(0, n_chunks, cnt_chunk, jnp.zeros((tq, LANES), F32))
        total = jnp.sum(cnt, axis=1, keepdims=True)
        return jnp.where(total >= float(top_k), trial, cand)

    cand = lax.fori_loop(0, 32, bit_body, jnp.zeros((tq, 1), I32))
    thr = cand ^ INT_MIN

    def bias_chunk(c, carry):
        off = chunk_off(c)
        sel = (key_ref[:, pl.ds(off, ck)] >= thr) & (key_pos(c) <= q_pos)
        bias_ref[:, pl.ds(off, ck)] = jnp.where(sel, 0.0, NEG_BIG)
        return carry

    lax.fori_loop(0, n_chunks, bias_chunk, 0)

    scale = HEAD_DIM ** -0.5
    gq = GQA_GROUP
    for g in range(KV_HEADS):
        qg = jnp.concatenate([q_ref[:, (g * gq + j) * HEAD_DIM:(g * gq + j + 1) * HEAD_DIM] for j in range(gq)],
                             axis=0)

        def att_chunk(c, carry, g=g, qg=qg):
            m, l, acc = carry
            off = chunk_off(c)
            kc = k_ref[pl.ds(off, ck), g * HEAD_DIM:(g + 1) * HEAD_DIM]
            vc = v_ref[pl.ds(off, ck), g * HEAD_DIM:(g + 1) * HEAD_DIM]
            s = lax.dot_general(qg, kc, (((1,), (1,)), ((), ())), preferred_element_type=F32)
            bias = bias_ref[:, pl.ds(off, ck)]
            s = s * scale + jnp.concatenate([bias] * gq, axis=0)
            m_new = jnp.maximum(m, jnp.max(s, axis=1, keepdims=True))
            p = jnp.exp(s - m_new)
            alpha = jnp.exp(m - m_new)
            l = alpha * l + jnp.sum(p, axis=1, keepdims=True)
            acc = alpha * acc + jnp.dot(p.astype(BF16), vc, preferred_element_type=F32)
            return m_new, l, acc

        init = (jnp.full((gq * tq, 1), NEG_BIG, F32), jnp.zeros((gq * tq, 1), F32),
                jnp.zeros((gq * tq, HEAD_DIM), F32))
        _, l, acc = lax.fori_loop(0, n_chunks, att_chunk, init)
        out = acc / l
        for j in range(gq):
            h = g * gq + j
            o_ref[:, h * HEAD_DIM:(h + 1) * HEAD_DIM] = out[j * tq:(j + 1) * tq, :].astype(o_ref.dtype)


def _dsa_attention(qkv, idx, kn_g, kn_b, bsz, seq):
    top_k = min(DSA_TOPK_MAX, seq // 4)
    nq = seq // QUERY_BLOCK
    att_w = ATT_HEADS * HEAD_DIM
    kv_w = KV_HEADS * HEAD_DIM
    qi_w = IDX_HEADS * IDX_DIM
    assert seq % KEY_CHUNK == 0 and att_w % kv_w == 0 and qi_w % LANES == 0
    return pl.pallas_call(
        functools.partial(_attn_kernel, top_k=top_k),
        grid=(bsz, nq),
        in_specs=[pl.BlockSpec((QUERY_BLOCK, att_w), lambda b, i: (b * nq + i, 0)),
                  pl.BlockSpec((seq, kv_w), lambda b, i: (b, att_w // kv_w)),
                  pl.BlockSpec((seq, kv_w), lambda b, i: (b, att_w // kv_w + 1)),
                  pl.BlockSpec((QUERY_BLOCK, qi_w), lambda b, i: (b * nq + i, 0)),
                  pl.BlockSpec((seq, LANES), lambda b, i: (b, qi_w // LANES)),
                  pl.BlockSpec((1, IDX_DIM), lambda b, i: (0, 0)),
                  pl.BlockSpec((1, IDX_DIM), lambda b, i: (0, 0))],
        out_specs=pl.BlockSpec((QUERY_BLOCK, att_w), lambda b, i: (b * nq + i, 0)),
        out_shape=jax.ShapeDtypeStruct((bsz * seq, att_w), BF16),
        scratch_shapes=[pltpu.VMEM((seq, IDX_DIM), BF16),
                        pltpu.VMEM((QUERY_BLOCK, seq), I32),
                        pltpu.VMEM((QUERY_BLOCK, seq), F32)],
        compiler_params=_params(("parallel", "arbitrary")),
        name="dsa_attention",
    )(qkv, qkv, qkv, idx, idx, kn_g.reshape(1, IDX_DIM), kn_b.reshape(1, IDX_DIM))


def _silu(x):
    return x / (1.0 + jnp.exp(-x))


def _ssd_kernel(xbc_ref, z_ref, dt_ref, cw_ref, cb_ref, dtb_ref, alog_ref, dsk_ref, ng_ref, o_ref,
                xpad_ref, act_ref, y_ref, h_ref):
    c = pl.program_id(1)
    cq = SSD_CHUNK
    width = SSD_HEADS * SSD_HEAD_DIM
    b_off = width
    c_off = width + SSD_GROUPS * SSD_STATE

    @pl.when(c == 0)
    def _():
        xpad_ref[0:SUBLANES, :] = jnp.zeros((SUBLANES, xpad_ref.shape[1]), F32)
        h_ref[...] = jnp.zeros(h_ref.shape, F32)

    xpad_ref[SUBLANES:SUBLANES + cq, :] = xbc_ref[...]
    col = 512
    for j in range(xpad_ref.shape[1] // col):
        cs = slice(j * col, (j + 1) * col)
        acc = cb_ref[:, cs] + jnp.zeros((cq, col), F32)
        for t in range(SSD_CONV):
            r0 = SUBLANES - (SSD_CONV - 1) + t
            acc = acc + xpad_ref[r0:r0 + cq, cs] * cw_ref[t:t + 1, cs]
        act_ref[:, cs] = _silu(acc)
    xpad_ref[0:SUBLANES, :] = xpad_ref[cq:cq + SUBLANES, :]

    xdt_in = dt_ref[:, 0:LANES] + dtb_ref[...]
    dt = jnp.maximum(xdt_in, 0.0) + jnp.log1p(jnp.exp(-jnp.abs(xdt_in)))
    da = dt * (-jnp.exp(alog_ref[...]))
    row = lax.broadcasted_iota(I32, (cq, cq), 0)
    coli = lax.broadcasted_iota(I32, (cq, cq), 1)
    causal = row >= coli
    tril = causal.astype(F32)
    acs = jnp.dot(tril, da, preferred_element_type=F32, precision=lax.Precision.HIGHEST)
    acs_t = acs.T
    dec_in = jnp.exp(acs)
    a_last = acs[cq - 1:cq, :]
    dec_out = jnp.exp(a_last - acs)
    dec_chunk = jnp.exp(a_last)

    for g in range(SSD_GROUPS):
        bg = act_ref[:, b_off + g * SSD_STATE:b_off + (g + 1) * SSD_STATE].astype(BF16)
        cg = act_ref[:, c_off + g * SSD_STATE:c_off + (g + 1) * SSD_STATE].astype(BF16)
        cbm = lax.dot_general(cg, bg, (((1,), (1,)), ((), ())), preferred_element_type=F32)
        for j in range(SSD_HEADS_PER_GROUP):
            hd = g * SSD_HEADS_PER_GROUP + j
            xs = slice(hd * SSD_HEAD_DIM, (hd + 1) * SSD_HEAD_DIM)
            seg = acs[:, hd:hd + 1] - acs_t[hd:hd + 1, :]
            lmat = jnp.exp(jnp.where(causal, seg, -jnp.inf))
            xh = act_ref[:, xs]
            xdt = xh * dt[:, hd:hd + 1]
            y = jnp.dot((cbm * lmat).astype(BF16), xdt.astype(BF16), preferred_element_type=F32)
            hprev = h_ref[hd]
            yoff = lax.dot_general(cg, hprev.astype(BF16), (((1,), (1,)), ((), ())),
                                   preferred_element_type=F32)
            y = y + yoff * dec_in[:, hd:hd + 1] + xh * dsk_ref[:, xs]
            y_ref[:, xs] = y
            st = lax.dot_general((xdt * dec_out[:, hd:hd + 1]).astype(BF16), bg,
                                 (((0,), (0,)), ((), ())), preferred_element_type=F32)
            h_ref[hd] = hprev * dec_chunk[:, hd:hd + 1] + st

    gw = width // SSD_GROUPS
    for g in range(SSD_GROUPS):
        gs = slice(g * gw, (g + 1) * gw)
        gated = y_ref[:, gs] * _silu(z_ref[:, gs])
        ms = jnp.mean(jnp.square(gated), axis=-1, keepdims=True)
        o_ref[:, gs] = (gated * lax.rsqrt(ms + RMS_EPS) * ng_ref[:, gs]).astype(o_ref.dtype)


def _ssd_mixer(ssd_in, conv_w, conv_b, dt_bias, a_log, d_skip, norm_g, bsz, seq):
    nc = seq // SSD_CHUNK
    width = SSD_HEADS * SSD_HEAD_DIM
    xbc_w = width + 2 * SSD_GROUPS * SSD_STATE
    dt_w = ssd_in.shape[1] - xbc_w - width
    assert xbc_w % width == 0 and (xbc_w + width) % dt_w == 0 and dt_w >= LANES
    pad = LANES - SSD_HEADS
    dtb = jnp.pad(dt_bias.astype(F32), (0, pad)).reshape(1, LANES)
    alog = jnp.pad(a_log.astype(F32), (0, pad)).reshape(1, LANES)
    dsk = jnp.repeat(d_skip.astype(F32), SSD_HEAD_DIM).reshape(1, width)
    const = lambda b, c: (0, 0)
    return pl.pallas_call(
        _ssd_kernel,
        grid=(bsz, nc),
        in_specs=[pl.BlockSpec((SSD_CHUNK, xbc_w), lambda b, c: (b * nc + c, 0)),
                  pl.BlockSpec((SSD_CHUNK, width), lambda b, c: (b * nc + c, xbc_w // width)),
                  pl.BlockSpec((SSD_CHUNK, dt_w), lambda b, c: (b * nc + c, (xbc_w + width) // dt_w)),
                  pl.BlockSpec((SSD_CONV, xbc_w), const),
                  pl.BlockSpec((1, xbc_w), const),
                  pl.BlockSpec((1, LANES), const),
                  pl.BlockSpec((1, LANES), const),
                  pl.BlockSpec((1, width), const),
                  pl.BlockSpec((1, width), const)],
        out_specs=pl.BlockSpec((SSD_CHUNK, width), lambda b, c: (b * nc + c, 0)),
        out_shape=jax.ShapeDtypeStruct((bsz * seq, width), BF16),
        scratch_shapes=[pltpu.VMEM((SSD_CHUNK + SUBLANES, xbc_w), F32),
                        pltpu.VMEM((SSD_CHUNK, xbc_w), F32),
                        pltpu.VMEM((SSD_CHUNK, width), F32),
                        pltpu.VMEM((SSD_HEADS, SSD_HEAD_DIM, SSD_STATE), F32)],
        compiler_params=_params(("parallel", "arbitrary")),
        name="ssd_mixer",
    )(ssd_in, ssd_in, ssd_in, conv_w.astype(F32), conv_b.astype(F32).reshape(1, xbc_w), dtb, alog, dsk,
      norm_g.astype(F32).reshape(1, width))


def _layer_norm_rows(x, g, b):
    mu = jnp.mean(x, axis=-1, keepdims=True)
    var = jnp.mean(jnp.square(x - mu), axis=-1, keepdims=True)
    return (x - mu) * lax.rsqrt(var + LN_EPS) * g + b


def _ln_router_kernel(x_ref, mix_ref, g_ref, b_ref, wr_ref, br_ref, o_ref, xp_ref, r_ref, *, alpha):
    x1 = _layer_norm_rows(alpha * x_ref[...] + mix_ref[...], g_ref[...], b_ref[...])
    o_ref[...] = x1
    half = xp_ref.shape[1]
    bits = pltpu.bitcast(x1.astype(BF16).astype(F32), jnp.uint32)
    xp_ref[...] = (bits[:, :half] >> 16) | bits[:, half:]
    logits = jnp.dot(x1, wr_ref[...], preferred_element_type=F32, precision=lax.Precision.HIGHEST)
    logits = logits + br_ref[...]
    rows = logits.shape[0]
    lane = lax.broadcasted_iota(I32, (rows, LANES), 1).astype(F32)
    ng, epg = float(N_EXPERT_GROUPS), float(EXPERTS_PER_GROUP)
    far = float(LANES)

    gmask = lane < ng
    gl = jnp.where(gmask, logits, -jnp.inf)
    ge = jnp.exp(gl - jnp.max(gl, axis=1, keepdims=True))
    gprob = ge / jnp.sum(ge, axis=1, keepdims=True)
    gprob = jnp.where(gmask, gprob, -1.0)
    gw = jnp.max(gprob, axis=1, keepdims=True)
    gsel = jnp.min(jnp.where(gprob == gw, lane, far), axis=1, keepdims=True)

    e_lo = ng + gsel * epg
    emask = (lane >= e_lo) & (lane < e_lo + epg)
    el = jnp.where(emask, logits, -jnp.inf)
    v0 = jnp.max(el, axis=1, keepdims=True)
    i0 = jnp.min(jnp.where(emask & (el == v0), lane, far), axis=1, keepdims=True)
    emask1 = emask & (lane != i0)
    el1 = jnp.where(emask1, logits, -jnp.inf)
    v1 = jnp.max(el1, axis=1, keepdims=True)
    i1 = jnp.min(jnp.where(emask1 & (el1 == v1), lane, far), axis=1, keepdims=True)
    e = jnp.exp(v1 - v0)
    p0 = 1.0 / (1.0 + e)
    p1 = e / (1.0 + e)
    out = jnp.where(lane == 0.0, i0 - ng,
                    jnp.where(lane == 1.0, i1 - ng,
                              jnp.where(lane == 2.0, gw * p0, jnp.where(lane == 3.0, gw * p1, 0.0))))
    r_ref[...] = out


def _ln_router(x, mixed, g, b, w_rg, b_rg, w_re, b_re, alpha, tr=256):
    t, d = x.shape
    ncol = N_EXPERT_GROUPS + N_EXPERTS
    wr = jnp.pad(jnp.concatenate([w_rg, w_re], axis=1).astype(F32), ((0, 0), (0, LANES - ncol)))
    br = jnp.pad(jnp.concatenate([b_rg, b_re]).astype(F32), (0, LANES - ncol)).reshape(1, LANES)
    const = lambda i: (0, 0)
    return pl.pallas_call(
        functools.partial(_ln_router_kernel, alpha=alpha),
        grid=(t // tr,),
        in_specs=[pl.BlockSpec((tr, d), lambda i: (i, 0)),
                  pl.BlockSpec((tr, d), lambda i: (i, 0)),
                  pl.BlockSpec((1, d), const), pl.BlockSpec((1, d), const),
                  pl.BlockSpec((d, LANES), const), pl.BlockSpec((1, LANES), const)],
        out_specs=[pl.BlockSpec((tr, d), lambda i: (i, 0)),
                   pl.BlockSpec((tr, d // 2), lambda i: (i, 0)),
                   pl.BlockSpec((tr, LANES), lambda i: (i, 0))],
        out_shape=[jax.ShapeDtypeStruct((t, d), F32), jax.ShapeDtypeStruct((t, d // 2), jnp.uint32),
                   jax.ShapeDtypeStruct((t, LANES), F32)],
        compiler_params=_params(("parallel",)),
        name="ln1_router",
    )(x, mixed, g.astype(F32).reshape(1, d), b.astype(F32).reshape(1, d), wr, br)


def _row_copy(src_ref, dst_ref, sem, tok, row):
    return pltpu.make_async_copy(src_ref.at[pl.ds(tok, 1)], dst_ref.at[pl.ds(row, 1)], sem)


def _moe_kernel(tok_ref, be_ref, bv_ref, nu_ref, x_ref, wg_ref, wu_ref, wd_ref, o_ref,
                xq_ref, xbf_ref, act_ref, sem, *, nf):
    s = pl.program_id(0)
    p = pl.program_id(1)
    n_blk = pl.num_programs(0)
    sub = MOE_SUB
    n_sub = MOE_ROWS // sub
    half = xq_ref.shape[1]

    def nsub_of(blk):
        return (bv_ref[blk] + (sub - 1)) // sub

    def issue(blk, r):
        base = blk * MOE_ROWS + r * sub

        def body(i, carry):
            _row_copy(x_ref, xq_ref, sem, tok_ref[base + i], r * sub + i).start()
            return carry
        lax.fori_loop(0, sub, body, 0)

    nsub = nsub_of(s)

    @pl.when((s == 0) & (p == 0))
    def _():
        for r in range(n_sub):
            @pl.when(r < nsub)
            def _(r=r):
                issue(s, r)

    @pl.when(p == 0)
    def _():
        def wait_body(i, carry):
            _row_copy(x_ref, xq_ref, sem, 0, i).wait()
            return carry
        lax.fori_loop(0, nsub * sub, wait_body, 0)
        for r in range(n_sub):
            @pl.when(r < nsub)
            def _(r=r):
                rs = slice(r * sub, (r + 1) * sub)
                word = xq_ref[rs, :]
                xbf_ref[rs, 0:half] = pltpu.bitcast(word << 16, F32).astype(BF16)
                xbf_ref[rs, half:2 * half] = pltpu.bitcast(word & jnp.uint32(0xFFFF0000), F32).astype(BF16)

    nxt = jnp.minimum(s + 1, n_blk - 1)
    for r in range(n_sub):
        @pl.when((p == r + 1) & (s + 1 < n_blk) & (r < nsub_of(nxt)))
        def _(r=r):
            issue(nxt, r)

    for k in range(1, n_sub + 1):
        m = k * sub

        @pl.when((p < nf) & (nsub == k))
        def _(m=m):
            xr = xbf_ref[0:m, :]
            gate = jnp.dot(xr, wg_ref[...].astype(BF16), preferred_element_type=F32)
            up = jnp.dot(xr, wu_ref[...].astype(BF16), preferred_element_type=F32)
            act_ref[p, 0:m, :] = (_silu(gate) * up).astype(BF16)

    for k in range(0, n_sub + 1):
        m = k * sub

        @pl.when((p >= nf) & (nsub == k))
        def _(m=m):
            if m > 0:
                wd = wd_ref[...].astype(BF16)
                acc = jnp.dot(act_ref[0, 0:m, :], wd[0:MOE_F_CHUNK, :], preferred_element_type=F32)
                for q in range(1, nf):
                    acc = acc + jnp.dot(act_ref[q, 0:m, :], wd[q * MOE_F_CHUNK:(q + 1) * MOE_F_CHUNK, :],
                                        preferred_element_type=F32)
                o_ref[0:m, :] = acc
            if m < MOE_ROWS:
                o_ref[m:MOE_ROWS, :] = jnp.zeros((MOE_ROWS - m, o_ref.shape[1]), F32)


def _moe_ffn(xp, row_tok, blk_e, blk_valid, n_used, w_gate, w_up, w_down):
    n_rows = row_tok.shape[0]
    n_e, d, f = w_gate.shape
    nf = f // MOE_F_CHUNK
    nn = d // MOE_N_CHUNK
    n_blk = n_rows // MOE_ROWS
    assert nf + nn > MOE_ROWS // MOE_SUB

    def gu_map(s, p, tok, be, bv, nu):
        return (be[s], 0, jnp.where(s < nu[0], jnp.minimum(p, nf - 1), nf - 1))

    def dn_map(s, p, tok, be, bv, nu):
        return (be[s], 0, jnp.where(s < nu[0], jnp.maximum(p - nf, 0), nn - 1))

    return pl.pallas_call(
        functools.partial(_moe_kernel, nf=nf),
        grid_spec=pltpu.PrefetchScalarGridSpec(
            num_scalar_prefetch=4,
            grid=(n_blk, nf + nn),
            in_specs=[pl.BlockSpec(memory_space=pl.ANY),
                      pl.BlockSpec((None, d, MOE_F_CHUNK), gu_map),
                      pl.BlockSpec((None, d, MOE_F_CHUNK), gu_map),
                      pl.BlockSpec((None, f, MOE_N_CHUNK), dn_map)],
            out_specs=pl.BlockSpec((MOE_ROWS, MOE_N_CHUNK),
                                   lambda s, p, tok, be, bv, nu: (s, jnp.maximum(p - nf, 0))),
            scratch_shapes=[pltpu.VMEM((MOE_ROWS, d // 2), jnp.uint32),
                            pltpu.VMEM((MOE_ROWS, d), BF16),
                            pltpu.VMEM((nf, MOE_ROWS, MOE_F_CHUNK), BF16),
                            pltpu.SemaphoreType.DMA(())]),
        out_shape=jax.ShapeDtypeStruct((n_rows, d), F32),
        compiler_params=_params(("arbitrary", "arbitrary")),
        name="moe_ffn",
    )(row_tok, blk_e, blk_valid, n_used, xp, w_gate, w_up, w_down)


def _combine_kernel(dest_ref, y_ref, x_ref, r_ref, g_ref, b_ref, o_ref, buf_ref, sem, *, alpha, tt):
    s = pl.program_id(0)
    n = pl.num_programs(0)

    def copy(blk, r):
        slot = blk % 2
        return pltpu.make_async_copy(y_ref.at[pl.ds(dest_ref[blk * 2 * tt + r], 1)],
                                     buf_ref.at[slot, pl.ds(r, 1)], sem.at[slot])

    def start(blk):
        def body(r, carry):
            copy(blk, r).start()
            return carry
        lax.fori_loop(0, 2 * tt, body, 0)

    def wait(blk):
        def body(r, carry):
            copy(blk, r).wait()
            return carry
        lax.fori_loop(0, 2 * tt, body, 0)

    @pl.when(s == 0)
    def _():
        start(s)

    @pl.when(s + 1 < n)
    def _():
        start(s + 1)

    wait(s)
    slot = s % 2
    g0 = r_ref[:, 2:3]
    g1 = r_ref[:, 3:4]
    y = g0 * buf_ref[slot, 0:tt, :] + g1 * buf_ref[slot, tt:2 * tt, :]
    o_ref[...] = _layer_norm_rows(alpha * x_ref[...] + y, g_ref[...], b_ref[...])


def _combine_ln(y_rows, dest, x1, route, g, b, alpha, tt=128):
    t, d = x1.shape
    const = lambda i, dr: (0, 0)
    return pl.pallas_call(
        functools.partial(_combine_kernel, alpha=alpha, tt=tt),
        grid_spec=pltpu.PrefetchScalarGridSpec(
            num_scalar_prefetch=1,
            grid=(t // tt,),
            in_specs=[pl.BlockSpec(memory_space=pl.ANY),
                      pl.BlockSpec((tt, d), lambda i, dr: (i, 0)),
                      pl.BlockSpec((tt, LANES), lambda i, dr: (i, 0)),
                      pl.BlockSpec((1, d), const), pl.BlockSpec((1, d), const)],
            out_specs=pl.BlockSpec((tt, d), lambda i, dr: (i, 0)),
            scratch_shapes=[pltpu.VMEM((2, 2 * tt, d), F32),
                            pltpu.SemaphoreType.DMA((2,))]),
        out_shape=jax.ShapeDtypeStruct((t, d), F32),
        compiler_params=_params(("arbitrary",)),
        name="moe_combine_ln2",
    )(dest, y_rows, x1, route, g.astype(F32).reshape(1, d), b.astype(F32).reshape(1, d))


def _dispatch_plan(expert_id, tt):
    t = expert_id.shape[0]
    m = 2 * t
    e_flat = expert_id.reshape(m)
    onehot = (e_flat[:, None] == jnp.arange(N_EXPERTS, dtype=I32)[None, :]).astype(I32)
    counts = jnp.sum(onehot, axis=0)
    rank = jnp.sum((jnp.cumsum(onehot, axis=0) - onehot) * onehot, axis=1)
    pcounts = (counts + MOE_ROWS - 1) // MOE_ROWS * MOE_ROWS
    pends = jnp.cumsum(pcounts)
    pstarts = pends - pcounts
    dest = (pstarts[e_flat] + rank).astype(I32)
    n_blk = -(-m // MOE_ROWS) + N_EXPERTS
    n_rows = n_blk * MOE_ROWS
    tok = jnp.arange(m, dtype=I32) // 2
    row_tok = jnp.zeros((n_rows,), I32).at[dest].set(tok)
    n_used = (pends[-1] // MOE_ROWS).astype(I32)
    blk_idx = jnp.arange(n_blk, dtype=I32)
    blk_start = jnp.minimum(blk_idx, n_used - 1) * MOE_ROWS
    blk_e = jnp.minimum(jnp.searchsorted(pends, blk_start, side='right'), N_EXPERTS - 1).astype(I32)
    blk_valid = jnp.clip(counts[blk_e] - (blk_start - pstarts[blk_e]), 0, MOE_ROWS)
    blk_valid = jnp.where(blk_idx < n_used, blk_valid, 0).astype(I32)
    dest_tiles = dest.reshape(t // tt, tt, 2).transpose(0, 2, 1).reshape(m)
    return row_tok, blk_e, blk_valid, n_used.reshape(1), dest_tiles


def kernel(x, w_in, idx_kn_g, idx_kn_b, conv_w, conv_b, dt_bias, a_log, d_skip, ssd_norm_g, w_out,
           ln1_g, ln1_b, w_rg, b_rg, w_re, b_re, w_gate, w_up, w_down, ln2_g, ln2_b):
    bsz, seq, d = x.shape
    depth = w_in.shape[0]
    alpha = (2 * depth) ** 0.25
    att_w = ATT_HEADS * HEAD_DIM
    kv_w = KV_HEADS * HEAD_DIM
    qi_w = IDX_HEADS * IDX_DIM
    ssd_w = SSD_HEADS * SSD_HEAD_DIM
    xbc_w = ssd_w + 2 * SSD_GROUPS * SSD_STATE
    sizes = (att_w, kv_w, kv_w, qi_w, IDX_DIM, IDX_HEADS, ssd_w, xbc_w, SSD_HEADS)
    offs = [0]
    for sz in sizes:
        offs.append(offs[-1] + sz)
    tt = 128
    xf = x.reshape(bsz * seq, d)
    for l in range(depth):
        w = w_in[l].astype(BF16)
        zpad = lambda n: jnp.zeros((d, n), BF16)
        w_qkv = w[:, offs[0]:offs[3]]
        w_idx = jnp.concatenate([w[:, offs[3]:offs[6]], zpad(LANES - IDX_DIM - IDX_HEADS)], axis=1)
        w_ssd = jnp.concatenate([w[:, offs[7]:offs[8]], w[:, offs[6]:offs[7]], w[:, offs[8]:offs[9]],
                                 zpad(2 * LANES - SSD_HEADS)], axis=1)
        qkv = _matmul(xf, w_qkv, BF16, 512, 512)
        idx = _matmul(xf, w_idx, F32, 512, 384)
        ssd_in = _matmul(xf, w_ssd, F32, 512, 256)
        att = _dsa_attention(qkv, idx, idx_kn_g[l], idx_kn_b[l], bsz, seq)
        ssd = _ssd_mixer(ssd_in, conv_w[l], conv_b[l], dt_bias[l], a_log[l], d_skip[l], ssd_norm_g[l], bsz, seq)
        mixed = _matmul(jnp.concatenate([att, ssd], axis=1), w_out[l].astype(BF16), F32, 1024, 512)
        x1, x1p, route = _ln_router(xf, mixed, ln1_g[l], ln1_b[l], w_rg[l], b_rg[l], w_re[l], b_re[l], alpha)
        expert_id = route[:, 0:2].astype(I32)
        row_tok, blk_e, blk_valid, n_used, dest_tiles = _dispatch_plan(expert_id, tt)
        y_rows = _moe_ffn(x1p, row_tok, blk_e, blk_valid, n_used, w_gate[l], w_up[l], w_down[l])
        xf = _combine_ln(y_rows, dest_tiles, x1, route, ln2_g[l], ln2_b[l], alpha, tt)
    return xf.reshape(bsz, seq, d)
```

```python
import functools

import jax
import jax.numpy as jnp
from jax import lax
from jax.experimental import pallas as pl
from jax.experimental.pallas import tpu as pltpu

F32 = jnp.float32
BF16 = jnp.bfloat16
I32 = jnp.int32

HEAD_DIM = 128
KV_HEADS = 4
GQA_GROUP = 4
ATT_HEADS = KV_HEADS * GQA_GROUP
IDX_HEADS = 16
IDX_DIM = 64
DSA_TOPK_MAX = 256
QUERY_BLOCK = 128
SSD_HEAD_DIM = 64
SSD_GROUPS = 8
SSD_HEADS_PER_GROUP = 4
SSD_HEADS = SSD_GROUPS * SSD_HEADS_PER_GROUP
SSD_STATE = 128
SSD_CONV = 4
SSD_CHUNK = 128
N_EXPERT_GROUPS = 8
EXPERTS_PER_GROUP = 8
N_EXPERTS = 64
LN_EPS = 1e-5
RMS_EPS = 1e-5

LANES = 128
SUBLANES = 8
VMEM_LIMIT = 56 * 1024 * 1024

KEY_CHUNK = 512
MOE_ROWS = 1024
MOE_SUB = 256
MOE_F_CHUNK = 256
MOE_N_CHUNK = 256
MOE_K_SPLIT = 4
MOE_KD_SPLIT = 2
NEG_BIG = -1e30
INT_MIN = -2 ** 31


def _params(sem):
    return pltpu.CompilerParams(dimension_semantics=sem, vmem_limit_bytes=VMEM_LIMIT)


def _mm_cast_kernel(a_ref, b_ref, o_ref, abf_ref):
    @pl.when(pl.program_id(1) == 0)
    def _():
        abf_ref[...] = a_ref[...].astype(BF16)

    o_ref[...] = jnp.dot(abf_ref[...], b_ref[...], preferred_element_type=F32).astype(o_ref.dtype)


def _mm_kernel(a_ref, b_ref, o_ref):
    o_ref[...] = jnp.dot(a_ref[...], b_ref[...], preferred_element_type=F32).astype(o_ref.dtype)


def _matmul(a, b, out_dtype, tm, tn):
    m, k = a.shape
    n = b.shape[1]
    tm = min(tm, m)
    assert m % tm == 0 and n % tn == 0
    cast = a.dtype != BF16
    out_specs = pl.BlockSpec((tm, tn), lambda i, j: (i, j))
    out_shape = jax.ShapeDtypeStruct((m, n), out_dtype)
    if cast:
        out_specs = [out_specs, pl.BlockSpec((tm, k), lambda i, j: (i, 0))]
        out_shape = [out_shape, jax.ShapeDtypeStruct((m, k), BF16)]
    return pl.pallas_call(
        _mm_cast_kernel if cast else _mm_kernel,
        grid=(m // tm, n // tn),
        in_specs=[pl.BlockSpec((tm, k), lambda i, j: (i, 0)),
                  pl.BlockSpec((k, tn), lambda i, j: (0, j))],
        out_specs=out_specs,
        out_shape=out_shape,
        compiler_params=_params(("parallel", "arbitrary")),
        name="matmul_cast" if cast else "matmul",
    )(a, b)


def _attn_kernel(q_ref, k_ref, v_ref, qi_ref, kw_ref, g_ref, b_ref, o_ref,
                 kln_ref, key_ref, bias_ref, s_ref, mrun_ref, lrun_ref, acc_ref, *, top_k):
    i = pl.program_id(1)
    tq = QUERY_BLOCK
    ck = KEY_CHUNK

    @pl.when(i == 0)
    def _():
        kx = kw_ref[:, 0:IDX_DIM]
        mu = jnp.mean(kx, axis=-1, keepdims=True)
        var = jnp.mean(jnp.square(kx - mu), axis=-1, keepdims=True)
        y = (kx - mu) * lax.rsqrt(var + LN_EPS)
        kln_ref[...] = (y * g_ref[...] + b_ref[...]).astype(BF16)

    q_start = i * tq
    n_chunks = (q_start + tq + ck - 1) // ck
    q_pos = q_start + lax.broadcasted_iota(I32, (tq, 1), 0)
    w = kw_ref[pl.ds(pl.multiple_of(q_start, tq), tq), IDX_DIM:IDX_DIM + IDX_HEADS]
    w = w * (IDX_HEADS ** -0.5 * IDX_DIM ** -0.5)
    qi = jnp.concatenate([qi_ref[:, h * IDX_DIM:(h + 1) * IDX_DIM] for h in range(IDX_HEADS)], axis=0).astype(BF16)

    def chunk_off(c):
        return pl.multiple_of(c * ck, ck)

    def key_pos(c):
        return c * ck + lax.broadcasted_iota(I32, (1, ck), 1)

    def score_chunk(c, carry):
        off = chunk_off(c)
        kc = kln_ref[pl.ds(off, ck), :]
        d = lax.dot_general(qi, kc, (((1,), (1,)), ((), ())), preferred_element_type=F32)
        acc = jnp.zeros((tq, ck), F32)
        for h in range(IDX_HEADS):
            acc = acc + jnp.maximum(d[h * tq:(h + 1) * tq, :], 0.0) * w[:, h:h + 1]
        acc = jnp.where(key_pos(c) <= q_pos, acc, -jnp.inf)
        bits = pltpu.bitcast(acc, I32)
        key_ref[:, pl.ds(off, ck)] = bits ^ ((bits >> 31) & 0x7FFFFFFF)
        return carry

    lax.fori_loop(0, n_chunks, score_chunk, 0)

    def bit_body(b, cand):
        trial = cand | lax.shift_left(jnp.int32(1), jnp.int32(31) - jnp.asarray(b, I32))
        trial_b = jnp.broadcast_to(trial ^ INT_MIN, (tq, LANES))

        def cnt_chunk(c, cnt):
            kc = key_ref[:, pl.ds(chunk_off(c), ck)]
            for s in range(ck // LANES):
                cnt = cnt + jnp.where(kc[:, s * LANES:(s + 1) * LANES] >= trial_b, 1.0, 0.0)
            return cnt

        cnt = lax.fori_loop(0, n_chunks, cnt_chunk, jnp.zeros((tq, LANES), F32))
        total = jnp.sum(cnt, axis=1, keepdims=True)
        return jnp.where(total >= float(top_k), trial, cand)

    cand = lax.fori_loop(0, 32, bit_body, jnp.zeros((tq, 1), I32))
    thr = cand ^ INT_MIN

    def bias_chunk(c, carry):
        off = chunk_off(c)
        sel = (key_ref[:, pl.ds(off, ck)] >= thr) & (key_pos(c) <= q_pos)
        bias_ref[:, pl.ds(off, ck)] = jnp.where(sel, 0.0, NEG_BIG)
        return carry

    lax.fori_loop(0, n_chunks, bias_chunk, 0)

    scale = HEAD_DIM ** -0.5
    gq = GQA_GROUP
    for g in range(KV_HEADS):
        qg = jnp.concatenate([q_ref[:, (g * gq + j) * HEAD_DIM:(g * gq + j + 1) * HEAD_DIM] for j in range(gq)],
                             axis=0)

        mrun_ref[...] = jnp.full(mrun_ref.shape, NEG_BIG, F32)

        def logit_chunk(c, carry, g=g, qg=qg):
            off = chunk_off(c)
            kc = k_ref[pl.ds(off, ck), g * HEAD_DIM:(g + 1) * HEAD_DIM]
            s = lax.dot_general(qg, kc, (((1,), (1,)), ((), ())), preferred_element_type=F32)
            bias = bias_ref[:, pl.ds(off, ck)]
            s = s * scale + jnp.concatenate([bias] * gq, axis=0)
            s_ref[:, pl.ds(off, ck)] = s
            m = mrun_ref[...]
            for t in range(ck // LANES):
                m = jnp.maximum(m, s[:, t * LANES:(t + 1) * LANES])
            mrun_ref[...] = m
            return carry

        lax.fori_loop(0, n_chunks, logit_chunk, 0)
        m = jnp.max(mrun_ref[...], axis=1, keepdims=True)

        lrun_ref[...] = jnp.zeros(lrun_ref.shape, F32)
        acc_ref[...] = jnp.zeros(acc_ref.shape, F32)

        def prob_chunk(c, carry, g=g, m=m):
            off = chunk_off(c)
            vc = v_ref[pl.ds(off, ck), g * HEAD_DIM:(g + 1) * HEAD_DIM]
            p = jnp.exp(s_ref[:, pl.ds(off, ck)] - m)
            l = lrun_ref[...]
            for t in range(ck // LANES):
                l = l + p[:, t * LANES:(t + 1) * LANES]
            lrun_ref[...] = l
            acc_ref[...] += jnp.dot(p.astype(BF16), vc, preferred_element_type=F32)
            return carry

        lax.fori_loop(0, n_chunks, prob_chunk, 0)
        out = acc_ref[...] / jnp.sum(lrun_ref[...], axis=1, keepdims=True)
        for j in range(gq):
            h = g * gq + j
            o_ref[:, h * HEAD_DIM:(h + 1) * HEAD_DIM] = out[j * tq:(j + 1) * tq, :].astype(o_ref.dtype)


def _dsa_attention(qkv, idx, kn_g, kn_b, bsz, seq):
    top_k = min(DSA_TOPK_MAX, seq // 4)
    nq = seq // QUERY_BLOCK
    att_w = ATT_HEADS * HEAD_DIM
    kv_w = KV_HEADS * HEAD_DIM
    qi_w = IDX_HEADS * IDX_DIM
    assert seq % KEY_CHUNK == 0 and att_w % kv_w == 0 and qi_w % LANES == 0
    return pl.pallas_call(
        functools.partial(_attn_kernel, top_k=top_k),
        grid=(bsz, nq),
        in_specs=[pl.BlockSpec((QUERY_BLOCK, att_w), lambda b, i: (b * nq + i, 0)),
                  pl.BlockSpec((seq, kv_w), lambda b, i: (b, att_w // kv_w)),
                  pl.BlockSpec((seq, kv_w), lambda b, i: (b, att_w // kv_w + 1)),
                  pl.BlockSpec((QUERY_BLOCK, qi_w), lambda b, i: (b * nq + i, 0)),
                  pl.BlockSpec((seq, LANES), lambda b, i: (b, qi_w // LANES)),
                  pl.BlockSpec((1, IDX_DIM), lambda b, i: (0, 0)),
                  pl.BlockSpec((1, IDX_DIM), lambda b, i: (0, 0))],
        out_specs=pl.BlockSpec((QUERY_BLOCK, att_w), lambda b, i: (b * nq + i, 0)),
        out_shape=jax.ShapeDtypeStruct((bsz * seq, att_w), BF16),
        scratch_shapes=[pltpu.VMEM((seq, IDX_DIM), BF16),
                        pltpu.VMEM((QUERY_BLOCK, seq), I32),
                        pltpu.VMEM((QUERY_BLOCK, seq), F32),
                        pltpu.VMEM((GQA_GROUP * QUERY_BLOCK, seq), F32),
                        pltpu.VMEM((GQA_GROUP * QUERY_BLOCK, LANES), F32),
                        pltpu.VMEM((GQA_GROUP * QUERY_BLOCK, LANES), F32),
                        pltpu.VMEM((GQA_GROUP * QUERY_BLOCK, HEAD_DIM), F32)],
        compiler_params=_params(("parallel", "arbitrary")),
        name="dsa_attention",
    )(qkv, qkv, qkv, idx, idx, kn_g.reshape(1, IDX_DIM), kn_b.reshape(1, IDX_DIM))


def _silu(x):
    return x / (1.0 + jnp.exp(-x))


def _ssd_kernel(xbc_ref, z_ref, dt_ref, cw_ref, cb_ref, dtb_ref, alog_ref, dsk_ref, ng_ref, o_ref,
                xpad_ref, act_ref, y_ref, h_ref):
    c = pl.program_id(1)
    cq = SSD_CHUNK
    width = SSD_HEADS * SSD_HEAD_DIM
    b_off = width
    c_off = width + SSD_GROUPS * SSD_STATE

    @pl.when(c == 0)
    def _():
        xpad_ref[0:SUBLANES, :] = jnp.zeros((SUBLANES, xpad_ref.shape[1]), F32)
        h_ref[...] = jnp.zeros(h_ref.shape, F32)

    xpad_ref[SUBLANES:SUBLANES + cq, :] = xbc_ref[...]
    col = 512
    for j in range(xpad_ref.shape[1] // col):
        cs = slice(j * col, (j + 1) * col)
        acc = cb_ref[:, cs] + jnp.zeros((cq, col), F32)
        for t in range(SSD_CONV):
            r0 = SUBLANES - (SSD_CONV - 1) + t
            acc = acc + xpad_ref[r0:r0 + cq, cs] * cw_ref[t:t + 1, cs]
        act_ref[:, cs] = _silu(acc)
    xpad_ref[0:SUBLANES, :] = xpad_ref[cq:cq + SUBLANES, :]

    xdt_in = dt_ref[:, 0:LANES] + dtb_ref[...]
    dt = jnp.maximum(xdt_in, 0.0) + jnp.log1p(jnp.exp(-jnp.abs(xdt_in)))
    da = dt * (-jnp.exp(alog_ref[...]))
    row = lax.broadcasted_iota(I32, (cq, cq), 0)
    coli = lax.broadcasted_iota(I32, (cq, cq), 1)
    causal = row >= coli
    tril = causal.astype(F32)
    acs = jnp.dot(tril, da, preferred_element_type=F32, precision=lax.Precision.HIGHEST)
    acs_t = acs.T
    dec_in = jnp.exp(acs)
    a_last = acs[cq - 1:cq, :]
    dec_out = jnp.exp(a_last - acs)
    dec_chunk = jnp.exp(a_last)

    for g in range(SSD_GROUPS):
        bg = act_ref[:, b_off + g * SSD_STATE:b_off + (g + 1) * SSD_STATE].astype(BF16)
        cg = act_ref[:, c_off + g * SSD_STATE:c_off + (g + 1) * SSD_STATE].astype(BF16)
        cbm = lax.dot_general(cg, bg, (((1,), (1,)), ((), ())), preferred_element_type=F32)
        for j in range(SSD_HEADS_PER_GROUP):
            hd = g * SSD_HEADS_PER_GROUP + j
            xs = slice(hd * SSD_HEAD_DIM, (hd + 1) * SSD_HEAD_DIM)
            seg = acs[:, hd:hd + 1] - acs_t[hd:hd + 1, :]
            lmat = jnp.exp(jnp.where(causal, seg, -jnp.inf))
            xh = act_ref[:, xs]
            xdt = xh * dt[:, hd:hd + 1]
            y = jnp.dot((cbm * lmat).astype(BF16), xdt.astype(BF16), preferred_element_type=F32)
            hprev = h_ref[hd]
            yoff = lax.dot_general(cg, hprev.astype(BF16), (((1,), (1,)), ((), ())),
                                   preferred_element_type=F32)
            y = y + yoff * dec_in[:, hd:hd + 1] + xh * dsk_ref[:, xs]
            y_ref[:, xs] = y
            st = lax.dot_general((xdt * dec_out[:, hd:hd + 1]).astype(BF16), bg,
                                 (((0,), (0,)), ((), ())), preferred_element_type=F32)
            h_ref[hd] = hprev * dec_chunk[:, hd:hd + 1] + st

    gw = width // SSD_GROUPS
    for g in range(SSD_GROUPS):
        gs = slice(g * gw, (g + 1) * gw)
        gated = y_ref[:, gs] * _silu(z_ref[:, gs])
        ms = jnp.mean(jnp.square(gated), axis=-1, keepdims=True)
        o_ref[:, gs] = (gated * lax.rsqrt(ms + RMS_EPS) * ng_ref[:, gs]).astype(o_ref.dtype)


def _ssd_mixer(ssd_in, conv_w, conv_b, dt_bias, a_log, d_skip, norm_g, bsz, seq):
    nc = seq // SSD_CHUNK
    width = SSD_HEADS * SSD_HEAD_DIM
    xbc_w = width + 2 * SSD_GROUPS * SSD_STATE
    dt_w = ssd_in.shape[1] - xbc_w - width
    assert xbc_w % width == 0 and (xbc_w + width) % dt_w == 0 and dt_w >= LANES
    pad = LANES - SSD_HEADS
    dtb = jnp.pad(dt_bias.astype(F32), (0, pad)).reshape(1, LANES)
    alog = jnp.pad(a_log.astype(F32), (0, pad)).reshape(1, LANES)
    dsk = jnp.repeat(d_skip.astype(F32), SSD_HEAD_DIM).reshape(1, width)
    const = lambda b, c: (0, 0)
    return pl.pallas_call(
        _ssd_kernel,
        grid=(bsz, nc),
        in_specs=[pl.BlockSpec((SSD_CHUNK, xbc_w), lambda b, c: (b * nc + c, 0)),
                  pl.BlockSpec((SSD_CHUNK, width), lambda b, c: (b * nc + c, xbc_w // width)),
                  pl.BlockSpec((SSD_CHUNK, dt_w), lambda b, c: (b * nc + c, (xbc_w + width) // dt_w)),
                  pl.BlockSpec((SSD_CONV, xbc_w), const),
                  pl.BlockSpec((1, xbc_w), const),
                  pl.BlockSpec((1, LANES), const),
                  pl.BlockSpec((1, LANES), const),
                  pl.BlockSpec((1, width), const),
                  pl.BlockSpec((1, width), const)],
        out_specs=pl.BlockSpec((SSD_CHUNK, width), lambda b, c: (b * nc + c, 0)),
        out_shape=jax.ShapeDtypeStruct((bsz * seq, width), BF16),
        scratch_shapes=[pltpu.VMEM((SSD_CHUNK + SUBLANES, xbc_w), F32),
                        pltpu.VMEM((SSD_CHUNK, xbc_w), F32),
                        pltpu.VMEM((SSD_CHUNK, width), F32),
                        pltpu.VMEM((SSD_HEADS, SSD_HEAD_DIM, SSD_STATE), F32)],
        compiler_params=_params(("parallel", "arbitrary")),
        name="ssd_mixer",
    )(ssd_in, ssd_in, ssd_in, conv_w.astype(F32), conv_b.astype(F32).reshape(1, xbc_w), dtb, alog, dsk,
      norm_g.astype(F32).reshape(1, width))


def _layer_norm_rows(x, g, b):
    mu = jnp.mean(x, axis=-1, keepdims=True)
    var = jnp.mean(jnp.square(x - mu), axis=-1, keepdims=True)
    return (x - mu) * lax.rsqrt(var + LN_EPS) * g + b


def _ln_router_kernel(x_ref, mix_ref, g_ref, b_ref, wr_ref, br_ref, o_ref, xp_ref, r_ref, *, alpha):
    x1 = _layer_norm_rows(alpha * x_ref[...] + mix_ref[...], g_ref[...], b_ref[...])
    o_ref[...] = x1
    half = xp_ref.shape[1]
    bits = pltpu.bitcast(x1.astype(BF16).astype(F32), jnp.uint32)
    xp_ref[...] = (bits[:, :half] >> 16) | bits[:, half:]
    logits = jnp.dot(x1, wr_ref[...], preferred_element_type=F32, precision=lax.Precision.HIGHEST)
    logits = logits + br_ref[...]
    rows = logits.shape[0]
    lane = lax.broadcasted_iota(I32, (rows, LANES), 1).astype(F32)
    ng, epg = float(N_EXPERT_GROUPS), float(EXPERTS_PER_GROUP)
    far = float(LANES)

    gmask = lane < ng
    gl = jnp.where(gmask, logits, -jnp.inf)
    ge = jnp.exp(gl - jnp.max(gl, axis=1, keepdims=True))
    gprob = ge / jnp.sum(ge, axis=1, keepdims=True)
    gprob = jnp.where(gmask, gprob, -1.0)
    gw = jnp.max(gprob, axis=1, keepdims=True)
    gsel = jnp.min(jnp.where(gprob == gw, lane, far), axis=1, keepdims=True)

    e_lo = ng + gsel * epg
    emask = (lane >= e_lo) & (lane < e_lo + epg)
    el = jnp.where(emask, logits, -jnp.inf)
    v0 = jnp.max(el, axis=1, keepdims=True)
    i0 = jnp.min(jnp.where(emask & (el == v0), lane, far), axis=1, keepdims=True)
    emask1 = emask & (lane != i0)
    el1 = jnp.where(emask1, logits, -jnp.inf)
    v1 = jnp.max(el1, axis=1, keepdims=True)
    i1 = jnp.min(jnp.where(emask1 & (el1 == v1), lane, far), axis=1, keepdims=True)
    e = jnp.exp(v1 - v0)
    p0 = 1.0 / (1.0 + e)
    p1 = e / (1.0 + e)
    out = jnp.where(lane == 0.0, i0 - ng,
                    jnp.where(lane == 1.0, i1 - ng,
                              jnp.where(lane == 2.0, gw * p0, jnp.where(lane == 3.0, gw * p1, 0.0))))
    r_ref[...] = out


def _ln_router(x, mixed, g, b, w_rg, b_rg, w_re, b_re, alpha, tr=256):
    t, d = x.shape
    ncol = N_EXPERT_GROUPS + N_EXPERTS
    wr = jnp.pad(jnp.concatenate([w_rg, w_re], axis=1).astype(F32), ((0, 0), (0, LANES - ncol)))
    br = jnp.pad(jnp.concatenate([b_rg, b_re]).astype(F32), (0, LANES - ncol)).reshape(1, LANES)
    const = lambda i: (0, 0)
    return pl.pallas_call(
        functools.partial(_ln_router_kernel, alpha=alpha),
        grid=(t // tr,),
        in_specs=[pl.BlockSpec((tr, d), lambda i: (i, 0)),
                  pl.BlockSpec((tr, d), lambda i: (i, 0)),
                  pl.BlockSpec((1, d), const), pl.BlockSpec((1, d), const),
                  pl.BlockSpec((d, LANES), const), pl.BlockSpec((1, LANES), const)],
        out_specs=[pl.BlockSpec((tr, d), lambda i: (i, 0)),
                   pl.BlockSpec((tr, d // 2), lambda i: (i, 0)),
                   pl.BlockSpec((tr, LANES), lambda i: (i, 0))],
        out_shape=[jax.ShapeDtypeStruct((t, d), F32), jax.ShapeDtypeStruct((t, d // 2), jnp.uint32),
                   jax.ShapeDtypeStruct((t, LANES), F32)],
        compiler_params=_params(("parallel",)),
        name="ln1_router",
    )(x, mixed, g.astype(F32).reshape(1, d), b.astype(F32).reshape(1, d), wr, br)


def _row_copy(src_ref, dst_ref, sem, tok, row):
    return pltpu.make_async_copy(src_ref.at[pl.ds(tok, 1)], dst_ref.at[pl.ds(row, 1)], sem)


def _moe_kernel(tok_ref, be_ref, bv_ref, nu_ref, x_ref, *refs, nf, ks, kd):
    wg, wu = refs[0:ks], refs[ks:2 * ks]
    wd_lo, wd_hi = refs[2 * ks:2 * ks + kd], refs[2 * ks + kd:2 * ks + 2 * kd]
    o_ref, xq_ref, xbf_ref, act_ref, sem = refs[2 * ks + 2 * kd:]
    s = pl.program_id(0)
    p = pl.program_id(1)
    n_blk = pl.num_programs(0)
    sub = MOE_SUB
    n_sub = MOE_ROWS // sub
    half = xq_ref.shape[1]

    def nsub_of(blk):
        return (bv_ref[blk] + (sub - 1)) // sub

    def issue(blk, r):
        base = blk * MOE_ROWS + r * sub

        def body(i, carry):
            _row_copy(x_ref, xq_ref, sem, tok_ref[base + i], r * sub + i).start()
            return carry
        lax.fori_loop(0, sub, body, 0)

    nsub = nsub_of(s)

    @pl.when((s == 0) & (p == 0))
    def _():
        for r in range(n_sub):
            @pl.when(r < nsub)
            def _(r=r):
                issue(s, r)

    @pl.when(p == 0)
    def _():
        def wait_body(i, carry):
            _row_copy(x_ref, xq_ref, sem, 0, i).wait()
            return carry
        lax.fori_loop(0, nsub * sub, wait_body, 0)
        for r in range(n_sub):
            @pl.when(r < nsub)
            def _(r=r):
                rs = slice(r * sub, (r + 1) * sub)
                word = xq_ref[rs, :]
                xbf_ref[rs, 0:half] = pltpu.bitcast(word << 16, F32).astype(BF16)
                xbf_ref[rs, half:2 * half] = pltpu.bitcast(word & jnp.uint32(0xFFFF0000), F32).astype(BF16)

    nxt = jnp.minimum(s + 1, n_blk - 1)
    for r in range(n_sub):
        @pl.when((p == r + 1) & (s + 1 < n_blk) & (r < nsub_of(nxt)))
        def _(r=r):
            issue(nxt, r)

    for k in range(1, n_sub + 1):
        m = k * sub

        @pl.when((p < nf) & (nsub == k))
        def _(m=m):
            kq = xbf_ref.shape[1] // ks
            gate = up = None
            for q in range(ks):
                xr = xbf_ref[0:m, q * kq:(q + 1) * kq]
                gq = jnp.dot(xr, wg[q][...].astype(BF16), preferred_element_type=F32)
                uq = jnp.dot(xr, wu[q][...].astype(BF16), preferred_element_type=F32)
                gate = gq if gate is None else gate + gq
                up = uq if up is None else up + uq
            act_ref[p, 0:m, :] = (_silu(gate) * up).astype(BF16)

    per = nf // kd

    def down(parts, m):
        acc = None
        for q in range(nf):
            r0 = (q % per) * MOE_F_CHUNK
            wq = parts[q // per][r0:r0 + MOE_F_CHUNK, :].astype(BF16)
            t = jnp.dot(act_ref[q, 0:m, :], wq, preferred_element_type=F32)
            acc = t if acc is None else acc + t
        return pltpu.bitcast(acc.astype(BF16).astype(F32), jnp.uint32)

    for k in range(0, n_sub + 1):
        m = k * sub

        @pl.when((p >= nf) & (nsub == k))
        def _(m=m):
            if m > 0:
                o_ref[0:m, :] = (down(wd_lo, m) >> 16) | down(wd_hi, m)
            if m < MOE_ROWS:
                o_ref[m:MOE_ROWS, :] = jnp.zeros((MOE_ROWS - m, o_ref.shape[1]), jnp.uint32)


def _moe_ffn(xp, row_tok, blk_e, blk_valid, n_used, w_gate, w_up, w_down):
    n_rows = row_tok.shape[0]
    n_e, d, f = w_gate.shape
    ks, kd = MOE_K_SPLIT, MOE_KD_SPLIT
    nf = f // MOE_F_CHUNK
    nn = (d // 2) // MOE_N_CHUNK
    n_blk = n_rows // MOE_ROWS
    assert nf + nn > MOE_ROWS // MOE_SUB and nf % kd == 0 and d % ks == 0

    def gu_spec(q):
        return pl.BlockSpec((None, d // ks, MOE_F_CHUNK), lambda s, p, tok, be, bv, nu: (
            be[s], q, jnp.where(s < nu[0], jnp.minimum(p, nf - 1), nf - 1)))

    def dn_col(s, p, nu):
        return jnp.where(s < nu[0], jnp.maximum(p - nf, 0), nn - 1)

    def out_map(s, p, tok, be, bv, nu):
        return (s, jnp.maximum(p - nf, 0))

    def dn_spec(q, hi):
        return pl.BlockSpec((None, f // kd, MOE_N_CHUNK), lambda s, p, tok, be, bv, nu: (
            be[s], q, dn_col(s, p, nu) + hi * nn))

    in_specs = ([pl.BlockSpec(memory_space=pl.ANY)] + [gu_spec(q) for q in range(ks)] + [gu_spec(q) for q in range(ks)]
                + [dn_spec(q, 0) for q in range(kd)] + [dn_spec(q, 1) for q in range(kd)])
    return pl.pallas_call(
        functools.partial(_moe_kernel, nf=nf, ks=ks, kd=kd),
        grid_spec=pltpu.PrefetchScalarGridSpec(
            num_scalar_prefetch=4,
            grid=(n_blk, nf + nn),
            in_specs=in_specs,
            out_specs=pl.BlockSpec((MOE_ROWS, MOE_N_CHUNK), out_map),
            scratch_shapes=[pltpu.VMEM((MOE_ROWS, d // 2), jnp.uint32),
                            pltpu.VMEM((MOE_ROWS, d), BF16),
                            pltpu.VMEM((nf, MOE_ROWS, MOE_F_CHUNK), BF16),
                            pltpu.SemaphoreType.DMA(())]),
        out_shape=jax.ShapeDtypeStruct((n_rows, d // 2), jnp.uint32),
        compiler_params=_params(("arbitrary", "arbitrary")),
        name="moe_ffn",
    )(row_tok, blk_e, blk_valid, n_used, xp, *([w_gate] * ks), *([w_up] * ks), *([w_down] * (2 * kd)))


def _combine_kernel(dest_ref, y_ref, x_ref, r_ref, g_ref, b_ref, o_ref, buf_ref, sem, *, alpha, tt):
    s = pl.program_id(0)
    n = pl.num_programs(0)

    def copy(blk, r):
        slot = blk % 2
        return pltpu.make_async_copy(y_ref.at[pl.ds(dest_ref[blk * 2 * tt + r], 1)],
                                     buf_ref.at[slot, pl.ds(r, 1)], sem.at[slot])

    def start(blk):
        def body(r, carry):
            copy(blk, r).start()
            return carry
        lax.fori_loop(0, 2 * tt, body, 0)

    def wait(blk):
        def body(r, carry):
            copy(blk, r).wait()
            return carry
        lax.fori_loop(0, 2 * tt, body, 0)

    @pl.when(s == 0)
    def _():
        start(s)

    @pl.when(s + 1 < n)
    def _():
        start(s + 1)

    wait(s)
    slot = s % 2
    g0 = r_ref[:, 2:3]
    g1 = r_ref[:, 3:4]
    w0 = buf_ref[slot, 0:tt, :]
    w1 = buf_ref[slot, tt:2 * tt, :]
    half = w0.shape[1]
    himask = jnp.uint32(0xFFFF0000)
    y_lo = g0 * pltpu.bitcast(w0 << 16, F32) + g1 * pltpu.bitcast(w1 << 16, F32)
    y_hi = g0 * pltpu.bitcast(w0 & himask, F32) + g1 * pltpu.bitcast(w1 & himask, F32)
    r_lo = alpha * x_ref[:, 0:half] + y_lo
    r_hi = alpha * x_ref[:, half:2 * half] + y_hi
    inv_d = 1.0 / (2 * half)
    mu = (jnp.sum(r_lo, axis=-1, keepdims=True) + jnp.sum(r_hi, axis=-1, keepdims=True)) * inv_d
    var = (jnp.sum(jnp.square(r_lo - mu), axis=-1, keepdims=True)
           + jnp.sum(jnp.square(r_hi - mu), axis=-1, keepdims=True)) * inv_d
    rstd = lax.rsqrt(var + LN_EPS)
    o_ref[:, 0:half] = (r_lo - mu) * rstd * g_ref[:, 0:half] + b_ref[:, 0:half]
    o_ref[:, half:2 * half] = (r_hi - mu) * rstd * g_ref[:, half:2 * half] + b_ref[:, half:2 * half]


def _combine_ln(y_rows, dest, x1, route, g, b, alpha, tt=128):
    t, d = x1.shape
    const = lambda i, dr: (0, 0)
    return pl.pallas_call(
        functools.partial(_combine_kernel, alpha=alpha, tt=tt),
        grid_spec=pltpu.PrefetchScalarGridSpec(
            num_scalar_prefetch=1,
            grid=(t // tt,),
            in_specs=[pl.BlockSpec(memory_space=pl.ANY),
                      pl.BlockSpec((tt, d), lambda i, dr: (i, 0)),
                      pl.BlockSpec((tt, LANES), lambda i, dr: (i, 0)),
                      pl.BlockSpec((1, d), const), pl.BlockSpec((1, d), const)],
            out_specs=pl.BlockSpec((tt, d), lambda i, dr: (i, 0)),
            scratch_shapes=[pltpu.VMEM((2, 2 * tt, d // 2), jnp.uint32),
                            pltpu.SemaphoreType.DMA((2,))]),
        out_shape=jax.ShapeDtypeStruct((t, d), F32),
        compiler_params=_params(("arbitrary",)),
        name="moe_combine_ln2",
    )(dest, y_rows, x1, route, g.astype(F32).reshape(1, d), b.astype(F32).reshape(1, d))


def _dispatch_plan(expert_id, tt):
    t = expert_id.shape[0]
    m = 2 * t
    e_flat = expert_id.reshape(m)
    onehot = (e_flat[:, None] == jnp.arange(N_EXPERTS, dtype=I32)[None, :]).astype(I32)
    counts = jnp.sum(onehot, axis=0)
    rank = jnp.sum((jnp.cumsum(onehot, axis=0) - onehot) * onehot, axis=1)
    pcounts = (counts + MOE_ROWS - 1) // MOE_ROWS * MOE_ROWS
    pends = jnp.cumsum(pcounts)
    pstarts = pends - pcounts
    dest = (pstarts[e_flat] + rank).astype(I32)
    n_blk = -(-m // MOE_ROWS) + N_EXPERTS
    n_rows = n_blk * MOE_ROWS
    tok = jnp.arange(m, dtype=I32) // 2
    row_tok = jnp.zeros((n_rows,), I32).at[dest].set(tok)
    n_used = (pends[-1] // MOE_ROWS).astype(I32)
    blk_idx = jnp.arange(n_blk, dtype=I32)
    blk_start = jnp.minimum(blk_idx, n_used - 1) * MOE_ROWS
    blk_e = jnp.minimum(jnp.searchsorted(pends, blk_start, side='right'), N_EXPERTS - 1).astype(I32)
    blk_valid = jnp.clip(counts[blk_e] - (blk_start - pstarts[blk_e]), 0, MOE_ROWS)
    blk_valid = jnp.where(blk_idx < n_used, blk_valid, 0).astype(I32)
    dest_tiles = dest.reshape(t // tt, tt, 2).transpose(0, 2, 1).reshape(m)
    return row_tok, blk_e, blk_valid, n_used.reshape(1), dest_tiles


def kernel(x, w_in, idx_kn_g, idx_kn_b, conv_w, conv_b, dt_bias, a_log, d_skip, ssd_norm_g, w_out,
           ln1_g, ln1_b, w_rg, b_rg, w_re, b_re, w_gate, w_up, w_down, ln2_g, ln2_b):
    bsz, seq, d = x.shape
    depth = w_in.shape[0]
    alpha = (2 * depth) ** 0.25
    att_w = ATT_HEADS * HEAD_DIM
    kv_w = KV_HEADS * HEAD_DIM
    qi_w = IDX_HEADS * IDX_DIM
    ssd_w = SSD_HEADS * SSD_HEAD_DIM
    xbc_w = ssd_w + 2 * SSD_GROUPS * SSD_STATE
    sizes = (att_w, kv_w, kv_w, qi_w, IDX_DIM, IDX_HEADS, ssd_w, xbc_w, SSD_HEADS)
    offs = [0]
    for sz in sizes:
        offs.append(offs[-1] + sz)
    tt = 128
    xf = x.reshape(bsz * seq, d)
    for l in range(depth):
        w = w_in[l].astype(BF16)
        zpad = lambda n: jnp.zeros((d, n), BF16)
        w_qkv = w[:, offs[0]:offs[3]]
        w_idx = jnp.concatenate([w[:, offs[3]:offs[6]], zpad(LANES - IDX_DIM - IDX_HEADS)], axis=1)
        w_ssd = jnp.concatenate([w[:, offs[7]:offs[8]], w[:, offs[6]:offs[7]], w[:, offs[8]:offs[9]],
                                 zpad(2 * LANES - SSD_HEADS)], axis=1)
        qkv, xbf = _matmul(xf, w_qkv, BF16, 512, 512)
        idx = _matmul(xbf, w_idx, F32, 1024, 384)
        ssd_in = _matmul(xbf, w_ssd, F32, 1024, 256)
        att = _dsa_attention(qkv, idx, idx_kn_g[l], idx_kn_b[l], bsz, seq)
        ssd = _ssd_mixer(ssd_in, conv_w[l], conv_b[l], dt_bias[l], a_log[l], d_skip[l], ssd_norm_g[l], bsz, seq)
        mixed = _matmul(jnp.concatenate([att, ssd], axis=1), w_out[l].astype(BF16), F32, 1024, 512)
        x1, x1p, route = _ln_router(xf, mixed, ln1_g[l], ln1_b[l], w_rg[l], b_rg[l], w_re[l], b_re[l], alpha)
        expert_id = route[:, 0:2].astype(I32)
        row_tok, blk_e, blk_valid, n_used, dest_tiles = _dispatch_plan(expert_id, tt)
        y_rows = _moe_ffn(x1p, row_tok, blk_e, blk_valid, n_used, w_gate[l], w_up[l], w_down[l])
        xf = _combine_ln(y_rows, dest_tiles, x1, route, ln2_g[l], ln2_b[l], alpha, tt)
    return xf.reshape(bsz, seq, d)
```

```python
import functools

import jax
import jax.numpy as jnp
from jax import lax
from jax.experimental import pallas as pl
from jax.experimental.pallas import tpu as pltpu

F32 = jnp.float32
BF16 = jnp.bfloat16
I32 = jnp.int32

HEAD_DIM = 128
KV_HEADS = 4
GQA_GROUP = 4
ATT_HEADS = KV_HEADS * GQA_GROUP
IDX_HEADS = 16
IDX_DIM = 64
DSA_TOPK_MAX = 256
QUERY_BLOCK = 128
SSD_HEAD_DIM = 64
SSD_GROUPS = 8
SSD_HEADS_PER_GROUP = 4
SSD_HEADS = SSD_GROUPS * SSD_HEADS_PER_GROUP
SSD_STATE = 128
SSD_CONV = 4
SSD_CHUNK = 128
N_EXPERT_GROUPS = 8
EXPERTS_PER_GROUP = 8
N_EXPERTS = 64
LN_EPS = 1e-5
RMS_EPS = 1e-5

LANES = 128
SUBLANES = 8
VMEM_LIMIT = 56 * 1024 * 1024

KEY_CHUNK = 512
MOE_ROWS = 1024
MOE_SUB = 256
MOE_F_CHUNK = 256
MOE_N_CHUNK = 256
MOE_K_SPLIT = 4
MOE_KD_SPLIT = 2
DMA_UNROLL = 8
NEG_BIG = -1e30
INT_MIN = -2 ** 31
NEG_INF_KEY = -2139095041


def _params(sem):
    return pltpu.CompilerParams(dimension_semantics=sem, vmem_limit_bytes=VMEM_LIMIT)


def _mm_cast_kernel(a_ref, b_ref, o_ref, abf_ref):
    @pl.when(pl.program_id(1) == 0)
    def _():
        abf_ref[...] = a_ref[...].astype(BF16)

    o_ref[...] = jnp.dot(abf_ref[...], b_ref[...], preferred_element_type=F32).astype(o_ref.dtype)


def _mm_kernel(a_ref, b_ref, o_ref):
    o_ref[...] = jnp.dot(a_ref[...], b_ref[...], preferred_element_type=F32).astype(o_ref.dtype)


def _matmul(a, b, out_dtype, tm, tn):
    m, k = a.shape
    n = b.shape[1]
    tm = min(tm, m)
    assert m % tm == 0 and n % tn == 0
    cast = a.dtype != BF16
    out_specs = pl.BlockSpec((tm, tn), lambda i, j: (i, j))
    out_shape = jax.ShapeDtypeStruct((m, n), out_dtype)
    if cast:
        out_specs = [out_specs, pl.BlockSpec((tm, k), lambda i, j: (i, 0))]
        out_shape = [out_shape, jax.ShapeDtypeStruct((m, k), BF16)]
    return pl.pallas_call(
        _mm_cast_kernel if cast else _mm_kernel,
        grid=(m // tm, n // tn),
        in_specs=[pl.BlockSpec((tm, k), lambda i, j: (i, 0)),
                  pl.BlockSpec((k, tn), lambda i, j: (0, j))],
        out_specs=out_specs,
        out_shape=out_shape,
        compiler_params=_params(("parallel", "arbitrary")),
        name="matmul_cast" if cast else "matmul",
    )(a, b)


def _attn_kernel(q_ref, k_ref, v_ref, qi_ref, kw_ref, g_ref, b_ref, o_ref,
                 kln_ref, key_ref, bias_ref, s_ref, mrun_ref, lrun_ref, acc_ref, *, top_k):
    i = pl.program_id(1)
    tq = QUERY_BLOCK
    ck = KEY_CHUNK

    @pl.when(i == 0)
    def _():
        kx = kw_ref[:, 0:IDX_DIM]
        mu = jnp.mean(kx, axis=-1, keepdims=True)
        var = jnp.mean(jnp.square(kx - mu), axis=-1, keepdims=True)
        y = (kx - mu) * lax.rsqrt(var + LN_EPS)
        kln_ref[...] = (y * g_ref[...] + b_ref[...]).astype(BF16)

    q_start = i * tq
    n_chunks = (q_start + tq + ck - 1) // ck
    q_pos = q_start + lax.broadcasted_iota(I32, (tq, 1), 0)
    w = kw_ref[pl.ds(pl.multiple_of(q_start, tq), tq), IDX_DIM:IDX_DIM + IDX_HEADS]
    w = w * (IDX_HEADS ** -0.5 * IDX_DIM ** -0.5)
    qi = jnp.concatenate([qi_ref[:, h * IDX_DIM:(h + 1) * IDX_DIM] for h in range(IDX_HEADS)], axis=0).astype(BF16)

    def chunk_off(c):
        return pl.multiple_of(c * ck, ck)

    def key_pos(c):
        return c * ck + lax.broadcasted_iota(I32, (1, ck), 1)

    def score_chunk(c, carry):
        off = chunk_off(c)
        kc = kln_ref[pl.ds(off, ck), :]
        d = lax.dot_general(qi, kc, (((1,), (1,)), ((), ())), preferred_element_type=F32)
        acc = jnp.zeros((tq, ck), F32)
        for h in range(IDX_HEADS):
            acc = acc + jnp.maximum(d[h * tq:(h + 1) * tq, :], 0.0) * w[:, h:h + 1]
        acc = jnp.where(key_pos(c) <= q_pos, acc, -jnp.inf)
        bits = pltpu.bitcast(acc, I32)
        key_ref[:, pl.ds(off, ck)] = bits ^ ((bits >> 31) & 0x7FFFFFFF)
        return carry

    lax.fori_loop(0, n_chunks, score_chunk, 0)

    def bit_body(b, carry):
        cand, cnt_cand = carry
        trial = cand | lax.shift_left(jnp.int32(1), jnp.int32(31) - jnp.asarray(b, I32))
        trial_b = jnp.broadcast_to(trial ^ INT_MIN, (tq, LANES))

        def cnt_chunk(c, cnt):
            kc = key_ref[:, pl.ds(chunk_off(c), ck)]
            for s in range(ck // LANES):
                cnt = cnt + jnp.where(kc[:, s * LANES:(s + 1) * LANES] >= trial_b, 1.0, 0.0)
            return cnt

        cnt = lax.fori_loop(0, n_chunks, cnt_chunk, jnp.zeros((tq, LANES), F32))
        total = jnp.sum(cnt, axis=1, keepdims=True)
        ok = total >= float(top_k)
        return jnp.where(ok, trial, cand), jnp.where(ok, total, cnt_cand)

    n_keys = (jnp.zeros((tq, 1), I32) + n_chunks * ck).astype(F32)
    cand, cnt_ge = lax.fori_loop(0, 32, bit_body, (jnp.zeros((tq, 1), I32), n_keys))
    thr = cand ^ INT_MIN

    def bias_chunk(c, carry):
        off = chunk_off(c)
        sel = (key_ref[:, pl.ds(off, ck)] >= thr) & (key_pos(c) <= q_pos)
        bias_ref[:, pl.ds(off, ck)] = jnp.where(sel, 0.0, NEG_BIG)
        return carry

    lax.fori_loop(0, n_chunks, bias_chunk, 0)

    tie = (cnt_ge > float(top_k)) & (thr > NEG_INF_KEY)

    @pl.when(jnp.max(jnp.where(tie, 1.0, 0.0)) > 0.5)
    def _():
        tri = (lax.broadcasted_iota(I32, (ck, ck), 0) <= lax.broadcasted_iota(I32, (ck, ck), 1)).astype(BF16)

        def gt_chunk(c, cnt):
            kc = key_ref[:, pl.ds(chunk_off(c), ck)]
            return cnt + jnp.sum(jnp.where(kc > thr, 1.0, 0.0), axis=1, keepdims=True)

        need = float(top_k) - lax.fori_loop(0, n_chunks, gt_chunk, jnp.zeros((tq, 1), F32))

        def tie_chunk(c, seen):
            off = chunk_off(c)
            kc = key_ref[:, pl.ds(off, ck)]
            eq = jnp.where(kc == thr, 1.0, 0.0)
            rank = seen + jnp.dot(eq.astype(BF16), tri, preferred_element_type=F32)
            keep = (kc > thr) | ((kc == thr) & ((rank <= need) | jnp.logical_not(tie)))
            sel = keep & (key_pos(c) <= q_pos)
            bias_ref[:, pl.ds(off, ck)] = jnp.where(sel, 0.0, NEG_BIG)
            return seen + jnp.sum(eq, axis=1, keepdims=True)

        lax.fori_loop(0, n_chunks, tie_chunk, jnp.zeros((tq, 1), F32))

    scale = HEAD_DIM ** -0.5
    gq = GQA_GROUP
    for g in range(KV_HEADS):
        qg = jnp.concatenate([q_ref[:, (g * gq + j) * HEAD_DIM:(g * gq + j + 1) * HEAD_DIM] for j in range(gq)],
                             axis=0)

        mrun_ref[...] = jnp.full(mrun_ref.shape, NEG_BIG, F32)

        def logit_chunk(c, carry, g=g, qg=qg):
            off = chunk_off(c)
            kc = k_ref[pl.ds(off, ck), g * HEAD_DIM:(g + 1) * HEAD_DIM]
            s = lax.dot_general(qg, kc, (((1,), (1,)), ((), ())), preferred_element_type=F32)
            bias = bias_ref[:, pl.ds(off, ck)]
            s = s * scale + jnp.concatenate([bias] * gq, axis=0)
            s_ref[:, pl.ds(off, ck)] = s
            m = mrun_ref[...]
            for t in range(ck // LANES):
                m = jnp.maximum(m, s[:, t * LANES:(t + 1) * LANES])
            mrun_ref[...] = m
            return carry

        lax.fori_loop(0, n_chunks, logit_chunk, 0)
        m = jnp.max(mrun_ref[...], axis=1, keepdims=True)

        lrun_ref[...] = jnp.zeros(lrun_ref.shape, F32)
        acc_ref[...] = jnp.zeros(acc_ref.shape, F32)

        def prob_chunk(c, carry, g=g, m=m):
            off = chunk_off(c)
            vc = v_ref[pl.ds(off, ck), g * HEAD_DIM:(g + 1) * HEAD_DIM]
            p = jnp.exp(s_ref[:, pl.ds(off, ck)] - m)
            l = lrun_ref[...]
            for t in range(ck // LANES):
                l = l + p[:, t * LANES:(t + 1) * LANES]
            lrun_ref[...] = l
            acc_ref[...] += jnp.dot(p.astype(BF16), vc, preferred_element_type=F32)
            return carry

        lax.fori_loop(0, n_chunks, prob_chunk, 0)
        out = acc_ref[...] / jnp.sum(lrun_ref[...], axis=1, keepdims=True)
        for j in range(gq):
            h = g * gq + j
            o_ref[:, h * HEAD_DIM:(h + 1) * HEAD_DIM] = out[j * tq:(j + 1) * tq, :].astype(o_ref.dtype)


def _dsa_attention(qkv, idx, kn_g, kn_b, bsz, seq):
    top_k = min(DSA_TOPK_MAX, seq // 4)
    nq = seq // QUERY_BLOCK
    att_w = ATT_HEADS * HEAD_DIM
    kv_w = KV_HEADS * HEAD_DIM
    qi_w = IDX_HEADS * IDX_DIM
    assert seq % KEY_CHUNK == 0 and att_w % kv_w == 0 and qi_w % LANES == 0
    return pl.pallas_call(
        functools.partial(_attn_kernel, top_k=top_k),
        grid=(bsz, nq),
        in_specs=[pl.BlockSpec((QUERY_BLOCK, att_w), lambda b, i: (b * nq + i, 0)),
                  pl.BlockSpec((seq, kv_w), lambda b, i: (b, att_w // kv_w)),
                  pl.BlockSpec((seq, kv_w), lambda b, i: (b, att_w // kv_w + 1)),
                  pl.BlockSpec((QUERY_BLOCK, qi_w), lambda b, i: (b * nq + i, 0)),
                  pl.BlockSpec((seq, LANES), lambda b, i: (b, qi_w // LANES)),
                  pl.BlockSpec((1, IDX_DIM), lambda b, i: (0, 0)),
                  pl.BlockSpec((1, IDX_DIM), lambda b, i: (0, 0))],
        out_specs=pl.BlockSpec((QUERY_BLOCK, att_w), lambda b, i: (b * nq + i, 0)),
        out_shape=jax.ShapeDtypeStruct((bsz * seq, att_w), BF16),
        scratch_shapes=[pltpu.VMEM((seq, IDX_DIM), BF16),
                        pltpu.VMEM((QUERY_BLOCK, seq), I32),
                        pltpu.VMEM((QUERY_BLOCK, seq), F32),
                        pltpu.VMEM((GQA_GROUP * QUERY_BLOCK, seq), F32),
                        pltpu.VMEM((GQA_GROUP * QUERY_BLOCK, LANES), F32),
                        pltpu.VMEM((GQA_GROUP * QUERY_BLOCK, LANES), F32),
                        pltpu.VMEM((GQA_GROUP * QUERY_BLOCK, HEAD_DIM), F32)],
        compiler_params=_params(("parallel", "arbitrary")),
        name="dsa_attention",
    )(qkv, qkv, qkv, idx, idx, kn_g.reshape(1, IDX_DIM), kn_b.reshape(1, IDX_DIM))


def _silu(x):
    return x / (1.0 + jnp.exp(-x))


def _ssd_kernel(xbc_ref, z_ref, dt_ref, cw_ref, cb_ref, dtb_ref, alog_ref, dsk_ref, ng_ref, o_ref,
                xpad_ref, act_ref, y_ref, h_ref):
    c = pl.program_id(1)
    cq = SSD_CHUNK
    width = SSD_HEADS * SSD_HEAD_DIM
    b_off = width
    c_off = width + SSD_GROUPS * SSD_STATE

    @pl.when(c == 0)
    def _():
        xpad_ref[0:SUBLANES, :] = jnp.zeros((SUBLANES, xpad_ref.shape[1]), F32)
        h_ref[...] = jnp.zeros(h_ref.shape, F32)

    xpad_ref[SUBLANES:SUBLANES + cq, :] = xbc_ref[...]
    col = 512
    for j in range(xpad_ref.shape[1] // col):
        cs = slice(j * col, (j + 1) * col)
        acc = cb_ref[:, cs] + jnp.zeros((cq, col), F32)
        for t in range(SSD_CONV):
            r0 = SUBLANES - (SSD_CONV - 1) + t
            acc = acc + xpad_ref[r0:r0 + cq, cs] * cw_ref[t:t + 1, cs]
        act_ref[:, cs] = _silu(acc)
    xpad_ref[0:SUBLANES, :] = xpad_ref[cq:cq + SUBLANES, :]

    xdt_in = dt_ref[:, 0:LANES] + dtb_ref[...]
    dt = jnp.maximum(xdt_in, 0.0) + jnp.log1p(jnp.exp(-jnp.abs(xdt_in)))
    da = dt * (-jnp.exp(alog_ref[...]))
    row = lax.broadcasted_iota(I32, (cq, cq), 0)
    coli = lax.broadcasted_iota(I32, (cq, cq), 1)
    causal = row >= coli
    tril = causal.astype(F32)
    acs = jnp.dot(tril, da, preferred_element_type=F32, precision=lax.Precision.HIGHEST)
    acs_t = acs.T
    dec_in = jnp.exp(acs)
    a_last = acs[cq - 1:cq, :]
    dec_out = jnp.exp(a_last - acs)
    dec_chunk = jnp.exp(a_last)

    for g in range(SSD_GROUPS):
        bg = act_ref[:, b_off + g * SSD_STATE:b_off + (g + 1) * SSD_STATE].astype(BF16)
        cg = act_ref[:, c_off + g * SSD_STATE:c_off + (g + 1) * SSD_STATE].astype(BF16)
        cbm = lax.dot_general(cg, bg, (((1,), (1,)), ((), ())), preferred_element_type=F32)
        for j in range(SSD_HEADS_PER_GROUP):
            hd = g * SSD_HEADS_PER_GROUP + j
            xs = slice(hd * SSD_HEAD_DIM, (hd + 1) * SSD_HEAD_DIM)
            seg = acs[:, hd:hd + 1] - acs_t[hd:hd + 1, :]
            lmat = jnp.exp(jnp.where(causal, seg, -jnp.inf))
            xh = act_ref[:, xs]
            xdt = xh * dt[:, hd:hd + 1]
            y = jnp.dot((cbm * lmat).astype(BF16), xdt.astype(BF16), preferred_element_type=F32)
            hprev = h_ref[hd]
            yoff = lax.dot_general(cg, hprev.astype(BF16), (((1,), (1,)), ((), ())),
                                   preferred_element_type=F32)
            y = y + yoff * dec_in[:, hd:hd + 1] + xh * dsk_ref[:, xs]
            y_ref[:, xs] = y
            st = lax.dot_general((xdt * dec_out[:, hd:hd + 1]).astype(BF16), bg,
                                 (((0,), (0,)), ((), ())), preferred_element_type=F32)
            h_ref[hd] = hprev * dec_chunk[:, hd:hd + 1] + st

    gw = width // SSD_GROUPS
    for g in range(SSD_GROUPS):
        gs = slice(g * gw, (g + 1) * gw)
        gated = y_ref[:, gs] * _silu(z_ref[:, gs])
        ms = jnp.mean(jnp.square(gated), axis=-1, keepdims=True)
        o_ref[:, gs] = (gated * lax.rsqrt(ms + RMS_EPS) * ng_ref[:, gs]).astype(o_ref.dtype)


def _ssd_mixer(ssd_in, conv_w, conv_b, dt_bias, a_log, d_skip, norm_g, bsz, seq):
    nc = seq // SSD_CHUNK
    width = SSD_HEADS * SSD_HEAD_DIM
    xbc_w = width + 2 * SSD_GROUPS * SSD_STATE
    dt_w = ssd_in.shape[1] - xbc_w - width
    assert xbc_w % width == 0 and (xbc_w + width) % dt_w == 0 and dt_w >= LANES
    pad = LANES - SSD_HEADS
    dtb = jnp.pad(dt_bias.astype(F32), (0, pad)).reshape(1, LANES)
    alog = jnp.pad(a_log.astype(F32), (0, pad)).reshape(1, LANES)
    dsk = jnp.repeat(d_skip.astype(F32), SSD_HEAD_DIM).reshape(1, width)
    const = lambda b, c: (0, 0)
    return pl.pallas_call(
        _ssd_kernel,
        grid=(bsz, nc),
        in_specs=[pl.BlockSpec((SSD_CHUNK, xbc_w), lambda b, c: (b * nc + c, 0)),
                  pl.BlockSpec((SSD_CHUNK, width), lambda b, c: (b * nc + c, xbc_w // width)),
                  pl.BlockSpec((SSD_CHUNK, dt_w), lambda b, c: (b * nc + c, (xbc_w + width) // dt_w)),
                  pl.BlockSpec((SSD_CONV, xbc_w), const),
                  pl.BlockSpec((1, xbc_w), const),
                  pl.BlockSpec((1, LANES), const),
                  pl.BlockSpec((1, LANES), const),
                  pl.BlockSpec((1, width), const),
                  pl.BlockSpec((1, width), const)],
        out_specs=pl.BlockSpec((SSD_CHUNK, width), lambda b, c: (b * nc + c, 0)),
        out_shape=jax.ShapeDtypeStruct((bsz * seq, width), BF16),
        scratch_shapes=[pltpu.VMEM((SSD_CHUNK + SUBLANES, xbc_w), F32),
                        pltpu.VMEM((SSD_CHUNK, xbc_w), F32),
                        pltpu.VMEM((SSD_CHUNK, width), F32),
                        pltpu.VMEM((SSD_HEADS, SSD_HEAD_DIM, SSD_STATE), F32)],
        compiler_params=_params(("parallel", "arbitrary")),
        name="ssd_mixer",
    )(ssd_in, ssd_in, ssd_in, conv_w.astype(F32), conv_b.astype(F32).reshape(1, xbc_w), dtb, alog, dsk,
      norm_g.astype(F32).reshape(1, width))


def _layer_norm_rows(x, g, b):
    mu = jnp.mean(x, axis=-1, keepdims=True)
    var = jnp.mean(jnp.square(x - mu), axis=-1, keepdims=True)
    return (x - mu) * lax.rsqrt(var + LN_EPS) * g + b


def _ln_router_kernel(x_ref, mix_ref, g_ref, b_ref, wr_ref, br_ref, o_ref, xp_ref, r_ref, *, alpha):
    x1 = _layer_norm_rows(alpha * x_ref[...] + mix_ref[...], g_ref[...], b_ref[...])
    o_ref[...] = x1
    half = xp_ref.shape[1]
    bits = pltpu.bitcast(x1.astype(BF16).astype(F32), jnp.uint32)
    xp_ref[...] = (bits[:, :half] >> 16) | bits[:, half:]
    logits = jnp.dot(x1, wr_ref[...], preferred_element_type=F32, precision=lax.Precision.HIGHEST)
    logits = logits + br_ref[...]
    rows = logits.shape[0]
    lane = lax.broadcasted_iota(I32, (rows, LANES), 1).astype(F32)
    ng, epg = float(N_EXPERT_GROUPS), float(EXPERTS_PER_GROUP)
    far = float(LANES)

    gmask = lane < ng
    gl = jnp.where(gmask, logits, -jnp.inf)
    ge = jnp.exp(gl - jnp.max(gl, axis=1, keepdims=True))
    gprob = ge / jnp.sum(ge, axis=1, keepdims=True)
    gprob = jnp.where(gmask, gprob, -1.0)
    gw = jnp.max(gprob, axis=1, keepdims=True)
    gsel = jnp.min(jnp.where(gprob == gw, lane, far), axis=1, keepdims=True)

    e_lo = ng + gsel * epg
    emask = (lane >= e_lo) & (lane < e_lo + epg)
    el = jnp.where(emask, logits, -jnp.inf)
    v0 = jnp.max(el, axis=1, keepdims=True)
    i0 = jnp.min(jnp.where(emask & (el == v0), lane, far), axis=1, keepdims=True)
    emask1 = emask & (lane != i0)
    el1 = jnp.where(emask1, logits, -jnp.inf)
    v1 = jnp.max(el1, axis=1, keepdims=True)
    i1 = jnp.min(jnp.where(emask1 & (el1 == v1), lane, far), axis=1, keepdims=True)
    e = jnp.exp(v1 - v0)
    p0 = 1.0 / (1.0 + e)
    p1 = e / (1.0 + e)
    out = jnp.where(lane == 0.0, i0 - ng,
                    jnp.where(lane == 1.0, i1 - ng,
                              jnp.where(lane == 2.0, gw * p0, jnp.where(lane == 3.0, gw * p1, 0.0))))
    r_ref[...] = out


def _ln_router(x, mixed, g, b, w_rg, b_rg, w_re, b_re, alpha, tr=256):
    t, d = x.shape
    ncol = N_EXPERT_GROUPS + N_EXPERTS
    wr = jnp.pad(jnp.concatenate([w_rg, w_re], axis=1).astype(F32), ((0, 0), (0, LANES - ncol)))
    br = jnp.pad(jnp.concatenate([b_rg, b_re]).astype(F32), (0, LANES - ncol)).reshape(1, LANES)
    const = lambda i: (0, 0)
    return pl.pallas_call(
        functools.partial(_ln_router_kernel, alpha=alpha),
        grid=(t // tr,),
        in_specs=[pl.BlockSpec((tr, d), lambda i: (i, 0)),
                  pl.BlockSpec((tr, d), lambda i: (i, 0)),
                  pl.BlockSpec((1, d), const), pl.BlockSpec((1, d), const),
                  pl.BlockSpec((d, LANES), const), pl.BlockSpec((1, LANES), const)],
        out_specs=[pl.BlockSpec((tr, d), lambda i: (i, 0)),
                   pl.BlockSpec((tr, d // 2), lambda i: (i, 0)),
                   pl.BlockSpec((tr, LANES), lambda i: (i, 0))],
        out_shape=[jax.ShapeDtypeStruct((t, d), F32), jax.ShapeDtypeStruct((t, d // 2), jnp.uint32),
                   jax.ShapeDtypeStruct((t, LANES), F32)],
        compiler_params=_params(("parallel",)),
        name="ln1_router",
    )(x, mixed, g.astype(F32).reshape(1, d), b.astype(F32).reshape(1, d), wr, br)


def _row_copy(src_ref, dst_ref, sem, tok, row):
    return pltpu.make_async_copy(src_ref.at[pl.ds(tok, 1)], dst_ref.at[pl.ds(row, 1)], sem)


def _moe_kernel(tok_ref, be_ref, bv_ref, nu_ref, x_ref, *refs, nf, ks, kd):
    wg, wu = refs[0:ks], refs[ks:2 * ks]
    wd_lo, wd_hi = refs[2 * ks:2 * ks + kd], refs[2 * ks + kd:2 * ks + 2 * kd]
    o_ref, xq_ref, xbf_ref, act_ref, sem = refs[2 * ks + 2 * kd:]
    s = pl.program_id(0)
    p = pl.program_id(1)
    n_blk = pl.num_programs(0)
    sub = MOE_SUB
    n_sub = MOE_ROWS // sub
    half = xq_ref.shape[1]

    def nsub_of(blk):
        return (bv_ref[blk] + (sub - 1)) // sub

    def issue(blk, r):
        base = blk * MOE_ROWS + r * sub

        def body(i, carry):
            for u in range(DMA_UNROLL):
                j = i * DMA_UNROLL + u
                _row_copy(x_ref, xq_ref, sem, tok_ref[base + j], r * sub + j).start()
            return carry
        lax.fori_loop(0, sub // DMA_UNROLL, body, 0)

    nsub = nsub_of(s)

    @pl.when((s == 0) & (p == 0))
    def _():
        for r in range(n_sub):
            @pl.when(r < nsub)
            def _(r=r):
                issue(s, r)

    @pl.when(p == 0)
    def _():
        for r in range(n_sub):
            @pl.when(r < nsub)
            def _(r=r):
                pltpu.make_async_copy(x_ref.at[pl.ds(0, sub)], xq_ref.at[pl.ds(r * sub, sub)], sem).wait()
        for r in range(n_sub):
            @pl.when(r < nsub)
            def _(r=r):
                rs = slice(r * sub, (r + 1) * sub)
                word = xq_ref[rs, :]
                xbf_ref[rs, 0:half] = pltpu.bitcast(word << 16, F32).astype(BF16)
                xbf_ref[rs, half:2 * half] = pltpu.bitcast(word & jnp.uint32(0xFFFF0000), F32).astype(BF16)

    nxt = jnp.minimum(s + 1, n_blk - 1)
    for r in range(n_sub):
        @pl.when((p == r + 1) & (s + 1 < n_blk) & (r < nsub_of(nxt)))
        def _(r=r):
            issue(nxt, r)

    for k in range(1, n_sub + 1):
        m = k * sub

        @pl.when((p < nf) & (nsub == k))
        def _(m=m):
            kq = xbf_ref.shape[1] // ks
            gate = up = None
            for q in range(ks):
                xr = xbf_ref[0:m, q * kq:(q + 1) * kq]
                gq = jnp.dot(xr, wg[q][...].astype(BF16), preferred_element_type=F32)
                uq = jnp.dot(xr, wu[q][...].astype(BF16), preferred_element_type=F32)
                gate = gq if gate is None else gate + gq
                up = uq if up is None else up + uq
            act_ref[p, 0:m, :] = (_silu(gate) * up).astype(BF16)

    per = nf // kd

    def down(parts, m):
        acc = None
        for q in range(nf):
            r0 = (q % per) * MOE_F_CHUNK
            wq = parts[q // per][r0:r0 + MOE_F_CHUNK, :].astype(BF16)
            t = jnp.dot(act_ref[q, 0:m, :], wq, preferred_element_type=F32)
            acc = t if acc is None else acc + t
        return pltpu.bitcast(acc.astype(BF16).astype(F32), jnp.uint32)

    for k in range(0, n_sub + 1):
        m = k * sub

        @pl.when((p >= nf) & (nsub == k))
        def _(m=m):
            if m > 0:
                o_ref[0:m, :] = (down(wd_lo, m) >> 16) | down(wd_hi, m)
            if m < MOE_ROWS:
                o_ref[m:MOE_ROWS, :] = jnp.zeros((MOE_ROWS - m, o_ref.shape[1]), jnp.uint32)


def _moe_ffn(xp, row_tok, blk_e, blk_valid, n_used, w_gate, w_up, w_down):
    n_rows = row_tok.shape[0]
    n_e, d, f = w_gate.shape
    ks, kd = MOE_K_SPLIT, MOE_KD_SPLIT
    nf = f // MOE_F_CHUNK
    nn = (d // 2) // MOE_N_CHUNK
    n_blk = n_rows // MOE_ROWS
    assert nf + nn > MOE_ROWS // MOE_SUB and nf % kd == 0 and d % ks == 0

    def gu_spec(q):
        return pl.BlockSpec((None, d // ks, MOE_F_CHUNK), lambda s, p, tok, be, bv, nu: (
            be[s], q, jnp.where(s < nu[0], jnp.minimum(p, nf - 1), nf - 1)))

    def dn_col(s, p, nu):
        return jnp.where(s < nu[0], jnp.maximum(p - nf, 0), nn - 1)

    def out_map(s, p, tok, be, bv, nu):
        return (s, jnp.maximum(p - nf, 0))

    def dn_spec(q, hi):
        return pl.BlockSpec((None, f // kd, MOE_N_CHUNK), lambda s, p, tok, be, bv, nu: (
            be[s], q, dn_col(s, p, nu) + hi * nn))

    in_specs = ([pl.BlockSpec(memory_space=pl.ANY)] + [gu_spec(q) for q in range(ks)] + [gu_spec(q) for q in range(ks)]
                + [dn_spec(q, 0) for q in range(kd)] + [dn_spec(q, 1) for q in range(kd)])
    return pl.pallas_call(
        functools.partial(_moe_kernel, nf=nf, ks=ks, kd=kd),
        grid_spec=pltpu.PrefetchScalarGridSpec(
            num_scalar_prefetch=4,
            grid=(n_blk, nf + nn),
            in_specs=in_specs,
            out_specs=pl.BlockSpec((MOE_ROWS, MOE_N_CHUNK), out_map),
            scratch_shapes=[pltpu.VMEM((MOE_ROWS, d // 2), jnp.uint32),
                            pltpu.VMEM((MOE_ROWS, d), BF16),
                            pltpu.VMEM((nf, MOE_ROWS, MOE_F_CHUNK), BF16),
                            pltpu.SemaphoreType.DMA(())]),
        out_shape=jax.ShapeDtypeStruct((n_rows, d // 2), jnp.uint32),
        compiler_params=_params(("arbitrary", "arbitrary")),
        name="moe_ffn",
    )(row_tok, blk_e, blk_valid, n_used, xp, *([w_gate] * ks), *([w_up] * ks), *([w_down] * (2 * kd)))


def _combine_kernel(dest_ref, y_ref, x_ref, r_ref, g_ref, b_ref, o_ref, buf_ref, sem, *, alpha, tt):
    s = pl.program_id(0)
    n = pl.num_programs(0)

    rows = 2 * tt

    def start(blk, slot):
        base = blk * rows

        def body(i, carry):
            for u in range(DMA_UNROLL):
                r = i * DMA_UNROLL + u
                pltpu.make_async_copy(y_ref.at[pl.ds(dest_ref[base + r], 1)],
                                      buf_ref.at[slot, pl.ds(r, 1)], sem.at[slot]).start()
            return carry
        lax.fori_loop(0, rows // DMA_UNROLL, body, 0)

    @pl.when(s == 0)
    def _():
        start(s, 0)

    for par in range(2):
        @pl.when((s % 2 == par) & (s + 1 < n))
        def _(par=par):
            start(s + 1, 1 - par)

    slot = s % 2
    pltpu.make_async_copy(y_ref.at[pl.ds(0, rows)], buf_ref.at[slot], sem.at[slot]).wait()
    g0 = r_ref[:, 2:3]
    g1 = r_ref[:, 3:4]
    w0 = buf_ref[slot, 0:tt, :]
    w1 = buf_ref[slot, tt:2 * tt, :]
    half = w0.shape[1]
    himask = jnp.uint32(0xFFFF0000)
    y_lo = g0 * pltpu.bitcast(w0 << 16, F32) + g1 * pltpu.bitcast(w1 << 16, F32)
    y_hi = g0 * pltpu.bitcast(w0 & himask, F32) + g1 * pltpu.bitcast(w1 & himask, F32)
    r_lo = alpha * x_ref[:, 0:half] + y_lo
    r_hi = alpha * x_ref[:, half:2 * half] + y_hi
    inv_d = 1.0 / (2 * half)
    mu = (jnp.sum(r_lo, axis=-1, keepdims=True) + jnp.sum(r_hi, axis=-1, keepdims=True)) * inv_d
    var = (jnp.sum(jnp.square(r_lo - mu), axis=-1, keepdims=True)
           + jnp.sum(jnp.square(r_hi - mu), axis=-1, keepdims=True)) * inv_d
    rstd = lax.rsqrt(var + LN_EPS)
    o_ref[:, 0:half] = (r_lo - mu) * rstd * g_ref[:, 0:half] + b_ref[:, 0:half]
    o_ref[:, half:2 * half] = (r_hi - mu) * rstd * g_ref[:, half:2 * half] + b_ref[:, half:2 * half]


def _combine_ln(y_rows, dest, x1, route, g, b, alpha, tt=128):
    t, d = x1.shape
    const = lambda i, dr: (0, 0)
    return pl.pallas_call(
        functools.partial(_combine_kernel, alpha=alpha, tt=tt),
        grid_spec=pltpu.PrefetchScalarGridSpec(
            num_scalar_prefetch=1,
            grid=(t // tt,),
            in_specs=[pl.BlockSpec(memory_space=pl.ANY),
                      pl.BlockSpec((tt, d), lambda i, dr: (i, 0)),
                      pl.BlockSpec((tt, LANES), lambda i, dr: (i, 0)),
                      pl.BlockSpec((1, d), const), pl.BlockSpec((1, d), const)],
            out_specs=pl.BlockSpec((tt, d), lambda i, dr: (i, 0)),
            scratch_shapes=[pltpu.VMEM((2, 2 * tt, d // 2), jnp.uint32),
                            pltpu.SemaphoreType.DMA((2,))]),
        out_shape=jax.ShapeDtypeStruct((t, d), F32),
        compiler_params=_params(("arbitrary",)),
        name="moe_combine_ln2",
    )(dest, y_rows, x1, route, g.astype(F32).reshape(1, d), b.astype(F32).reshape(1, d))


def _dispatch_plan(expert_id, tt):
    t = expert_id.shape[0]
    m = 2 * t
    e_flat = expert_id.reshape(m)
    onehot = (e_flat[:, None] == jnp.arange(N_EXPERTS, dtype=I32)[None, :]).astype(I32)
    counts = jnp.sum(onehot, axis=0)
    rank = jnp.sum((jnp.cumsum(onehot, axis=0) - onehot) * onehot, axis=1)
    pcounts = (counts + MOE_ROWS - 1) // MOE_ROWS * MOE_ROWS
    pends = jnp.cumsum(pcounts)
    pstarts = pends - pcounts
    dest = (pstarts[e_flat] + rank).astype(I32)
    n_blk = -(-m // MOE_ROWS) + N_EXPERTS
    n_rows = n_blk * MOE_ROWS
    tok = jnp.arange(m, dtype=I32) // 2
    row_tok = jnp.zeros((n_rows,), I32).at[dest].set(tok)
    n_used = (pends[-1] // MOE_ROWS).astype(I32)
    blk_idx = jnp.arange(n_blk, dtype=I32)
    blk_start = jnp.minimum(blk_idx, n_used - 1) * MOE_ROWS
    blk_e = jnp.minimum(jnp.searchsorted(pends, blk_start, side='right'), N_EXPERTS - 1).astype(I32)
    blk_valid = jnp.clip(counts[blk_e] - (blk_start - pstarts[blk_e]), 0, MOE_ROWS)
    blk_valid = jnp.where(blk_idx < n_used, blk_valid, 0).astype(I32)
    dest_tiles = dest.reshape(t // tt, tt, 2).transpose(0, 2, 1).reshape(m)
    return row_tok, blk_e, blk_valid, n_used.reshape(1), dest_tiles


def kernel(x, w_in, idx_kn_g, idx_kn_b, conv_w, conv_b, dt_bias, a_log, d_skip, ssd_norm_g, w_out,
           ln1_g, ln1_b, w_rg, b_rg, w_re, b_re, w_gate, w_up, w_down, ln2_g, ln2_b):
    bsz, seq, d = x.shape
    depth = w_in.shape[0]
    alpha = (2 * depth) ** 0.25
    att_w = ATT_HEADS * HEAD_DIM
    kv_w = KV_HEADS * HEAD_DIM
    qi_w = IDX_HEADS * IDX_DIM
    ssd_w = SSD_HEADS * SSD_HEAD_DIM
    xbc_w = ssd_w + 2 * SSD_GROUPS * SSD_STATE
    sizes = (att_w, kv_w, kv_w, qi_w, IDX_DIM, IDX_HEADS, ssd_w, xbc_w, SSD_HEADS)
    offs = [0]
    for sz in sizes:
        offs.append(offs[-1] + sz)
    tt = 128
    xf = x.reshape(bsz * seq, d)
    for l in range(depth):
        w = w_in[l].astype(BF16)
        zpad = lambda n: jnp.zeros((d, n), BF16)
        w_qkv = w[:, offs[0]:offs[3]]
        w_idx = jnp.concatenate([w[:, offs[3]:offs[6]], zpad(LANES - IDX_DIM - IDX_HEADS)], axis=1)
        w_ssd = jnp.concatenate([w[:, offs[7]:offs[8]], w[:, offs[6]:offs[7]], w[:, offs[8]:offs[9]],
                                 zpad(2 * LANES - SSD_HEADS)], axis=1)
        qkv, xbf = _matmul(xf, w_qkv, BF16, 512, 512)
        idx = _matmul(xbf, w_idx, F32, 1024, 384)
        ssd_in = _matmul(xbf, w_ssd, F32, 1024, 256)
        att = _dsa_attention(qkv, idx, idx_kn_g[l], idx_kn_b[l], bsz, seq)
        ssd = _ssd_mixer(ssd_in, conv_w[l], conv_b[l], dt_bias[l], a_log[l], d_skip[l], ssd_norm_g[l], bsz, seq)
        mixed = _matmul(jnp.concatenate([att, ssd], axis=1), w_out[l].astype(BF16), F32, 1024, 512)
        x1, x1p, route = _ln_router(xf, mixed, ln1_g[l], ln1_b[l], w_rg[l], b_rg[l], w_re[l], b_re[l], alpha)
        expert_id = route[:, 0:2].astype(I32)
        row_tok, blk_e, blk_valid, n_used, dest_tiles = _dispatch_plan(expert_id, tt)
        y_rows = _moe_ffn(x1p, row_tok, blk_e, blk_valid, n_used, w_gate[l], w_up[l], w_down[l])
        xf = _combine_ln(y_rows, dest_tiles, x1, route, ln2_g[l], ln2_b[l], alpha, tt)
    return xf.reshape(bsz, seq, d)
```

```python
import functools

import jax
import jax.numpy as jnp
from jax import lax
from jax.experimental import pallas as pl
from jax.experimental.pallas import tpu as pltpu

F32 = jnp.float32
BF16 = jnp.bfloat16
I32 = jnp.int32

HEAD_DIM = 128
KV_HEADS = 4
GQA_GROUP = 4
ATT_HEADS = KV_HEADS * GQA_GROUP
IDX_HEADS = 16
IDX_DIM = 64
DSA_TOPK_MAX = 256
QUERY_BLOCK = 128
SSD_HEAD_DIM = 64
SSD_GROUPS = 8
SSD_HEADS_PER_GROUP = 4
SSD_HEADS = SSD_GROUPS * SSD_HEADS_PER_GROUP
SSD_STATE = 128
SSD_CONV = 4
SSD_CHUNK = 128
N_EXPERT_GROUPS = 8
EXPERTS_PER_GROUP = 8
N_EXPERTS = 64
LN_EPS = 1e-5
RMS_EPS = 1e-5

LANES = 128
SUBLANES = 8
VMEM_LIMIT = 56 * 1024 * 1024

KEY_CHUNK = 512
MOE_ROWS = 768
MOE_SUB = 256
MOE_F_CHUNK = 256
MOE_N_CHUNK = 256
MOE_K_SPLIT = 4
MOE_RING = 3
MOE_LOOKAHEAD = 2
DMA_UNROLL = 8
NEG_BIG = -1e30
INT_MIN = -2 ** 31
NEG_INF_KEY = -2139095041


def _params(sem):
    return pltpu.CompilerParams(dimension_semantics=sem, vmem_limit_bytes=VMEM_LIMIT)


def _mm_cast_kernel(a_ref, b_ref, o_ref, abf_ref):
    @pl.when(pl.program_id(1) == 0)
    def _():
        abf_ref[...] = a_ref[...].astype(BF16)

    o_ref[...] = jnp.dot(abf_ref[...], b_ref[...], preferred_element_type=F32).astype(o_ref.dtype)


def _mm_kernel(a_ref, b_ref, o_ref):
    o_ref[...] = jnp.dot(a_ref[...], b_ref[...], preferred_element_type=F32).astype(o_ref.dtype)


def _mm_pair_kernel(a1_ref, a2_ref, b_ref, o_ref):
    k1 = a1_ref.shape[1]
    acc = jnp.dot(a1_ref[...], b_ref[0:k1, :], preferred_element_type=F32)
    acc = acc + jnp.dot(a2_ref[...], b_ref[k1:, :], preferred_element_type=F32)
    o_ref[...] = acc.astype(o_ref.dtype)


def _matmul_pair(a1, a2, b, out_dtype, tm, tn):
    m, k1 = a1.shape
    k2 = a2.shape[1]
    n = b.shape[1]
    tm = min(tm, m)
    assert m % tm == 0 and n % tn == 0 and b.shape[0] == k1 + k2
    return pl.pallas_call(
        _mm_pair_kernel,
        grid=(m // tm, n // tn),
        in_specs=[pl.BlockSpec((tm, k1), lambda i, j: (i, 0)),
                  pl.BlockSpec((tm, k2), lambda i, j: (i, 0)),
                  pl.BlockSpec((k1 + k2, tn), lambda i, j: (0, j))],
        out_specs=pl.BlockSpec((tm, tn), lambda i, j: (i, j)),
        out_shape=jax.ShapeDtypeStruct((m, n), out_dtype),
        compiler_params=_params(("parallel", "arbitrary")),
        name="matmul_pair",
    )(a1, a2, b)


def _matmul(a, b, out_dtype, tm, tn):
    m, k = a.shape
    n = b.shape[1]
    tm = min(tm, m)
    assert m % tm == 0 and n % tn == 0
    cast = a.dtype != BF16
    out_specs = pl.BlockSpec((tm, tn), lambda i, j: (i, j))
    out_shape = jax.ShapeDtypeStruct((m, n), out_dtype)
    if cast:
        out_specs = [out_specs, pl.BlockSpec((tm, k), lambda i, j: (i, 0))]
        out_shape = [out_shape, jax.ShapeDtypeStruct((m, k), BF16)]
    return pl.pallas_call(
        _mm_cast_kernel if cast else _mm_kernel,
        grid=(m // tm, n // tn),
        in_specs=[pl.BlockSpec((tm, k), lambda i, j: (i, 0)),
                  pl.BlockSpec((k, tn), lambda i, j: (0, j))],
        out_specs=out_specs,
        out_shape=out_shape,
        compiler_params=_params(("parallel", "arbitrary")),
        name="matmul_cast" if cast else "matmul",
    )(a, b)


def _attn_kernel(q_ref, k_ref, v_ref, qi_ref, kw_ref, g_ref, b_ref, o_ref,
                 kln_ref, key_ref, bias_ref, s_ref, mrun_ref, lrun_ref, acc_ref, *, top_k):
    i = pl.program_id(1)
    tq = QUERY_BLOCK
    ck = KEY_CHUNK

    @pl.when(i == 0)
    def _():
        kx = kw_ref[:, 0:IDX_DIM]
        mu = jnp.mean(kx, axis=-1, keepdims=True)
        var = jnp.mean(jnp.square(kx - mu), axis=-1, keepdims=True)
        y = (kx - mu) * lax.rsqrt(var + LN_EPS)
        kln_ref[...] = (y * g_ref[...] + b_ref[...]).astype(BF16)

    q_start = i * tq
    n_chunks = (q_start + tq + ck - 1) // ck
    q_pos = q_start + lax.broadcasted_iota(I32, (tq, 1), 0)
    w = kw_ref[pl.ds(pl.multiple_of(q_start, tq), tq), IDX_DIM:IDX_DIM + IDX_HEADS]
    w = w * (IDX_HEADS ** -0.5 * IDX_DIM ** -0.5)
    qi = jnp.concatenate([qi_ref[:, h * IDX_DIM:(h + 1) * IDX_DIM] for h in range(IDX_HEADS)], axis=0).astype(BF16)

    def chunk_off(c):
        return pl.multiple_of(c * ck, ck)

    def key_pos(c):
        return c * ck + lax.broadcasted_iota(I32, (1, ck), 1)

    def score_chunk(c, carry):
        off = chunk_off(c)
        kc = kln_ref[pl.ds(off, ck), :]
        d = lax.dot_general(qi, kc, (((1,), (1,)), ((), ())), preferred_element_type=F32)
        acc = jnp.zeros((tq, ck), F32)
        for h in range(IDX_HEADS):
            acc = acc + jnp.maximum(d[h * tq:(h + 1) * tq, :], 0.0) * w[:, h:h + 1]
        acc = jnp.where(key_pos(c) <= q_pos, acc, -jnp.inf)
        bits = pltpu.bitcast(acc, I32)
        key_ref[:, pl.ds(off, ck)] = bits ^ ((bits >> 31) & 0x7FFFFFFF)
        return carry

    lax.fori_loop(0, n_chunks, score_chunk, 0)

    def bit_body(b, carry):
        cand, cnt_cand = carry
        trial = cand | lax.shift_left(jnp.int32(1), jnp.int32(31) - jnp.asarray(b, I32))
        trial_b = jnp.broadcast_to(trial ^ INT_MIN, (tq, LANES))

        def cnt_chunk(c, cnt):
            kc = key_ref[:, pl.ds(chunk_off(c), ck)]
            for s in range(ck // LANES):
                cnt = cnt + jnp.where(kc[:, s * LANES:(s + 1) * LANES] >= trial_b, 1.0, 0.0)
            return cnt

        cnt = lax.fori_loop(0, n_chunks, cnt_chunk, jnp.zeros((tq, LANES), F32))
        total = jnp.sum(cnt, axis=1, keepdims=True)
        ok = total >= float(top_k)
        return jnp.where(ok, trial, cand), jnp.where(ok, total, cnt_cand)

    n_keys = (jnp.zeros((tq, 1), I32) + n_chunks * ck).astype(F32)
    cand, cnt_ge = lax.fori_loop(0, 32, bit_body, (jnp.zeros((tq, 1), I32), n_keys))
    thr = cand ^ INT_MIN

    def bias_chunk(c, carry):
        off = chunk_off(c)
        sel = (key_ref[:, pl.ds(off, ck)] >= thr) & (key_pos(c) <= q_pos)
        bias_ref[:, pl.ds(off, ck)] = jnp.where(sel, 0.0, NEG_BIG)
        return carry

    lax.fori_loop(0, n_chunks, bias_chunk, 0)

    tie = (cnt_ge > float(top_k)) & (thr > NEG_INF_KEY)

    @pl.when(jnp.max(jnp.where(tie, 1.0, 0.0)) > 0.5)
    def _():
        tri = (lax.broadcasted_iota(I32, (ck, ck), 0) <= lax.broadcasted_iota(I32, (ck, ck), 1)).astype(BF16)

        def gt_chunk(c, cnt):
            kc = key_ref[:, pl.ds(chunk_off(c), ck)]
            return cnt + jnp.sum(jnp.where(kc > thr, 1.0, 0.0), axis=1, keepdims=True)

        need = float(top_k) - lax.fori_loop(0, n_chunks, gt_chunk, jnp.zeros((tq, 1), F32))

        def tie_chunk(c, seen):
            off = chunk_off(c)
            kc = key_ref[:, pl.ds(off, ck)]
            eq = jnp.where(kc == thr, 1.0, 0.0)
            rank = seen + jnp.dot(eq.astype(BF16), tri, preferred_element_type=F32)
            keep = (kc > thr) | ((kc == thr) & ((rank <= need) | jnp.logical_not(tie)))
            sel = keep & (key_pos(c) <= q_pos)
            bias_ref[:, pl.ds(off, ck)] = jnp.where(sel, 0.0, NEG_BIG)
            return seen + jnp.sum(eq, axis=1, keepdims=True)

        lax.fori_loop(0, n_chunks, tie_chunk, jnp.zeros((tq, 1), F32))

    scale = HEAD_DIM ** -0.5
    gq = GQA_GROUP
    for g in range(KV_HEADS):
        qg = jnp.concatenate([q_ref[:, (g * gq + j) * HEAD_DIM:(g * gq + j + 1) * HEAD_DIM] for j in range(gq)],
                             axis=0)

        mrun_ref[...] = jnp.full(mrun_ref.shape, NEG_BIG, F32)

        def logit_chunk(c, carry, g=g, qg=qg):
            off = chunk_off(c)
            kc = k_ref[pl.ds(off, ck), g * HEAD_DIM:(g + 1) * HEAD_DIM]
            s = lax.dot_general(qg, kc, (((1,), (1,)), ((), ())), preferred_element_type=F32)
            bias = bias_ref[:, pl.ds(off, ck)]
            s = s * scale + jnp.concatenate([bias] * gq, axis=0)
            s_ref[:, pl.ds(off, ck)] = s
            m = mrun_ref[...]
            for t in range(ck // LANES):
                m = jnp.maximum(m, s[:, t * LANES:(t + 1) * LANES])
            mrun_ref[...] = m
            return carry

        lax.fori_loop(0, n_chunks, logit_chunk, 0)
        m = jnp.max(mrun_ref[...], axis=1, keepdims=True)

        lrun_ref[...] = jnp.zeros(lrun_ref.shape, F32)
        acc_ref[...] = jnp.zeros(acc_ref.shape, F32)

        def prob_chunk(c, carry, g=g, m=m):
            off = chunk_off(c)
            vc = v_ref[pl.ds(off, ck), g * HEAD_DIM:(g + 1) * HEAD_DIM]
            p = jnp.exp(s_ref[:, pl.ds(off, ck)] - m)
            l = lrun_ref[...]
            for t in range(ck // LANES):
                l = l + p[:, t * LANES:(t + 1) * LANES]
            lrun_ref[...] = l
            acc_ref[...] += jnp.dot(p.astype(BF16), vc, preferred_element_type=F32)
            return carry

        lax.fori_loop(0, n_chunks, prob_chunk, 0)
        out = acc_ref[...] / jnp.sum(lrun_ref[...], axis=1, keepdims=True)
        for j in range(gq):
            h = g * gq + j
            o_ref[:, h * HEAD_DIM:(h + 1) * HEAD_DIM] = out[j * tq:(j + 1) * tq, :].astype(o_ref.dtype)


def _dsa_attention(qkv, idx, kn_g, kn_b, bsz, seq):
    top_k = min(DSA_TOPK_MAX, seq // 4)
    nq = seq // QUERY_BLOCK
    att_w = ATT_HEADS * HEAD_DIM
    kv_w = KV_HEADS * HEAD_DIM
    qi_w = IDX_HEADS * IDX_DIM
    assert seq % KEY_CHUNK == 0 and att_w % kv_w == 0 and qi_w % LANES == 0
    return pl.pallas_call(
        functools.partial(_attn_kernel, top_k=top_k),
        grid=(bsz, nq),
        in_specs=[pl.BlockSpec((QUERY_BLOCK, att_w), lambda b, i: (b * nq + i, 0)),
                  pl.BlockSpec((seq, kv_w), lambda b, i: (b, att_w // kv_w)),
                  pl.BlockSpec((seq, kv_w), lambda b, i: (b, att_w // kv_w + 1)),
                  pl.BlockSpec((QUERY_BLOCK, qi_w), lambda b, i: (b * nq + i, 0)),
                  pl.BlockSpec((seq, LANES), lambda b, i: (b, qi_w // LANES)),
                  pl.BlockSpec((1, IDX_DIM), lambda b, i: (0, 0)),
                  pl.BlockSpec((1, IDX_DIM), lambda b, i: (0, 0))],
        out_specs=pl.BlockSpec((QUERY_BLOCK, att_w), lambda b, i: (b * nq + i, 0)),
        out_shape=jax.ShapeDtypeStruct((bsz * seq, att_w), BF16),
        scratch_shapes=[pltpu.VMEM((seq, IDX_DIM), BF16),
                        pltpu.VMEM((QUERY_BLOCK, seq), I32),
                        pltpu.VMEM((QUERY_BLOCK, seq), F32),
                        pltpu.VMEM((GQA_GROUP * QUERY_BLOCK, seq), F32),
                        pltpu.VMEM((GQA_GROUP * QUERY_BLOCK, LANES), F32),
                        pltpu.VMEM((GQA_GROUP * QUERY_BLOCK, LANES), F32),
                        pltpu.VMEM((GQA_GROUP * QUERY_BLOCK, HEAD_DIM), F32)],
        compiler_params=_params(("parallel", "arbitrary")),
        name="dsa_attention",
    )(qkv, qkv, qkv, idx, idx, kn_g.reshape(1, IDX_DIM), kn_b.reshape(1, IDX_DIM))


def _silu(x):
    return x / (1.0 + jnp.exp(-x))


def _ssd_kernel(xbc_ref, z_ref, dt_ref, cw_ref, cb_ref, dtb_ref, alog_ref, dsk_ref, ng_ref, o_ref,
                xpad_ref, act_ref, y_ref, h_ref):
    c = pl.program_id(1)
    cq = SSD_CHUNK
    width = SSD_HEADS * SSD_HEAD_DIM
    b_off = width
    c_off = width + SSD_GROUPS * SSD_STATE

    @pl.when(c == 0)
    def _():
        xpad_ref[0:SUBLANES, :] = jnp.zeros((SUBLANES, xpad_ref.shape[1]), F32)
        h_ref[...] = jnp.zeros(h_ref.shape, F32)

    xpad_ref[SUBLANES:SUBLANES + cq, :] = xbc_ref[...]
    col = 512
    for j in range(xpad_ref.shape[1] // col):
        cs = slice(j * col, (j + 1) * col)
        acc = cb_ref[:, cs] + jnp.zeros((cq, col), F32)
        for t in range(SSD_CONV):
            r0 = SUBLANES - (SSD_CONV - 1) + t
            acc = acc + xpad_ref[r0:r0 + cq, cs] * cw_ref[t:t + 1, cs]
        act_ref[:, cs] = _silu(acc)
    xpad_ref[0:SUBLANES, :] = xpad_ref[cq:cq + SUBLANES, :]

    xdt_in = dt_ref[:, 0:LANES] + dtb_ref[...]
    dt = jnp.maximum(xdt_in, 0.0) + jnp.log1p(jnp.exp(-jnp.abs(xdt_in)))
    da = dt * (-jnp.exp(alog_ref[...]))
    row = lax.broadcasted_iota(I32, (cq, cq), 0)
    coli = lax.broadcasted_iota(I32, (cq, cq), 1)
    causal = row >= coli
    tril = causal.astype(F32)
    acs = jnp.dot(tril, da, preferred_element_type=F32, precision=lax.Precision.HIGHEST)
    acs_t = acs.T
    dec_in = jnp.exp(acs)
    a_last = acs[cq - 1:cq, :]
    dec_out = jnp.exp(a_last - acs)
    dec_chunk = jnp.exp(a_last)

    for g in range(SSD_GROUPS):
        bg = act_ref[:, b_off + g * SSD_STATE:b_off + (g + 1) * SSD_STATE].astype(BF16)
        cg = act_ref[:, c_off + g * SSD_STATE:c_off + (g + 1) * SSD_STATE].astype(BF16)
        cbm = lax.dot_general(cg, bg, (((1,), (1,)), ((), ())), preferred_element_type=F32)
        for j in range(SSD_HEADS_PER_GROUP):
            hd = g * SSD_HEADS_PER_GROUP + j
            xs = slice(hd * SSD_HEAD_DIM, (hd + 1) * SSD_HEAD_DIM)
            seg = acs[:, hd:hd + 1] - acs_t[hd:hd + 1, :]
            lmat = jnp.exp(jnp.where(causal, seg, -jnp.inf))
            xh = act_ref[:, xs]
            xdt = xh * dt[:, hd:hd + 1]
            y = jnp.dot((cbm * lmat).astype(BF16), xdt.astype(BF16), preferred_element_type=F32)
            hprev = h_ref[hd]
            yoff = lax.dot_general(cg, hprev.astype(BF16), (((1,), (1,)), ((), ())),
                                   preferred_element_type=F32)
            y = y + yoff * dec_in[:, hd:hd + 1] + xh * dsk_ref[:, xs]
            y_ref[:, xs] = y
            st = lax.dot_general((xdt * dec_out[:, hd:hd + 1]).astype(BF16), bg,
                                 (((0,), (0,)), ((), ())), preferred_element_type=F32)
            h_ref[hd] = hprev * dec_chunk[:, hd:hd + 1] + st

    gw = width // SSD_GROUPS
    for g in range(SSD_GROUPS):
        gs = slice(g * gw, (g + 1) * gw)
        gated = y_ref[:, gs] * _silu(z_ref[:, gs])
        ms = jnp.mean(jnp.square(gated), axis=-1, keepdims=True)
        o_ref[:, gs] = (gated * lax.rsqrt(ms + RMS_EPS) * ng_ref[:, gs]).astype(o_ref.dtype)


def _ssd_mixer(ssd_in, conv_w, conv_b, dt_bias, a_log, d_skip, norm_g, bsz, seq):
    nc = seq // SSD_CHUNK
    width = SSD_HEADS * SSD_HEAD_DIM
    xbc_w = width + 2 * SSD_GROUPS * SSD_STATE
    dt_w = ssd_in.shape[1] - xbc_w - width
    assert xbc_w % width == 0 and (xbc_w + width) % dt_w == 0 and dt_w >= LANES
    pad = LANES - SSD_HEADS
    dtb = jnp.pad(dt_bias.astype(F32), (0, pad)).reshape(1, LANES)
    alog = jnp.pad(a_log.astype(F32), (0, pad)).reshape(1, LANES)
    dsk = jnp.repeat(d_skip.astype(F32), SSD_HEAD_DIM).reshape(1, width)
    const = lambda b, c: (0, 0)
    return pl.pallas_call(
        _ssd_kernel,
        grid=(bsz, nc),
        in_specs=[pl.BlockSpec((SSD_CHUNK, xbc_w), lambda b, c: (b * nc + c, 0)),
                  pl.BlockSpec((SSD_CHUNK, width), lambda b, c: (b * nc + c, xbc_w // width)),
                  pl.BlockSpec((SSD_CHUNK, dt_w), lambda b, c: (b * nc + c, (xbc_w + width) // dt_w)),
                  pl.BlockSpec((SSD_CONV, xbc_w), const),
                  pl.BlockSpec((1, xbc_w), const),
                  pl.BlockSpec((1, LANES), const),
                  pl.BlockSpec((1, LANES), const),
                  pl.BlockSpec((1, width), const),
                  pl.BlockSpec((1, width), const)],
        out_specs=pl.BlockSpec((SSD_CHUNK, width), lambda b, c: (b * nc + c, 0)),
        out_shape=jax.ShapeDtypeStruct((bsz * seq, width), BF16),
        scratch_shapes=[pltpu.VMEM((SSD_CHUNK + SUBLANES, xbc_w), F32),
                        pltpu.VMEM((SSD_CHUNK, xbc_w), F32),
                        pltpu.VMEM((SSD_CHUNK, width), F32),
                        pltpu.VMEM((SSD_HEADS, SSD_HEAD_DIM, SSD_STATE), F32)],
        compiler_params=_params(("parallel", "arbitrary")),
        name="ssd_mixer",
    )(ssd_in, ssd_in, ssd_in, conv_w.astype(F32), conv_b.astype(F32).reshape(1, xbc_w), dtb, alog, dsk,
      norm_g.astype(F32).reshape(1, width))


def _layer_norm_rows(x, g, b):
    mu = jnp.mean(x, axis=-1, keepdims=True)
    var = jnp.mean(jnp.square(x - mu), axis=-1, keepdims=True)
    return (x - mu) * lax.rsqrt(var + LN_EPS) * g + b


def _ln_router_kernel(x_ref, mix_ref, g_ref, b_ref, wr_ref, br_ref, o_ref, xp_ref, r_ref, *, alpha):
    x1 = _layer_norm_rows(alpha * x_ref[...] + mix_ref[...], g_ref[...], b_ref[...])
    o_ref[...] = x1
    half = xp_ref.shape[1]
    bits = pltpu.bitcast(x1.astype(BF16).astype(F32), jnp.uint32)
    xp_ref[...] = (bits[:, :half] >> 16) | bits[:, half:]
    logits = jnp.dot(x1, wr_ref[...], preferred_element_type=F32, precision=lax.Precision.HIGHEST)
    logits = logits + br_ref[...]
    rows = logits.shape[0]
    lane = lax.broadcasted_iota(I32, (rows, LANES), 1).astype(F32)
    ng, epg = float(N_EXPERT_GROUPS), float(EXPERTS_PER_GROUP)
    far = float(LANES)

    gmask = lane < ng
    gl = jnp.where(gmask, logits, -jnp.inf)
    ge = jnp.exp(gl - jnp.max(gl, axis=1, keepdims=True))
    gprob = ge / jnp.sum(ge, axis=1, keepdims=True)
    gprob = jnp.where(gmask, gprob, -1.0)
    gw = jnp.max(gprob, axis=1, keepdims=True)
    gsel = jnp.min(jnp.where(gprob == gw, lane, far), axis=1, keepdims=True)

    e_lo = ng + gsel * epg
    emask = (lane >= e_lo) & (lane < e_lo + epg)
    el = jnp.where(emask, logits, -jnp.inf)
    v0 = jnp.max(el, axis=1, keepdims=True)
    i0 = jnp.min(jnp.where(emask & (el == v0), lane, far), axis=1, keepdims=True)
    emask1 = emask & (lane != i0)
    el1 = jnp.where(emask1, logits, -jnp.inf)
    v1 = jnp.max(el1, axis=1, keepdims=True)
    i1 = jnp.min(jnp.where(emask1 & (el1 == v1), lane, far), axis=1, keepdims=True)
    e = jnp.exp(v1 - v0)
    p0 = 1.0 / (1.0 + e)
    p1 = e / (1.0 + e)
    out = jnp.where(lane == 0.0, i0 - ng,
                    jnp.where(lane == 1.0, i1 - ng,
                              jnp.where(lane == 2.0, gw * p0, jnp.where(lane == 3.0, gw * p1, 0.0))))
    r_ref[...] = out


def _ln_router(x, mixed, g, b, w_rg, b_rg, w_re, b_re, alpha, tr=256):
    t, d = x.shape
    ncol = N_EXPERT_GROUPS + N_EXPERTS
    wr = jnp.pad(jnp.concatenate([w_rg, w_re], axis=1).astype(F32), ((0, 0), (0, LANES - ncol)))
    br = jnp.pad(jnp.concatenate([b_rg, b_re]).astype(F32), (0, LANES - ncol)).reshape(1, LANES)
    const = lambda i: (0, 0)
    return pl.pallas_call(
        functools.partial(_ln_router_kernel, alpha=alpha),
        grid=(t // tr,),
        in_specs=[pl.BlockSpec((tr, d), lambda i: (i, 0)),
                  pl.BlockSpec((tr, d), lambda i: (i, 0)),
                  pl.BlockSpec((1, d), const), pl.BlockSpec((1, d), const),
                  pl.BlockSpec((d, LANES), const), pl.BlockSpec((1, LANES), const)],
        out_specs=[pl.BlockSpec((tr, d), lambda i: (i, 0)),
                   pl.BlockSpec((tr, d // 2), lambda i: (i, 0)),
                   pl.BlockSpec((tr, LANES), lambda i: (i, 0))],
        out_shape=[jax.ShapeDtypeStruct((t, d), F32), jax.ShapeDtypeStruct((t, d // 2), jnp.uint32),
                   jax.ShapeDtypeStruct((t, LANES), F32)],
        compiler_params=_params(("parallel",)),
        name="ln1_router",
    )(x, mixed, g.astype(F32).reshape(1, d), b.astype(F32).reshape(1, d), wr, br)


def _row_copy(src_ref, dst_ref, sem, tok, row):
    return pltpu.make_async_copy(src_ref.at[pl.ds(tok, 1)], dst_ref.at[pl.ds(row, 1)], sem)


def _moe_kernel(tok_ref, be_ref, bv_ref, nu_ref, x_ref, wg_ref, wu_ref, wd_ref, o_ref,
                xq_ref, xbf_ref, act_ref, gbuf_ref, dbuf_ref, sem, gsem, dsem, *, nf, nn):
    s = pl.program_id(0)
    p = pl.program_id(1)
    n_blk = pl.num_programs(0)
    n_used = nu_ref[0]
    steps = nf + nn
    sub = MOE_SUB
    n_sub = MOE_ROWS // sub
    half = xq_ref.shape[1]
    ks = MOE_K_SPLIT
    kq = wg_ref.shape[1] // ks

    def weights_dma(blk, ph, start):
        e = be_ref[blk]

        @pl.when(ph < nf)
        def _():
            slot = lax.rem(blk * nf + ph, MOE_RING)
            col = pl.multiple_of(ph * MOE_F_CHUNK, MOE_F_CHUNK)
            for q in range(ks):
                rows = pl.ds(q * kq, kq)
                for j, w_ref in enumerate((wg_ref, wu_ref)):
                    cp = pltpu.make_async_copy(w_ref.at[e, rows, pl.ds(col, MOE_F_CHUNK)],
                                               gbuf_ref.at[slot, j, rows], gsem.at[slot])
                    cp.start() if start else cp.wait()

        @pl.when(ph >= nf)
        def _():
            k = ph - nf
            slot = lax.rem(blk * nn + k, MOE_RING)
            for j in range(2):
                col = pl.multiple_of((k + j * nn) * MOE_N_CHUNK, MOE_N_CHUNK)
                cp = pltpu.make_async_copy(wd_ref.at[e, :, pl.ds(col, MOE_N_CHUNK)],
                                           dbuf_ref.at[slot, j], dsem.at[slot])
                cp.start() if start else cp.wait()

    @pl.when((s == 0) & (p == 0))
    def _():
        for a in range(MOE_LOOKAHEAD):
            weights_dma(s, p + a, True)

    ahead = p + MOE_LOOKAHEAD
    blk_a = jnp.where(ahead >= steps, s + 1, s)
    ph_a = jnp.where(ahead >= steps, ahead - steps, ahead)

    @pl.when(blk_a < n_used)
    def _():
        weights_dma(blk_a, ph_a, True)

    @pl.when(s < n_used)
    def _():
        weights_dma(s, p, False)

    def nsub_of(blk):
        return (bv_ref[blk] + (sub - 1)) // sub

    def issue(blk, r):
        base = blk * MOE_ROWS + r * sub

        def body(i, carry):
            for u in range(DMA_UNROLL):
                j = i * DMA_UNROLL + u
                _row_copy(x_ref, xq_ref, sem, tok_ref[base + j], r * sub + j).start()
            return carry
        lax.fori_loop(0, sub // DMA_UNROLL, body, 0)

    nsub = nsub_of(s)

    @pl.when((s == 0) & (p == 0))
    def _():
        for r in range(n_sub):
            @pl.when(r < nsub)
            def _(r=r):
                issue(s, r)

    @pl.when(p == 0)
    def _():
        for r in range(n_sub):
            @pl.when(r < nsub)
            def _(r=r):
                pltpu.make_async_copy(x_ref.at[pl.ds(0, sub)], xq_ref.at[pl.ds(r * sub, sub)], sem).wait()
        for r in range(n_sub):
            @pl.when(r < nsub)
            def _(r=r):
                rs = slice(r * sub, (r + 1) * sub)
                word = xq_ref[rs, :]
                xbf_ref[rs, 0:half] = pltpu.bitcast(word << 16, F32).astype(BF16)
                xbf_ref[rs, half:2 * half] = pltpu.bitcast(word & jnp.uint32(0xFFFF0000), F32).astype(BF16)

    nxt = jnp.minimum(s + 1, n_blk - 1)
    for r in range(n_sub):
        @pl.when((p == r + 1) & (s + 1 < n_blk) & (r < nsub_of(nxt)))
        def _(r=r):
            issue(nxt, r)

    for k in range(1, n_sub + 1):
        m = k * sub

        @pl.when((p < nf) & (nsub == k))
        def _(m=m):
            slot = lax.rem(s * nf + p, MOE_RING)
            gate = up = None
            for q in range(ks):
                xr = xbf_ref[0:m, q * kq:(q + 1) * kq]
                gq = jnp.dot(xr, gbuf_ref[slot, 0, q * kq:(q + 1) * kq, :].astype(BF16), preferred_element_type=F32)
                uq = jnp.dot(xr, gbuf_ref[slot, 1, q * kq:(q + 1) * kq, :].astype(BF16), preferred_element_type=F32)
                gate = gq if gate is None else gate + gq
                up = uq if up is None else up + uq
            act_ref[p, 0:m, :] = (_silu(gate) * up).astype(BF16)

    def down(slot, j, m):
        acc = None
        for q in range(nf):
            wq = dbuf_ref[slot, j, q * MOE_F_CHUNK:(q + 1) * MOE_F_CHUNK, :].astype(BF16)
            t = jnp.dot(act_ref[q, 0:m, :], wq, preferred_element_type=F32)
            acc = t if acc is None else acc + t
        return pltpu.bitcast(acc.astype(BF16).astype(F32), jnp.uint32)

    for k in range(0, n_sub + 1):
        m = k * sub

        @pl.when((p >= nf) & (nsub == k))
        def _(m=m):
            if m > 0:
                slot = lax.rem(s * nn + p - nf, MOE_RING)
                o_ref[0:m, :] = (down(slot, 0, m) >> 16) | down(slot, 1, m)
            if m < MOE_ROWS:
                o_ref[m:MOE_ROWS, :] = jnp.zeros((MOE_ROWS - m, o_ref.shape[1]), jnp.uint32)


def _moe_ffn(xp, row_tok, blk_e, blk_valid, n_used, w_gate, w_up, w_down):
    n_rows = row_tok.shape[0]
    n_e, d, f = w_gate.shape
    nf = f // MOE_F_CHUNK
    nn = (d // 2) // MOE_N_CHUNK
    n_blk = n_rows // MOE_ROWS
    assert nf + nn > MOE_ROWS // MOE_SUB and d % MOE_K_SPLIT == 0
    assert MOE_LOOKAHEAD < MOE_RING and MOE_LOOKAHEAD <= min(nf, nn)

    def out_map(s, p, tok, be, bv, nu):
        return (s, jnp.maximum(p - nf, 0))

    hbm = pl.BlockSpec(memory_space=pl.ANY)
    return pl.pallas_call(
        functools.partial(_moe_kernel, nf=nf, nn=nn),
        grid_spec=pltpu.PrefetchScalarGridSpec(
            num_scalar_prefetch=4,
            grid=(n_blk, nf + nn),
            in_specs=[hbm, hbm, hbm, hbm],
            out_specs=pl.BlockSpec((MOE_ROWS, MOE_N_CHUNK), out_map),
            scratch_shapes=[pltpu.VMEM((MOE_ROWS, d // 2), jnp.uint32),
                            pltpu.VMEM((MOE_ROWS, d), BF16),
                            pltpu.VMEM((nf, MOE_ROWS, MOE_F_CHUNK), BF16),
                            pltpu.VMEM((MOE_RING, 2, d, MOE_F_CHUNK), F32),
                            pltpu.VMEM((MOE_RING, 2, f, MOE_N_CHUNK), F32),
                            pltpu.SemaphoreType.DMA(()),
                            pltpu.SemaphoreType.DMA((MOE_RING,)),
                            pltpu.SemaphoreType.DMA((MOE_RING,))]),
        out_shape=jax.ShapeDtypeStruct((n_rows, d // 2), jnp.uint32),
        compiler_params=_params(("arbitrary", "arbitrary")),
        name="moe_ffn",
    )(row_tok, blk_e, blk_valid, n_used, xp, w_gate, w_up, w_down)


def _combine_kernel(dest_ref, y_ref, x_ref, r_ref, g_ref, b_ref, o_ref, buf_ref, sem, *, alpha, tt):
    s = pl.program_id(0)
    n = pl.num_programs(0)

    rows = 2 * tt

    def start(blk, slot):
        base = blk * rows

        def body(i, carry):
            for u in range(DMA_UNROLL):
                r = i * DMA_UNROLL + u
                pltpu.make_async_copy(y_ref.at[pl.ds(dest_ref[base + r], 1)],
                                      buf_ref.at[slot, pl.ds(r, 1)], sem.at[slot]).start()
            return carry
        lax.fori_loop(0, rows // DMA_UNROLL, body, 0)

    @pl.when(s == 0)
    def _():
        start(s, 0)

    for par in range(2):
        @pl.when((s % 2 == par) & (s + 1 < n))
        def _(par=par):
            start(s + 1, 1 - par)

    slot = s % 2
    pltpu.make_async_copy(y_ref.at[pl.ds(0, rows)], buf_ref.at[slot], sem.at[slot]).wait()
    g0 = r_ref[:, 2:3]
    g1 = r_ref[:, 3:4]
    w0 = buf_ref[slot, 0:tt, :]
    w1 = buf_ref[slot, tt:2 * tt, :]
    half = w0.shape[1]
    himask = jnp.uint32(0xFFFF0000)
    y_lo = g0 * pltpu.bitcast(w0 << 16, F32) + g1 * pltpu.bitcast(w1 << 16, F32)
    y_hi = g0 * pltpu.bitcast(w0 & himask, F32) + g1 * pltpu.bitcast(w1 & himask, F32)
    r_lo = alpha * x_ref[:, 0:half] + y_lo
    r_hi = alpha * x_ref[:, half:2 * half] + y_hi
    inv_d = 1.0 / (2 * half)
    mu = (jnp.sum(r_lo, axis=-1, keepdims=True) + jnp.sum(r_hi, axis=-1, keepdims=True)) * inv_d
    var = (jnp.sum(jnp.square(r_lo - mu), axis=-1, keepdims=True)
           + jnp.sum(jnp.square(r_hi - mu), axis=-1, keepdims=True)) * inv_d
    rstd = lax.rsqrt(var + LN_EPS)
    o_ref[:, 0:half] = (r_lo - mu) * rstd * g_ref[:, 0:half] + b_ref[:, 0:half]
    o_ref[:, half:2 * half] = (r_hi - mu) * rstd * g_ref[:, half:2 * half] + b_ref[:, half:2 * half]


def _combine_ln(y_rows, dest, x1, route, g, b, alpha, tt=128):
    t, d = x1.shape
    const = lambda i, dr: (0, 0)
    return pl.pallas_call(
        functools.partial(_combine_kernel, alpha=alpha, tt=tt),
        grid_spec=pltpu.PrefetchScalarGridSpec(
            num_scalar_prefetch=1,
            grid=(t // tt,),
            in_specs=[pl.BlockSpec(memory_space=pl.ANY),
                      pl.BlockSpec((tt, d), lambda i, dr: (i, 0)),
                      pl.BlockSpec((tt, LANES), lambda i, dr: (i, 0)),
                      pl.BlockSpec((1, d), const), pl.BlockSpec((1, d), const)],
            out_specs=pl.BlockSpec((tt, d), lambda i, dr: (i, 0)),
            scratch_shapes=[pltpu.VMEM((2, 2 * tt, d // 2), jnp.uint32),
                            pltpu.SemaphoreType.DMA((2,))]),
        out_shape=jax.ShapeDtypeStruct((t, d), F32),
        compiler_params=_params(("arbitrary",)),
        name="moe_combine_ln2",
    )(dest, y_rows, x1, route, g.astype(F32).reshape(1, d), b.astype(F32).reshape(1, d))


def _dispatch_plan(expert_id, tt):
    t = expert_id.shape[0]
    m = 2 * t
    e_flat = expert_id.reshape(m)
    onehot = (e_flat[:, None] == jnp.arange(N_EXPERTS, dtype=I32)[None, :]).astype(I32)
    counts = jnp.sum(onehot, axis=0)
    rank = jnp.sum((jnp.cumsum(onehot, axis=0) - onehot) * onehot, axis=1)
    pcounts = (counts + MOE_ROWS - 1) // MOE_ROWS * MOE_ROWS
    pends = jnp.cumsum(pcounts)
    pstarts = pends - pcounts
    dest = (pstarts[e_flat] + rank).astype(I32)
    n_blk = -(-m // MOE_ROWS) + N_EXPERTS
    n_rows = n_blk * MOE_ROWS
    tok = jnp.arange(m, dtype=I32) // 2
    row_tok = jnp.zeros((n_rows,), I32).at[dest].set(tok)
    n_used = (pends[-1] // MOE_ROWS).astype(I32)
    blk_idx = jnp.arange(n_blk, dtype=I32)
    blk_start = jnp.minimum(blk_idx, n_used - 1) * MOE_ROWS
    blk_e = jnp.minimum(jnp.searchsorted(pends, blk_start, side='right'), N_EXPERTS - 1).astype(I32)
    blk_valid = jnp.clip(counts[blk_e] - (blk_start - pstarts[blk_e]), 0, MOE_ROWS)
    blk_valid = jnp.where(blk_idx < n_used, blk_valid, 0).astype(I32)
    dest_tiles = dest.reshape(t // tt, tt, 2).transpose(0, 2, 1).reshape(m)
    return row_tok, blk_e, blk_valid, n_used.reshape(1), dest_tiles


def kernel(x, w_in, idx_kn_g, idx_kn_b, conv_w, conv_b, dt_bias, a_log, d_skip, ssd_norm_g, w_out,
           ln1_g, ln1_b, w_rg, b_rg, w_re, b_re, w_gate, w_up, w_down, ln2_g, ln2_b):
    bsz, seq, d = x.shape
    depth = w_in.shape[0]
    alpha = (2 * depth) ** 0.25
    att_w = ATT_HEADS * HEAD_DIM
    kv_w = KV_HEADS * HEAD_DIM
    qi_w = IDX_HEADS * IDX_DIM
    ssd_w = SSD_HEADS * SSD_HEAD_DIM
    xbc_w = ssd_w + 2 * SSD_GROUPS * SSD_STATE
    sizes = (att_w, kv_w, kv_w, qi_w, IDX_DIM, IDX_HEADS, ssd_w, xbc_w, SSD_HEADS)
    offs = [0]
    for sz in sizes:
        offs.append(offs[-1] + sz)
    tt = 128
    xf = x.reshape(bsz * seq, d)
    for l in range(depth):
        w = w_in[l].astype(BF16)
        zpad = lambda n: jnp.zeros((d, n), BF16)
        w_qkv = w[:, offs[0]:offs[3]]
        w_idx = jnp.concatenate([w[:, offs[3]:offs[6]], zpad(LANES - IDX_DIM - IDX_HEADS)], axis=1)
        w_ssd = jnp.concatenate([w[:, offs[7]:offs[8]], w[:, offs[6]:offs[7]], w[:, offs[8]:offs[9]],
                                 zpad(2 * LANES - SSD_HEADS)], axis=1)
        qkv, xbf = _matmul(xf, w_qkv, BF16, 512, 512)
        idx = _matmul(xbf, w_idx, F32, 1024, 384)
        ssd_in = _matmul(xbf, w_ssd, F32, 1024, 256)
        att = _dsa_attention(qkv, idx, idx_kn_g[l], idx_kn_b[l], bsz, seq)
        ssd = _ssd_mixer(ssd_in, conv_w[l], conv_b[l], dt_bias[l], a_log[l], d_skip[l], ssd_norm_g[l], bsz, seq)
        mixed = _matmul_pair(att, ssd, w_out[l].astype(BF16), F32, 1024, 512)
        x1, x1p, route = _ln_router(xf, mixed, ln1_g[l], ln1_b[l], w_rg[l], b_rg[l], w_re[l], b_re[l], alpha)
        expert_id = route[:, 0:2].astype(I32)
        row_tok, blk_e, blk_valid, n_used, dest_tiles = _dispatch_plan(expert_id, tt)
        y_rows = _moe_ffn(x1p, row_tok, blk_e, blk_valid, n_used, w_gate[l], w_up[l], w_down[l])
        xf = _combine_ln(y_rows, dest_tiles, x1, route, ln2_g[l], ln2_b[l], alpha, tt)
    return xf.reshape(bsz, seq, d)
```

```python
import functools

import jax
import jax.numpy as jnp
from jax import lax
from jax.experimental import pallas as pl
from jax.experimental.pallas import tpu as pltpu

F32 = jnp.float32
BF16 = jnp.bfloat16
I32 = jnp.int32

HEAD_DIM = 128
KV_HEADS = 4
GQA_GROUP = 4
ATT_HEADS = KV_HEADS * GQA_GROUP
IDX_HEADS = 16
IDX_DIM = 64
DSA_TOPK_MAX = 256
QUERY_BLOCK = 128
SSD_HEAD_DIM = 64
SSD_GROUPS = 8
SSD_HEADS_PER_GROUP = 4
SSD_HEADS = SSD_GROUPS * SSD_HEADS_PER_GROUP
SSD_STATE = 128
SSD_CONV = 4
SSD_CHUNK = 128
N_EXPERT_GROUPS = 8
EXPERTS_PER_GROUP = 8
N_EXPERTS = 64
LN_EPS = 1e-5
RMS_EPS = 1e-5

LANES = 128
SUBLANES = 8
VMEM_LIMIT = 56 * 1024 * 1024

KEY_CHUNK = 512
MOE_ROWS = 768
MOE_SUB = 256
MOE_F_CHUNK = 256
MOE_N_CHUNK = 256
MOE_K_SPLIT = 4
MOE_RING = 3
MOE_LOOKAHEAD = 2
DMA_UNROLL = 8
PLAN_TILE = 512
NEG_BIG = -1e30
INT_MIN = -2 ** 31
NEG_INF_KEY = -2139095041


def _params(sem):
    return pltpu.CompilerParams(dimension_semantics=sem, vmem_limit_bytes=VMEM_LIMIT)


def _mm_cast_kernel(a_ref, b_ref, o_ref, abf_ref):
    @pl.when(pl.program_id(1) == 0)
    def _():
        abf_ref[...] = a_ref[...].astype(BF16)

    o_ref[...] = jnp.dot(abf_ref[...], b_ref[...], preferred_element_type=F32).astype(o_ref.dtype)


def _mm_kernel(a_ref, b_ref, o_ref):
    o_ref[...] = jnp.dot(a_ref[...], b_ref[...], preferred_element_type=F32).astype(o_ref.dtype)


def _mm_pair_kernel(a1_ref, a2_ref, b_ref, o_ref):
    k1 = a1_ref.shape[1]
    acc = jnp.dot(a1_ref[...], b_ref[0:k1, :], preferred_element_type=F32)
    acc = acc + jnp.dot(a2_ref[...], b_ref[k1:, :], preferred_element_type=F32)
    o_ref[...] = acc.astype(o_ref.dtype)


def _matmul_pair(a1, a2, b, out_dtype, tm, tn):
    m, k1 = a1.shape
    k2 = a2.shape[1]
    n = b.shape[1]
    tm = min(tm, m)
    assert m % tm == 0 and n % tn == 0 and b.shape[0] == k1 + k2
    return pl.pallas_call(
        _mm_pair_kernel,
        grid=(m // tm, n // tn),
        in_specs=[pl.BlockSpec((tm, k1), lambda i, j: (i, 0)),
                  pl.BlockSpec((tm, k2), lambda i, j: (i, 0)),
                  pl.BlockSpec((k1 + k2, tn), lambda i, j: (0, j))],
        out_specs=pl.BlockSpec((tm, tn), lambda i, j: (i, j)),
        out_shape=jax.ShapeDtypeStruct((m, n), out_dtype),
        compiler_params=_params(("parallel", "arbitrary")),
        name="matmul_pair",
    )(a1, a2, b)


def _matmul(a, b, out_dtype, tm, tn):
    m, k = a.shape
    n = b.shape[1]
    tm = min(tm, m)
    assert m % tm == 0 and n % tn == 0
    cast = a.dtype != BF16
    out_specs = pl.BlockSpec((tm, tn), lambda i, j: (i, j))
    out_shape = jax.ShapeDtypeStruct((m, n), out_dtype)
    if cast:
        out_specs = [out_specs, pl.BlockSpec((tm, k), lambda i, j: (i, 0))]
        out_shape = [out_shape, jax.ShapeDtypeStruct((m, k), BF16)]
    return pl.pallas_call(
        _mm_cast_kernel if cast else _mm_kernel,
        grid=(m // tm, n // tn),
        in_specs=[pl.BlockSpec((tm, k), lambda i, j: (i, 0)),
                  pl.BlockSpec((k, tn), lambda i, j: (0, j))],
        out_specs=out_specs,
        out_shape=out_shape,
        compiler_params=_params(("parallel", "arbitrary")),
        name="matmul_cast" if cast else "matmul",
    )(a, b)


def _attn_kernel(q_ref, k_ref, v_ref, qi_ref, kw_ref, g_ref, b_ref, o_ref,
                 kln_ref, key_ref, bias_ref, s_ref, mrun_ref, lrun_ref, acc_ref, *, top_k):
    i = pl.program_id(1)
    tq = QUERY_BLOCK
    ck = KEY_CHUNK

    @pl.when(i == 0)
    def _():
        kx = kw_ref[:, 0:IDX_DIM]
        mu = jnp.mean(kx, axis=-1, keepdims=True)
        var = jnp.mean(jnp.square(kx - mu), axis=-1, keepdims=True)
        y = (kx - mu) * lax.rsqrt(var + LN_EPS)
        kln_ref[...] = (y * g_ref[...] + b_ref[...]).astype(BF16)

    q_start = i * tq
    n_chunks = (q_start + tq + ck - 1) // ck
    q_pos = q_start + lax.broadcasted_iota(I32, (tq, 1), 0)
    w = kw_ref[pl.ds(pl.multiple_of(q_start, tq), tq), IDX_DIM:IDX_DIM + IDX_HEADS]
    w = w * (IDX_HEADS ** -0.5 * IDX_DIM ** -0.5)
    qi = jnp.concatenate([qi_ref[:, h * IDX_DIM:(h + 1) * IDX_DIM] for h in range(IDX_HEADS)], axis=0).astype(BF16)

    def chunk_off(c):
        return pl.multiple_of(c * ck, ck)

    def key_pos(c):
        return c * ck + lax.broadcasted_iota(I32, (1, ck), 1)

    def score_chunk(c, carry):
        off = chunk_off(c)
        kc = kln_ref[pl.ds(off, ck), :]
        d = lax.dot_general(qi, kc, (((1,), (1,)), ((), ())), preferred_element_type=F32)
        acc = jnp.zeros((tq, ck), F32)
        for h in range(IDX_HEADS):
            acc = acc + jnp.maximum(d[h * tq:(h + 1) * tq, :], 0.0) * w[:, h:h + 1]
        acc = jnp.where(key_pos(c) <= q_pos, acc, -jnp.inf)
        bits = pltpu.bitcast(acc, I32)
        key_ref[:, pl.ds(off, ck)] = bits ^ ((bits >> 31) & 0x7FFFFFFF)
        return carry

    lax.fori_loop(0, n_chunks, score_chunk, 0)

    def bit_body(b, carry):
        cand, cnt_cand = carry
        trial = cand | lax.shift_left(jnp.int32(1), jnp.int32(31) - jnp.asarray(b, I32))
        trial_b = jnp.broadcast_to(trial ^ INT_MIN, (tq, LANES))

        def cnt_chunk(c, cnt):
            kc = key_ref[:, pl.ds(chunk_off(c), ck)]
            for s in range(ck // LANES):
                cnt = cnt + jnp.where(kc[:, s * LANES:(s + 1) * LANES] >= trial_b, 1.0, 0.0)
            return cnt

        cnt = lax.fori_loop(0, n_chunks, cnt_chunk, jnp.zeros((tq, LANES), F32))
        total = jnp.sum(cnt, axis=1, keepdims=True)
        ok = total >= float(top_k)
        return jnp.where(ok, trial, cand), jnp.where(ok, total, cnt_cand)

    n_keys = (jnp.zeros((tq, 1), I32) + n_chunks * ck).astype(F32)
    cand, cnt_ge = lax.fori_loop(0, 32, bit_body, (jnp.zeros((tq, 1), I32), n_keys))
    thr = cand ^ INT_MIN

    def bias_chunk(c, carry):
        off = chunk_off(c)
        sel = (key_ref[:, pl.ds(off, ck)] >= thr) & (key_pos(c) <= q_pos)
        bias_ref[:, pl.ds(off, ck)] = jnp.where(sel, 0.0, NEG_BIG)
        return carry

    lax.fori_loop(0, n_chunks, bias_chunk, 0)

    tie = (cnt_ge > float(top_k)) & (thr > NEG_INF_KEY)

    @pl.when(jnp.max(jnp.where(tie, 1.0, 0.0)) > 0.5)
    def _():
        tri = (lax.broadcasted_iota(I32, (ck, ck), 0) <= lax.broadcasted_iota(I32, (ck, ck), 1)).astype(BF16)

        def gt_chunk(c, cnt):
            kc = key_ref[:, pl.ds(chunk_off(c), ck)]
            return cnt + jnp.sum(jnp.where(kc > thr, 1.0, 0.0), axis=1, keepdims=True)

        need = float(top_k) - lax.fori_loop(0, n_chunks, gt_chunk, jnp.zeros((tq, 1), F32))

        def tie_chunk(c, seen):
            off = chunk_off(c)
            kc = key_ref[:, pl.ds(off, ck)]
            eq = jnp.where(kc == thr, 1.0, 0.0)
            rank = seen + jnp.dot(eq.astype(BF16), tri, preferred_element_type=F32)
            keep = (kc > thr) | ((kc == thr) & ((rank <= need) | jnp.logical_not(tie)))
            sel = keep & (key_pos(c) <= q_pos)
            bias_ref[:, pl.ds(off, ck)] = jnp.where(sel, 0.0, NEG_BIG)
            return seen + jnp.sum(eq, axis=1, keepdims=True)

        lax.fori_loop(0, n_chunks, tie_chunk, jnp.zeros((tq, 1), F32))

    scale = HEAD_DIM ** -0.5
    gq = GQA_GROUP
    for g in range(KV_HEADS):
        qg = jnp.concatenate([q_ref[:, (g * gq + j) * HEAD_DIM:(g * gq + j + 1) * HEAD_DIM] for j in range(gq)],
                             axis=0)

        mrun_ref[...] = jnp.full(mrun_ref.shape, NEG_BIG, F32)

        def logit_chunk(c, carry, g=g, qg=qg):
            off = chunk_off(c)
            kc = k_ref[pl.ds(off, ck), g * HEAD_DIM:(g + 1) * HEAD_DIM]
            s = lax.dot_general(qg, kc, (((1,), (1,)), ((), ())), preferred_element_type=F32)
            bias = bias_ref[:, pl.ds(off, ck)]
            s = s * scale + jnp.concatenate([bias] * gq, axis=0)
            s_ref[:, pl.ds(off, ck)] = s
            m = mrun_ref[...]
            for t in range(ck // LANES):
                m = jnp.maximum(m, s[:, t * LANES:(t + 1) * LANES])
            mrun_ref[...] = m
            return carry

        lax.fori_loop(0, n_chunks, logit_chunk, 0)
        m = jnp.max(mrun_ref[...], axis=1, keepdims=True)

        lrun_ref[...] = jnp.zeros(lrun_ref.shape, F32)
        acc_ref[...] = jnp.zeros(acc_ref.shape, F32)

        def prob_chunk(c, carry, g=g, m=m):
            off = chunk_off(c)
            vc = v_ref[pl.ds(off, ck), g * HEAD_DIM:(g + 1) * HEAD_DIM]
            p = jnp.exp(s_ref[:, pl.ds(off, ck)] - m)
            l = lrun_ref[...]
            for t in range(ck // LANES):
                l = l + p[:, t * LANES:(t + 1) * LANES]
            lrun_ref[...] = l
            acc_ref[...] += jnp.dot(p.astype(BF16), vc, preferred_element_type=F32)
            return carry

        lax.fori_loop(0, n_chunks, prob_chunk, 0)
        out = acc_ref[...] / jnp.sum(lrun_ref[...], axis=1, keepdims=True)
        for j in range(gq):
            h = g * gq + j
            o_ref[:, h * HEAD_DIM:(h + 1) * HEAD_DIM] = out[j * tq:(j + 1) * tq, :].astype(o_ref.dtype)


def _dsa_attention(qkv, idx, kn_g, kn_b, bsz, seq):
    top_k = min(DSA_TOPK_MAX, seq // 4)
    nq = seq // QUERY_BLOCK
    att_w = ATT_HEADS * HEAD_DIM
    kv_w = KV_HEADS * HEAD_DIM
    qi_w = IDX_HEADS * IDX_DIM
    assert seq % KEY_CHUNK == 0 and att_w % kv_w == 0 and qi_w % LANES == 0
    return pl.pallas_call(
        functools.partial(_attn_kernel, top_k=top_k),
        grid=(bsz, nq),
        in_specs=[pl.BlockSpec((QUERY_BLOCK, att_w), lambda b, i: (b * nq + i, 0)),
                  pl.BlockSpec((seq, kv_w), lambda b, i: (b, att_w // kv_w)),
                  pl.BlockSpec((seq, kv_w), lambda b, i: (b, att_w // kv_w + 1)),
                  pl.BlockSpec((QUERY_BLOCK, qi_w), lambda b, i: (b * nq + i, 0)),
                  pl.BlockSpec((seq, LANES), lambda b, i: (b, qi_w // LANES)),
                  pl.BlockSpec((1, IDX_DIM), lambda b, i: (0, 0)),
                  pl.BlockSpec((1, IDX_DIM), lambda b, i: (0, 0))],
        out_specs=pl.BlockSpec((QUERY_BLOCK, att_w), lambda b, i: (b * nq + i, 0)),
        out_shape=jax.ShapeDtypeStruct((bsz * seq, att_w), BF16),
        scratch_shapes=[pltpu.VMEM((seq, IDX_DIM), BF16),
                        pltpu.VMEM((QUERY_BLOCK, seq), I32),
                        pltpu.VMEM((QUERY_BLOCK, seq), F32),
                        pltpu.VMEM((GQA_GROUP * QUERY_BLOCK, seq), F32),
                        pltpu.VMEM((GQA_GROUP * QUERY_BLOCK, LANES), F32),
                        pltpu.VMEM((GQA_GROUP * QUERY_BLOCK, LANES), F32),
                        pltpu.VMEM((GQA_GROUP * QUERY_BLOCK, HEAD_DIM), F32)],
        compiler_params=_params(("parallel", "arbitrary")),
        name="dsa_attention",
    )(qkv, qkv, qkv, idx, idx, kn_g.reshape(1, IDX_DIM), kn_b.reshape(1, IDX_DIM))


def _silu(x):
    return x / (1.0 + jnp.exp(-x))


def _ssd_kernel(xbc_ref, z_ref, dt_ref, cw_ref, cb_ref, dtb_ref, alog_ref, dsk_ref, ng_ref, o_ref,
                xpad_ref, act_ref, y_ref, h_ref):
    c = pl.program_id(1)
    cq = SSD_CHUNK
    width = SSD_HEADS * SSD_HEAD_DIM
    b_off = width
    c_off = width + SSD_GROUPS * SSD_STATE

    @pl.when(c == 0)
    def _():
        xpad_ref[0:SUBLANES, :] = jnp.zeros((SUBLANES, xpad_ref.shape[1]), F32)
        h_ref[...] = jnp.zeros(h_ref.shape, F32)

    xpad_ref[SUBLANES:SUBLANES + cq, :] = xbc_ref[...]
    col = 512
    for j in range(xpad_ref.shape[1] // col):
        cs = slice(j * col, (j + 1) * col)
        acc = cb_ref[:, cs] + jnp.zeros((cq, col), F32)
        for t in range(SSD_CONV):
            r0 = SUBLANES - (SSD_CONV - 1) + t
            acc = acc + xpad_ref[r0:r0 + cq, cs] * cw_ref[t:t + 1, cs]
        act_ref[:, cs] = _silu(acc)
    xpad_ref[0:SUBLANES, :] = xpad_ref[cq:cq + SUBLANES, :]

    xdt_in = dt_ref[:, 0:LANES] + dtb_ref[...]
    dt = jnp.maximum(xdt_in, 0.0) + jnp.log1p(jnp.exp(-jnp.abs(xdt_in)))
    da = dt * (-jnp.exp(alog_ref[...]))
    row = lax.broadcasted_iota(I32, (cq, cq), 0)
    coli = lax.broadcasted_iota(I32, (cq, cq), 1)
    causal = row >= coli
    tril = causal.astype(F32)
    acs = jnp.dot(tril, da, preferred_element_type=F32, precision=lax.Precision.HIGHEST)
    acs_t = acs.T
    dec_in = jnp.exp(acs)
    a_last = acs[cq - 1:cq, :]
    dec_out = jnp.exp(a_last - acs)
    dec_chunk = jnp.exp(a_last)

    for g in range(SSD_GROUPS):
        bg = act_ref[:, b_off + g * SSD_STATE:b_off + (g + 1) * SSD_STATE].astype(BF16)
        cg = act_ref[:, c_off + g * SSD_STATE:c_off + (g + 1) * SSD_STATE].astype(BF16)
        cbm = lax.dot_general(cg, bg, (((1,), (1,)), ((), ())), preferred_element_type=F32)
        for j in range(SSD_HEADS_PER_GROUP):
            hd = g * SSD_HEADS_PER_GROUP + j
            xs = slice(hd * SSD_HEAD_DIM, (hd + 1) * SSD_HEAD_DIM)
            seg = acs[:, hd:hd + 1] - acs_t[hd:hd + 1, :]
            lmat = jnp.exp(jnp.where(causal, seg, -jnp.inf))
            xh = act_ref[:, xs]
            xdt = xh * dt[:, hd:hd + 1]
            y = jnp.dot((cbm * lmat).astype(BF16), xdt.astype(BF16), preferred_element_type=F32)
            hprev = h_ref[hd]
            yoff = lax.dot_general(cg, hprev.astype(BF16), (((1,), (1,)), ((), ())),
                                   preferred_element_type=F32)
            y = y + yoff * dec_in[:, hd:hd + 1] + xh * dsk_ref[:, xs]
            y_ref[:, xs] = y
            st = lax.dot_general((xdt * dec_out[:, hd:hd + 1]).astype(BF16), bg,
                                 (((0,), (0,)), ((), ())), preferred_element_type=F32)
            h_ref[hd] = hprev * dec_chunk[:, hd:hd + 1] + st

    gw = width // SSD_GROUPS
    for g in range(SSD_GROUPS):
        gs = slice(g * gw, (g + 1) * gw)
        gated = y_ref[:, gs] * _silu(z_ref[:, gs])
        ms = jnp.mean(jnp.square(gated), axis=-1, keepdims=True)
        o_ref[:, gs] = (gated * lax.rsqrt(ms + RMS_EPS) * ng_ref[:, gs]).astype(o_ref.dtype)


def _ssd_mixer(ssd_in, conv_w, conv_b, dt_bias, a_log, d_skip, norm_g, bsz, seq):
    nc = seq // SSD_CHUNK
    width = SSD_HEADS * SSD_HEAD_DIM
    xbc_w = width + 2 * SSD_GROUPS * SSD_STATE
    dt_w = ssd_in.shape[1] - xbc_w - width
    assert xbc_w % width == 0 and (xbc_w + width) % dt_w == 0 and dt_w >= LANES
    pad = LANES - SSD_HEADS
    dtb = jnp.pad(dt_bias.astype(F32), (0, pad)).reshape(1, LANES)
    alog = jnp.pad(a_log.astype(F32), (0, pad)).reshape(1, LANES)
    dsk = jnp.repeat(d_skip.astype(F32), SSD_HEAD_DIM).reshape(1, width)
    const = lambda b, c: (0, 0)
    return pl.pallas_call(
        _ssd_kernel,
        grid=(bsz, nc),
        in_specs=[pl.BlockSpec((SSD_CHUNK, xbc_w), lambda b, c: (b * nc + c, 0)),
                  pl.BlockSpec((SSD_CHUNK, width), lambda b, c: (b * nc + c, xbc_w // width)),
                  pl.BlockSpec((SSD_CHUNK, dt_w), lambda b, c: (b * nc + c, (xbc_w + width) // dt_w)),
                  pl.BlockSpec((SSD_CONV, xbc_w), const),
                  pl.BlockSpec((1, xbc_w), const),
                  pl.BlockSpec((1, LANES), const),
                  pl.BlockSpec((1, LANES), const),
                  pl.BlockSpec((1, width), const),
                  pl.BlockSpec((1, width), const)],
        out_specs=pl.BlockSpec((SSD_CHUNK, width), lambda b, c: (b * nc + c, 0)),
        out_shape=jax.ShapeDtypeStruct((bsz * seq, width), BF16),
        scratch_shapes=[pltpu.VMEM((SSD_CHUNK + SUBLANES, xbc_w), F32),
                        pltpu.VMEM((SSD_CHUNK, xbc_w), F32),
                        pltpu.VMEM((SSD_CHUNK, width), F32),
                        pltpu.VMEM((SSD_HEADS, SSD_HEAD_DIM, SSD_STATE), F32)],
        compiler_params=_params(("parallel", "arbitrary")),
        name="ssd_mixer",
    )(ssd_in, ssd_in, ssd_in, conv_w.astype(F32), conv_b.astype(F32).reshape(1, xbc_w), dtb, alog, dsk,
      norm_g.astype(F32).reshape(1, width))


def _layer_norm_rows(x, g, b):
    mu = jnp.mean(x, axis=-1, keepdims=True)
    var = jnp.mean(jnp.square(x - mu), axis=-1, keepdims=True)
    return (x - mu) * lax.rsqrt(var + LN_EPS) * g + b


def _ln_router_kernel(x_ref, mix_ref, g_ref, b_ref, wr_ref, br_ref, o_ref, xp_ref, r_ref, *, alpha):
    x1 = _layer_norm_rows(alpha * x_ref[...] + mix_ref[...], g_ref[...], b_ref[...])
    o_ref[...] = x1
    half = xp_ref.shape[1]
    bits = pltpu.bitcast(x1.astype(BF16).astype(F32), jnp.uint32)
    xp_ref[...] = (bits[:, :half] >> 16) | bits[:, half:]
    logits = jnp.dot(x1, wr_ref[...], preferred_element_type=F32, precision=lax.Precision.HIGHEST)
    logits = logits + br_ref[...]
    rows = logits.shape[0]
    lane = lax.broadcasted_iota(I32, (rows, LANES), 1).astype(F32)
    ng, epg = float(N_EXPERT_GROUPS), float(EXPERTS_PER_GROUP)
    far = float(LANES)

    gmask = lane < ng
    gl = jnp.where(gmask, logits, -jnp.inf)
    ge = jnp.exp(gl - jnp.max(gl, axis=1, keepdims=True))
    gprob = ge / jnp.sum(ge, axis=1, keepdims=True)
    gprob = jnp.where(gmask, gprob, -1.0)
    gw = jnp.max(gprob, axis=1, keepdims=True)
    gsel = jnp.min(jnp.where(gprob == gw, lane, far), axis=1, keepdims=True)

    e_lo = ng + gsel * epg
    emask = (lane >= e_lo) & (lane < e_lo + epg)
    el = jnp.where(emask, logits, -jnp.inf)
    v0 = jnp.max(el, axis=1, keepdims=True)
    i0 = jnp.min(jnp.where(emask & (el == v0), lane, far), axis=1, keepdims=True)
    emask1 = emask & (lane != i0)
    el1 = jnp.where(emask1, logits, -jnp.inf)
    v1 = jnp.max(el1, axis=1, keepdims=True)
    i1 = jnp.min(jnp.where(emask1 & (el1 == v1), lane, far), axis=1, keepdims=True)
    e = jnp.exp(v1 - v0)
    p0 = 1.0 / (1.0 + e)
    p1 = e / (1.0 + e)
    out = jnp.where(lane == 0.0, i0 - ng,
                    jnp.where(lane == 1.0, i1 - ng,
                              jnp.where(lane == 2.0, gw * p0, jnp.where(lane == 3.0, gw * p1, 0.0))))
    r_ref[...] = out


def _ln_router(x, mixed, g, b, w_rg, b_rg, w_re, b_re, alpha, tr=256):
    t, d = x.shape
    ncol = N_EXPERT_GROUPS + N_EXPERTS
    wr = jnp.pad(jnp.concatenate([w_rg, w_re], axis=1).astype(F32), ((0, 0), (0, LANES - ncol)))
    br = jnp.pad(jnp.concatenate([b_rg, b_re]).astype(F32), (0, LANES - ncol)).reshape(1, LANES)
    const = lambda i: (0, 0)
    return pl.pallas_call(
        functools.partial(_ln_router_kernel, alpha=alpha),
        grid=(t // tr,),
        in_specs=[pl.BlockSpec((tr, d), lambda i: (i, 0)),
                  pl.BlockSpec((tr, d), lambda i: (i, 0)),
                  pl.BlockSpec((1, d), const), pl.BlockSpec((1, d), const),
                  pl.BlockSpec((d, LANES), const), pl.BlockSpec((1, LANES), const)],
        out_specs=[pl.BlockSpec((tr, d), lambda i: (i, 0)),
                   pl.BlockSpec((tr, d // 2), lambda i: (i, 0)),
                   pl.BlockSpec((tr, LANES), lambda i: (i, 0))],
        out_shape=[jax.ShapeDtypeStruct((t, d), F32), jax.ShapeDtypeStruct((t, d // 2), jnp.uint32),
                   jax.ShapeDtypeStruct((t, LANES), F32)],
        compiler_params=_params(("parallel",)),
        name="ln1_router",
    )(x, mixed, g.astype(F32).reshape(1, d), b.astype(F32).reshape(1, d), wr, br)


def _row_copy(src_ref, dst_ref, sem, tok, row):
    return pltpu.make_async_copy(src_ref.at[pl.ds(tok, 1)], dst_ref.at[pl.ds(row, 1)], sem)


def _moe_kernel(tok_ref, be_ref, bv_ref, nu_ref, x_ref, wg_ref, wu_ref, wd_ref, o_ref,
                xq_ref, xbf_ref, act_ref, gbuf_ref, dbuf_ref, sem, gsem, dsem, *, nf, nn):
    s = pl.program_id(0)
    p = pl.program_id(1)
    n_blk = pl.num_programs(0)
    n_used = nu_ref[0]
    steps = nf + nn
    sub = MOE_SUB
    n_sub = MOE_ROWS // sub
    half = xq_ref.shape[1]
    ks = MOE_K_SPLIT
    kq = wg_ref.shape[1] // ks

    def weights_dma(blk, ph, start):
        e = be_ref[blk]

        @pl.when(ph < nf)
        def _():
            slot = lax.rem(blk * nf + ph, MOE_RING)
            col = pl.multiple_of(ph * MOE_F_CHUNK, MOE_F_CHUNK)
            for q in range(ks):
                rows = pl.ds(q * kq, kq)
                for j, w_ref in enumerate((wg_ref, wu_ref)):
                    cp = pltpu.make_async_copy(w_ref.at[e, rows, pl.ds(col, MOE_F_CHUNK)],
                                               gbuf_ref.at[slot, j, rows], gsem.at[slot])
                    cp.start() if start else cp.wait()

        @pl.when(ph >= nf)
        def _():
            k = ph - nf
            slot = lax.rem(blk * nn + k, MOE_RING)
            for j in range(2):
                col = pl.multiple_of((k + j * nn) * MOE_N_CHUNK, MOE_N_CHUNK)
                cp = pltpu.make_async_copy(wd_ref.at[e, :, pl.ds(col, MOE_N_CHUNK)],
                                           dbuf_ref.at[slot, j], dsem.at[slot])
                cp.start() if start else cp.wait()

    @pl.when((s == 0) & (p == 0))
    def _():
        for a in range(MOE_LOOKAHEAD):
            weights_dma(s, p + a, True)

    ahead = p + MOE_LOOKAHEAD
    blk_a = jnp.where(ahead >= steps, s + 1, s)
    ph_a = jnp.where(ahead >= steps, ahead - steps, ahead)

    @pl.when(blk_a < n_used)
    def _():
        weights_dma(blk_a, ph_a, True)

    @pl.when(s < n_used)
    def _():
        weights_dma(s, p, False)

    def nsub_of(blk):
        return (bv_ref[blk] + (sub - 1)) // sub

    def issue(blk, r):
        base = blk * MOE_ROWS + r * sub

        def body(i, carry):
            for u in range(DMA_UNROLL):
                j = i * DMA_UNROLL + u
                _row_copy(x_ref, xq_ref, sem, tok_ref[base + j], r * sub + j).start()
            return carry
        lax.fori_loop(0, sub // DMA_UNROLL, body, 0)

    nsub = nsub_of(s)

    @pl.when((s == 0) & (p == 0))
    def _():
        for r in range(n_sub):
            @pl.when(r < nsub)
            def _(r=r):
                issue(s, r)

    @pl.when(p == 0)
    def _():
        for r in range(n_sub):
            @pl.when(r < nsub)
            def _(r=r):
                pltpu.make_async_copy(x_ref.at[pl.ds(0, sub)], xq_ref.at[pl.ds(r * sub, sub)], sem).wait()
        for r in range(n_sub):
            @pl.when(r < nsub)
            def _(r=r):
                rs = slice(r * sub, (r + 1) * sub)
                word = xq_ref[rs, :]
                xbf_ref[rs, 0:half] = pltpu.bitcast(word << 16, F32).astype(BF16)
                xbf_ref[rs, half:2 * half] = pltpu.bitcast(word & jnp.uint32(0xFFFF0000), F32).astype(BF16)

    nxt = jnp.minimum(s + 1, n_blk - 1)
    for r in range(n_sub):
        @pl.when((p == r + 1) & (s + 1 < n_blk) & (r < nsub_of(nxt)))
        def _(r=r):
            issue(nxt, r)

    for k in range(1, n_sub + 1):
        m = k * sub

        @pl.when((p < nf) & (nsub == k))
        def _(m=m):
            slot = lax.rem(s * nf + p, MOE_RING)
            gate = up = None
            for q in range(ks):
                xr = xbf_ref[0:m, q * kq:(q + 1) * kq]
                gq = jnp.dot(xr, gbuf_ref[slot, 0, q * kq:(q + 1) * kq, :].astype(BF16), preferred_element_type=F32)
                uq = jnp.dot(xr, gbuf_ref[slot, 1, q * kq:(q + 1) * kq, :].astype(BF16), preferred_element_type=F32)
                gate = gq if gate is None else gate + gq
                up = uq if up is None else up + uq
            act_ref[p, 0:m, :] = (_silu(gate) * up).astype(BF16)

    def down(slot, j, m):
        acc = None
        for q in range(nf):
            wq = dbuf_ref[slot, j, q * MOE_F_CHUNK:(q + 1) * MOE_F_CHUNK, :].astype(BF16)
            t = jnp.dot(act_ref[q, 0:m, :], wq, preferred_element_type=F32)
            acc = t if acc is None else acc + t
        return pltpu.bitcast(acc.astype(BF16).astype(F32), jnp.uint32)

    for k in range(0, n_sub + 1):
        m = k * sub

        @pl.when((p >= nf) & (nsub == k))
        def _(m=m):
            if m > 0:
                slot = lax.rem(s * nn + p - nf, MOE_RING)
                o_ref[0:m, :] = (down(slot, 0, m) >> 16) | down(slot, 1, m)
            if m < MOE_ROWS:
                o_ref[m:MOE_ROWS, :] = jnp.zeros((MOE_ROWS - m, o_ref.shape[1]), jnp.uint32)


def _moe_ffn(xp, row_tok, blk_e, blk_valid, n_used, w_gate, w_up, w_down):
    n_rows = row_tok.shape[0]
    n_e, d, f = w_gate.shape
    nf = f // MOE_F_CHUNK
    nn = (d // 2) // MOE_N_CHUNK
    n_blk = n_rows // MOE_ROWS
    assert nf + nn > MOE_ROWS // MOE_SUB and d % MOE_K_SPLIT == 0
    assert MOE_LOOKAHEAD < MOE_RING and MOE_LOOKAHEAD <= min(nf, nn)

    def out_map(s, p, tok, be, bv, nu):
        return (s, jnp.maximum(p - nf, 0))

    hbm = pl.BlockSpec(memory_space=pl.ANY)
    return pl.pallas_call(
        functools.partial(_moe_kernel, nf=nf, nn=nn),
        grid_spec=pltpu.PrefetchScalarGridSpec(
            num_scalar_prefetch=4,
            grid=(n_blk, nf + nn),
            in_specs=[hbm, hbm, hbm, hbm],
            out_specs=pl.BlockSpec((MOE_ROWS, MOE_N_CHUNK), out_map),
            scratch_shapes=[pltpu.VMEM((MOE_ROWS, d // 2), jnp.uint32),
                            pltpu.VMEM((MOE_ROWS, d), BF16),
                            pltpu.VMEM((nf, MOE_ROWS, MOE_F_CHUNK), BF16),
                            pltpu.VMEM((MOE_RING, 2, d, MOE_F_CHUNK), F32),
                            pltpu.VMEM((MOE_RING, 2, f, MOE_N_CHUNK), F32),
                            pltpu.SemaphoreType.DMA(()),
                            pltpu.SemaphoreType.DMA((MOE_RING,)),
                            pltpu.SemaphoreType.DMA((MOE_RING,))]),
        out_shape=jax.ShapeDtypeStruct((n_rows, d // 2), jnp.uint32),
        compiler_params=_params(("arbitrary", "arbitrary")),
        name="moe_ffn",
    )(row_tok, blk_e, blk_valid, n_used, xp, w_gate, w_up, w_down)


def _combine_kernel(dest_ref, y_ref, x_ref, r_ref, g_ref, b_ref, o_ref, buf_ref, sem, *, alpha, tt):
    s = pl.program_id(0)
    n = pl.num_programs(0)

    rows = 2 * tt

    def start(blk, slot):
        base = blk * rows

        def body(i, carry):
            for u in range(DMA_UNROLL):
                r = i * DMA_UNROLL + u
                pltpu.make_async_copy(y_ref.at[pl.ds(dest_ref[base + r], 1)],
                                      buf_ref.at[slot, pl.ds(r, 1)], sem.at[slot]).start()
            return carry
        lax.fori_loop(0, rows // DMA_UNROLL, body, 0)

    @pl.when(s == 0)
    def _():
        start(s, 0)

    for par in range(2):
        @pl.when((s % 2 == par) & (s + 1 < n))
        def _(par=par):
            start(s + 1, 1 - par)

    slot = s % 2
    pltpu.make_async_copy(y_ref.at[pl.ds(0, rows)], buf_ref.at[slot], sem.at[slot]).wait()
    g0 = r_ref[:, 2:3]
    g1 = r_ref[:, 3:4]
    w0 = buf_ref[slot, 0:tt, :]
    w1 = buf_ref[slot, tt:2 * tt, :]
    half = w0.shape[1]
    himask = jnp.uint32(0xFFFF0000)
    y_lo = g0 * pltpu.bitcast(w0 << 16, F32) + g1 * pltpu.bitcast(w1 << 16, F32)
    y_hi = g0 * pltpu.bitcast(w0 & himask, F32) + g1 * pltpu.bitcast(w1 & himask, F32)
    r_lo = alpha * x_ref[:, 0:half] + y_lo
    r_hi = alpha * x_ref[:, half:2 * half] + y_hi
    inv_d = 1.0 / (2 * half)
    mu = (jnp.sum(r_lo, axis=-1, keepdims=True) + jnp.sum(r_hi, axis=-1, keepdims=True)) * inv_d
    var = (jnp.sum(jnp.square(r_lo - mu), axis=-1, keepdims=True)
           + jnp.sum(jnp.square(r_hi - mu), axis=-1, keepdims=True)) * inv_d
    rstd = lax.rsqrt(var + LN_EPS)
    o_ref[:, 0:half] = (r_lo - mu) * rstd * g_ref[:, 0:half] + b_ref[:, 0:half]
    o_ref[:, half:2 * half] = (r_hi - mu) * rstd * g_ref[:, half:2 * half] + b_ref[:, half:2 * half]


def _combine_ln(y_rows, dest, x1, route, g, b, alpha, tt=128):
    t, d = x1.shape
    const = lambda i, dr: (0, 0)
    return pl.pallas_call(
        functools.partial(_combine_kernel, alpha=alpha, tt=tt),
        grid_spec=pltpu.PrefetchScalarGridSpec(
            num_scalar_prefetch=1,
            grid=(t // tt,),
            in_specs=[pl.BlockSpec(memory_space=pl.ANY),
                      pl.BlockSpec((tt, d), lambda i, dr: (i, 0)),
                      pl.BlockSpec((tt, LANES), lambda i, dr: (i, 0)),
                      pl.BlockSpec((1, d), const), pl.BlockSpec((1, d), const)],
            out_specs=pl.BlockSpec((tt, d), lambda i, dr: (i, 0)),
            scratch_shapes=[pltpu.VMEM((2, 2 * tt, d // 2), jnp.uint32),
                            pltpu.SemaphoreType.DMA((2,))]),
        out_shape=jax.ShapeDtypeStruct((t, d), F32),
        compiler_params=_params(("arbitrary",)),
        name="moe_combine_ln2",
    )(dest, y_rows, x1, route, g.astype(F32).reshape(1, d), b.astype(F32).reshape(1, d))


def _plan_kernel(r_ref, dest_ref, meta_ref, cnt_ref, pstart_ref, carry_ref):
    ph = pl.program_id(0)
    i = pl.program_id(1)
    ts = r_ref.shape[0]
    rows = float(MOE_ROWS)
    lane = lax.broadcasted_iota(I32, (ts, LANES), 1).astype(F32)
    oh0 = jnp.where(lane == r_ref[:, 0:1], 1.0, 0.0)
    oh1 = jnp.where(lane == r_ref[:, 1:2], 1.0, 0.0)
    oh = oh0 + oh1

    @pl.when((ph == 0) & (i == 0))
    def _():
        cnt_ref[...] = jnp.zeros(cnt_ref.shape, F32)

    @pl.when(ph == 0)
    def _():
        cnt_ref[...] += jnp.sum(oh, axis=0, keepdims=True)

    @pl.when((ph == 0) & (i == pl.num_programs(1) - 1))
    def _():
        sq = (LANES, LANES)
        r_i = lax.broadcasted_iota(I32, sq, 0)
        c_i = lax.broadcasted_iota(I32, sq, 1)
        counts = jnp.broadcast_to(cnt_ref[...], sq)
        nblk_e = jnp.floor((counts + (rows - 0.5)) * (1.0 / rows))
        upper = jnp.where(r_i <= c_i, 1.0, 0.0)
        pend = jnp.dot(nblk_e.astype(BF16), upper.astype(BF16), preferred_element_type=F32)
        pstart = pend - nblk_e
        n_used = jnp.max(pend, axis=1, keepdims=True)
        pstart_ref[...] = pstart[0:1, :] * rows
        carry_ref[...] = jnp.zeros(carry_ref.shape, F32)
        b_eff = jnp.minimum(c_i.astype(F32), n_used - 1.0)
        pend_t, pstart_t, counts_t = pend.T, pstart.T, counts.T
        blk_e = jnp.sum(jnp.where(pend_t <= b_eff, 1.0, 0.0), axis=0, keepdims=True)
        blk_e = jnp.minimum(blk_e, float(N_EXPERTS - 1))
        pick = r_i.astype(F32) == blk_e
        cnt_b = jnp.sum(jnp.where(pick, counts_t, 0.0), axis=0, keepdims=True)
        first_b = jnp.sum(jnp.where(pick, pstart_t, 0.0), axis=0, keepdims=True)
        valid = jnp.clip(cnt_b - (b_eff[0:1, :] - first_b) * rows, 0.0, rows)
        valid = jnp.where(c_i[0:1, :].astype(F32) < n_used[0:1, :], valid, 0.0)
        sub_i = lax.broadcasted_iota(I32, meta_ref.shape, 0)
        meta = jnp.where(sub_i == 0, blk_e, jnp.where(sub_i == 1, valid, jnp.where(sub_i == 2, n_used[0:1, :], 0.0)))
        meta_ref[...] = meta.astype(I32)

    @pl.when(ph == 1)
    def _():
        below = lax.broadcasted_iota(I32, (ts, ts), 0) > lax.broadcasted_iota(I32, (ts, ts), 1)
        earlier = jnp.dot(below.astype(BF16), oh.astype(BF16), preferred_element_type=F32)
        base = pstart_ref[...] + carry_ref[...] + earlier
        d0 = jnp.sum(base * oh0, axis=1, keepdims=True)
        d1 = jnp.sum(base * oh1, axis=1, keepdims=True)
        carry_ref[...] += jnp.sum(oh, axis=0, keepdims=True)
        both = jnp.where(lane == 0.0, d0, jnp.where(lane == 1.0, d1, 0.0))
        dest_ref[...] = both.T[0:2, :].astype(I32)


def _dispatch_plan(route, tt):
    t = route.shape[0]
    m = 2 * t
    n_blk = -(-m // MOE_ROWS) + N_EXPERTS
    n_rows = n_blk * MOE_ROWS
    ts = min(PLAN_TILE, t)
    assert n_blk <= LANES and t % ts == 0 and N_EXPERTS <= LANES
    dest2, meta = pl.pallas_call(
        _plan_kernel,
        grid=(2, t // ts),
        in_specs=[pl.BlockSpec((ts, LANES), lambda ph, i: (i, 0))],
        out_specs=[pl.BlockSpec((2, ts), lambda ph, i: (0, i * ph)),
                   pl.BlockSpec((SUBLANES, LANES), lambda ph, i: (0, 0))],
        out_shape=[jax.ShapeDtypeStruct((2, t), I32), jax.ShapeDtypeStruct((SUBLANES, LANES), I32)],
        scratch_shapes=[pltpu.VMEM((1, LANES), F32), pltpu.VMEM((1, LANES), F32), pltpu.VMEM((1, LANES), F32)],
        compiler_params=_params(("arbitrary", "arbitrary")),
        name="moe_plan",
    )(route)
    tok = jnp.arange(t, dtype=I32)
    row_tok = jnp.zeros((n_rows,), I32).at[dest2.reshape(m)].set(jnp.concatenate([tok, tok]))
    dest_tiles = dest2.reshape(2, t // tt, tt).transpose(1, 0, 2).reshape(m)
    return row_tok, meta[0, :n_blk], meta[1, :n_blk], meta[2, 0:1], dest_tiles


def kernel(x, w_in, idx_kn_g, idx_kn_b, conv_w, conv_b, dt_bias, a_log, d_skip, ssd_norm_g, w_out,
           ln1_g, ln1_b, w_rg, b_rg, w_re, b_re, w_gate, w_up, w_down, ln2_g, ln2_b):
    bsz, seq, d = x.shape
    depth = w_in.shape[0]
    alpha = (2 * depth) ** 0.25
    att_w = ATT_HEADS * HEAD_DIM
    kv_w = KV_HEADS * HEAD_DIM
    qi_w = IDX_HEADS * IDX_DIM
    ssd_w = SSD_HEADS * SSD_HEAD_DIM
    xbc_w = ssd_w + 2 * SSD_GROUPS * SSD_STATE
    sizes = (att_w, kv_w, kv_w, qi_w, IDX_DIM, IDX_HEADS, ssd_w, xbc_w, SSD_HEADS)
    offs = [0]
    for sz in sizes:
        offs.append(offs[-1] + sz)
    tt = 128
    xf = x.reshape(bsz * seq, d)
    for l in range(depth):
        w = w_in[l].astype(BF16)
        zpad = lambda n: jnp.zeros((d, n), BF16)
        w_qkv = w[:, offs[0]:offs[3]]
        w_idx = jnp.concatenate([w[:, offs[3]:offs[6]], zpad(LANES - IDX_DIM - IDX_HEADS)], axis=1)
        w_ssd = jnp.concatenate([w[:, offs[7]:offs[8]], w[:, offs[6]:offs[7]], w[:, offs[8]:offs[9]],
                                 zpad(2 * LANES - SSD_HEADS)], axis=1)
        qkv, xbf = _matmul(xf, w_qkv, BF16, 512, 512)
        idx = _matmul(xbf, w_idx, F32, 1024, 384)
        ssd_in = _matmul(xbf, w_ssd, F32, 1024, 256)
        att = _dsa_attention(qkv, idx, idx_kn_g[l], idx_kn_b[l], bsz, seq)
        ssd = _ssd_mixer(ssd_in, conv_w[l], conv_b[l], dt_bias[l], a_log[l], d_skip[l], ssd_norm_g[l], bsz, seq)
        mixed = _matmul_pair(att, ssd, w_out[l].astype(BF16), F32, 1024, 512)
        x1, x1p, route = _ln_router(xf, mixed, ln1_g[l], ln1_b[l], w_rg[l], b_rg[l], w_re[l], b_re[l], alpha)
        row_tok, blk_e, blk_valid, n_used, dest_tiles = _dispatch_plan(route, tt)
        y_rows = _moe_ffn(x1p, row_tok, blk_e, blk_valid, n_used, w_gate[l], w_up[l], w_down[l])
        xf = _combine_ln(y_rows, dest_tiles, x1, route, ln2_g[l], ln2_b[l], alpha, tt)
    return xf.reshape(bsz, seq, d)
```

```python
import functools

import jax
import jax.numpy as jnp
from jax import lax
from jax.experimental import pallas as pl
from jax.experimental.pallas import tpu as pltpu

F32 = jnp.float32
BF16 = jnp.bfloat16
I32 = jnp.int32

HEAD_DIM = 128
KV_HEADS = 4
GQA_GROUP = 4
ATT_HEADS = KV_HEADS * GQA_GROUP
IDX_HEADS = 16
IDX_DIM = 64
DSA_TOPK_MAX = 256
QUERY_BLOCK = 128
SSD_HEAD_DIM = 64
SSD_GROUPS = 8
SSD_HEADS_PER_GROUP = 4
SSD_HEADS = SSD_GROUPS * SSD_HEADS_PER_GROUP
SSD_STATE = 128
SSD_CONV = 4
SSD_CHUNK = 128
N_EXPERT_GROUPS = 8
EXPERTS_PER_GROUP = 8
N_EXPERTS = 64
LN_EPS = 1e-5
RMS_EPS = 1e-5

LANES = 128
SUBLANES = 8
VMEM_LIMIT = 56 * 1024 * 1024

KEY_CHUNK = 1024
MOE_ROWS = 768
MOE_SUB = 256
MOE_F_CHUNK = 256
MOE_N_CHUNK = 256
MOE_K_SPLIT = 4
MOE_RING = 3
MOE_LOOKAHEAD = 2
DMA_UNROLL = 8
PLAN_TILE = 512
NEG_BIG = -1e30
INT_MIN = -2 ** 31
NEG_INF_KEY = -2139095041


def _params(sem):
    return pltpu.CompilerParams(dimension_semantics=sem, vmem_limit_bytes=VMEM_LIMIT)


def _mm_cast_kernel(a_ref, b_ref, o_ref, abf_ref):
    @pl.when(pl.program_id(1) == 0)
    def _():
        abf_ref[...] = a_ref[...].astype(BF16)

    o_ref[...] = jnp.dot(abf_ref[...], b_ref[...], preferred_element_type=F32).astype(o_ref.dtype)


def _mm_kernel(a_ref, b_ref, o_ref):
    o_ref[...] = jnp.dot(a_ref[...], b_ref[...], preferred_element_type=F32).astype(o_ref.dtype)


def _mm_pair_kernel(a1_ref, a2_ref, b_ref, o_ref):
    k1 = a1_ref.shape[1]
    acc = jnp.dot(a1_ref[...], b_ref[0:k1, :], preferred_element_type=F32)
    acc = acc + jnp.dot(a2_ref[...], b_ref[k1:, :], preferred_element_type=F32)
    o_ref[...] = acc.astype(o_ref.dtype)


def _matmul_pair(a1, a2, b, out_dtype, tm, tn):
    m, k1 = a1.shape
    k2 = a2.shape[1]
    n = b.shape[1]
    tm = min(tm, m)
    assert m % tm == 0 and n % tn == 0 and b.shape[0] == k1 + k2
    return pl.pallas_call(
        _mm_pair_kernel,
        grid=(m // tm, n // tn),
        in_specs=[pl.BlockSpec((tm, k1), lambda i, j: (i, 0)),
                  pl.BlockSpec((tm, k2), lambda i, j: (i, 0)),
                  pl.BlockSpec((k1 + k2, tn), lambda i, j: (0, j))],
        out_specs=pl.BlockSpec((tm, tn), lambda i, j: (i, j)),
        out_shape=jax.ShapeDtypeStruct((m, n), out_dtype),
        compiler_params=_params(("parallel", "arbitrary")),
        name="matmul_pair",
    )(a1, a2, b)


def _matmul(a, b, out_dtype, tm, tn):
    m, k = a.shape
    n = b.shape[1]
    tm = min(tm, m)
    assert m % tm == 0 and n % tn == 0
    cast = a.dtype != BF16
    out_specs = pl.BlockSpec((tm, tn), lambda i, j: (i, j))
    out_shape = jax.ShapeDtypeStruct((m, n), out_dtype)
    if cast:
        out_specs = [out_specs, pl.BlockSpec((tm, k), lambda i, j: (i, 0))]
        out_shape = [out_shape, jax.ShapeDtypeStruct((m, k), BF16)]
    return pl.pallas_call(
        _mm_cast_kernel if cast else _mm_kernel,
        grid=(m // tm, n // tn),
        in_specs=[pl.BlockSpec((tm, k), lambda i, j: (i, 0)),
                  pl.BlockSpec((k, tn), lambda i, j: (0, j))],
        out_specs=out_specs,
        out_shape=out_shape,
        compiler_params=_params(("parallel", "arbitrary")),
        name="matmul_cast" if cast else "matmul",
    )(a, b)


def _attn_kernel(q_ref, k_ref, v_ref, qi_ref, kw_ref, g_ref, b_ref, o_ref,
                 kln_ref, key_ref, bias_ref, s_ref, mrun_ref, lrun_ref, acc_ref, *, top_k):
    i = pl.program_id(1)
    tq = QUERY_BLOCK
    ck = KEY_CHUNK

    @pl.when(i == 0)
    def _():
        kx = kw_ref[:, 0:IDX_DIM]
        mu = jnp.mean(kx, axis=-1, keepdims=True)
        var = jnp.mean(jnp.square(kx - mu), axis=-1, keepdims=True)
        y = (kx - mu) * lax.rsqrt(var + LN_EPS)
        kln_ref[...] = (y * g_ref[...] + b_ref[...]).astype(BF16)

    q_start = i * tq
    n_chunks = (q_start + tq + ck - 1) // ck
    q_pos = q_start + lax.broadcasted_iota(I32, (tq, 1), 0)
    w = kw_ref[pl.ds(pl.multiple_of(q_start, tq), tq), IDX_DIM:IDX_DIM + IDX_HEADS]
    w = w * (IDX_HEADS ** -0.5 * IDX_DIM ** -0.5)
    qi = jnp.concatenate([qi_ref[:, h * IDX_DIM:(h + 1) * IDX_DIM] for h in range(IDX_HEADS)], axis=0).astype(BF16)

    def chunk_off(c):
        return pl.multiple_of(c * ck, ck)

    def key_pos(c):
        return c * ck + lax.broadcasted_iota(I32, (1, ck), 1)

    def score_chunk(c, carry):
        off = chunk_off(c)
        kc = kln_ref[pl.ds(off, ck), :]
        d = lax.dot_general(qi, kc, (((1,), (1,)), ((), ())), preferred_element_type=F32)
        acc = jnp.zeros((tq, ck), F32)
        for h in range(IDX_HEADS):
            acc = acc + jnp.maximum(d[h * tq:(h + 1) * tq, :], 0.0) * w[:, h:h + 1]
        acc = jnp.where(key_pos(c) <= q_pos, acc, -jnp.inf)
        bits = pltpu.bitcast(acc, I32)
        key_ref[:, pl.ds(off, ck)] = bits ^ ((bits >> 31) & 0x7FFFFFFF)
        return carry

    lax.fori_loop(0, n_chunks, score_chunk, 0)

    def bit_body(b, carry):
        cand, cnt_cand = carry
        trial = cand | lax.shift_left(jnp.int32(1), jnp.int32(31) - jnp.asarray(b, I32))
        trial_b = jnp.broadcast_to(trial ^ INT_MIN, (tq, LANES))

        def cnt_chunk(c, cnt):
            kc = key_ref[:, pl.ds(chunk_off(c), ck)]
            for s in range(ck // LANES):
                cnt = cnt + jnp.where(kc[:, s * LANES:(s + 1) * LANES] >= trial_b, 1.0, 0.0)
            return cnt

        cnt = lax.fori_loop(0, n_chunks, cnt_chunk, jnp.zeros((tq, LANES), F32))
        total = jnp.sum(cnt, axis=1, keepdims=True)
        ok = total >= float(top_k)
        return jnp.where(ok, trial, cand), jnp.where(ok, total, cnt_cand)

    n_keys = (jnp.zeros((tq, 1), I32) + n_chunks * ck).astype(F32)
    cand, cnt_ge = lax.fori_loop(0, 32, bit_body, (jnp.zeros((tq, 1), I32), n_keys))
    thr = cand ^ INT_MIN

    def bias_chunk(c, carry):
        off = chunk_off(c)
        sel = (key_ref[:, pl.ds(off, ck)] >= thr) & (key_pos(c) <= q_pos)
        bias_ref[:, pl.ds(off, ck)] = jnp.where(sel, 0.0, NEG_BIG)
        return carry

    lax.fori_loop(0, n_chunks, bias_chunk, 0)

    tie = (cnt_ge > float(top_k)) & (thr > NEG_INF_KEY)

    @pl.when(jnp.max(jnp.where(tie, 1.0, 0.0)) > 0.5)
    def _():
        tri = (lax.broadcasted_iota(I32, (ck, ck), 0) <= lax.broadcasted_iota(I32, (ck, ck), 1)).astype(BF16)

        def gt_chunk(c, cnt):
            kc = key_ref[:, pl.ds(chunk_off(c), ck)]
            return cnt + jnp.sum(jnp.where(kc > thr, 1.0, 0.0), axis=1, keepdims=True)

        need = float(top_k) - lax.fori_loop(0, n_chunks, gt_chunk, jnp.zeros((tq, 1), F32))

        def tie_chunk(c, seen):
            off = chunk_off(c)
            kc = key_ref[:, pl.ds(off, ck)]
            eq = jnp.where(kc == thr, 1.0, 0.0)
            rank = seen + jnp.dot(eq.astype(BF16), tri, preferred_element_type=F32)
            keep = (kc > thr) | ((kc == thr) & ((rank <= need) | jnp.logical_not(tie)))
            sel = keep & (key_pos(c) <= q_pos)
            bias_ref[:, pl.ds(off, ck)] = jnp.where(sel, 0.0, NEG_BIG)
            return seen + jnp.sum(eq, axis=1, keepdims=True)

        lax.fori_loop(0, n_chunks, tie_chunk, jnp.zeros((tq, 1), F32))

    scale = HEAD_DIM ** -0.5
    gq = GQA_GROUP
    for g in range(KV_HEADS):
        qg = jnp.concatenate([q_ref[:, (g * gq + j) * HEAD_DIM:(g * gq + j + 1) * HEAD_DIM] for j in range(gq)],
                             axis=0)

        mrun_ref[...] = jnp.full(mrun_ref.shape, NEG_BIG, F32)

        def logit_chunk(c, carry, g=g, qg=qg):
            off = chunk_off(c)
            kc = k_ref[pl.ds(off, ck), g * HEAD_DIM:(g + 1) * HEAD_DIM]
            s = lax.dot_general(qg, kc, (((1,), (1,)), ((), ())), preferred_element_type=F32)
            bias = bias_ref[:, pl.ds(off, ck)]
            s = s * scale + jnp.concatenate([bias] * gq, axis=0)
            s_ref[:, pl.ds(off, ck)] = s
            m = mrun_ref[...]
            for t in range(ck // LANES):
                m = jnp.maximum(m, s[:, t * LANES:(t + 1) * LANES])
            mrun_ref[...] = m
            return carry

        lax.fori_loop(0, n_chunks, logit_chunk, 0)
        m = jnp.max(mrun_ref[...], axis=1, keepdims=True)

        lrun_ref[...] = jnp.zeros(lrun_ref.shape, F32)
        acc_ref[...] = jnp.zeros(acc_ref.shape, F32)

        def prob_chunk(c, carry, g=g, m=m):
            off = chunk_off(c)
            vc = v_ref[pl.ds(off, ck), g * HEAD_DIM:(g + 1) * HEAD_DIM]
            p = jnp.exp(s_ref[:, pl.ds(off, ck)] - m)
            l = lrun_ref[...]
            for t in range(ck // LANES):
                l = l + p[:, t * LANES:(t + 1) * LANES]
            lrun_ref[...] = l
            acc_ref[...] += jnp.dot(p.astype(BF16), vc, preferred_element_type=F32)
            return carry

        lax.fori_loop(0, n_chunks, prob_chunk, 0)
        out = acc_ref[...] / jnp.sum(lrun_ref[...], axis=1, keepdims=True)
        for j in range(gq):
            h = g * gq + j
            o_ref[:, h * HEAD_DIM:(h + 1) * HEAD_DIM] = out[j * tq:(j + 1) * tq, :].astype(o_ref.dtype)


def _dsa_attention(qkv, idx, qi_off, kw_off, kn_g, kn_b, bsz, seq):
    top_k = min(DSA_TOPK_MAX, seq // 4)
    nq = seq // QUERY_BLOCK
    att_w = ATT_HEADS * HEAD_DIM
    kv_w = KV_HEADS * HEAD_DIM
    qi_w = IDX_HEADS * IDX_DIM
    assert seq % KEY_CHUNK == 0 and att_w % kv_w == 0 and qi_off % qi_w == 0 and kw_off % LANES == 0
    return pl.pallas_call(
        functools.partial(_attn_kernel, top_k=top_k),
        grid=(bsz, nq),
        in_specs=[pl.BlockSpec((QUERY_BLOCK, att_w), lambda b, i: (b * nq + i, 0)),
                  pl.BlockSpec((seq, kv_w), lambda b, i: (b, att_w // kv_w)),
                  pl.BlockSpec((seq, kv_w), lambda b, i: (b, att_w // kv_w + 1)),
                  pl.BlockSpec((QUERY_BLOCK, qi_w), lambda b, i: (b * nq + i, qi_off // qi_w)),
                  pl.BlockSpec((seq, LANES), lambda b, i: (b, kw_off // LANES)),
                  pl.BlockSpec((1, IDX_DIM), lambda b, i: (0, 0)),
                  pl.BlockSpec((1, IDX_DIM), lambda b, i: (0, 0))],
        out_specs=pl.BlockSpec((QUERY_BLOCK, att_w), lambda b, i: (b * nq + i, 0)),
        out_shape=jax.ShapeDtypeStruct((bsz * seq, att_w), BF16),
        scratch_shapes=[pltpu.VMEM((seq, IDX_DIM), BF16),
                        pltpu.VMEM((QUERY_BLOCK, seq), I32),
                        pltpu.VMEM((QUERY_BLOCK, seq), F32),
                        pltpu.VMEM((GQA_GROUP * QUERY_BLOCK, seq), F32),
                        pltpu.VMEM((GQA_GROUP * QUERY_BLOCK, LANES), F32),
                        pltpu.VMEM((GQA_GROUP * QUERY_BLOCK, LANES), F32),
                        pltpu.VMEM((GQA_GROUP * QUERY_BLOCK, HEAD_DIM), F32)],
        compiler_params=_params(("parallel", "arbitrary")),
        name="dsa_attention",
    )(qkv, qkv, qkv, idx, idx, kn_g.reshape(1, IDX_DIM), kn_b.reshape(1, IDX_DIM))


def _silu(x):
    return x / (1.0 + jnp.exp(-x))


def _ssd_kernel(xbc_ref, z_ref, dt_ref, cw_ref, cb_ref, dtb_ref, alog_ref, dsk_ref, ng_ref, o_ref,
                xpad_ref, act_ref, y_ref, h_ref):
    c = pl.program_id(1)
    cq = SSD_CHUNK
    width = SSD_HEADS * SSD_HEAD_DIM
    b_off = width
    c_off = width + SSD_GROUPS * SSD_STATE

    @pl.when(c == 0)
    def _():
        xpad_ref[0:SUBLANES, :] = jnp.zeros((SUBLANES, xpad_ref.shape[1]), F32)
        h_ref[...] = jnp.zeros(h_ref.shape, F32)

    xpad_ref[SUBLANES:SUBLANES + cq, :] = xbc_ref[...]
    col = 512
    for j in range(xpad_ref.shape[1] // col):
        cs = slice(j * col, (j + 1) * col)
        acc = cb_ref[:, cs] + jnp.zeros((cq, col), F32)
        for t in range(SSD_CONV):
            r0 = SUBLANES - (SSD_CONV - 1) + t
            acc = acc + xpad_ref[r0:r0 + cq, cs] * cw_ref[t:t + 1, cs]
        act_ref[:, cs] = _silu(acc)
    xpad_ref[0:SUBLANES, :] = xpad_ref[cq:cq + SUBLANES, :]

    xdt_in = dt_ref[:, 0:LANES] + dtb_ref[...]
    dt = jnp.maximum(xdt_in, 0.0) + jnp.log1p(jnp.exp(-jnp.abs(xdt_in)))
    da = dt * (-jnp.exp(alog_ref[...]))
    row = lax.broadcasted_iota(I32, (cq, cq), 0)
    coli = lax.broadcasted_iota(I32, (cq, cq), 1)
    causal = row >= coli
    tril = causal.astype(F32)
    acs = jnp.dot(tril, da, preferred_element_type=F32, precision=lax.Precision.HIGHEST)
    acs_t = acs.T
    dec_in = jnp.exp(acs)
    a_last = acs[cq - 1:cq, :]
    dec_out = jnp.exp(a_last - acs)
    dec_chunk = jnp.exp(a_last)

    for g in range(SSD_GROUPS):
        bg = act_ref[:, b_off + g * SSD_STATE:b_off + (g + 1) * SSD_STATE].astype(BF16)
        cg = act_ref[:, c_off + g * SSD_STATE:c_off + (g + 1) * SSD_STATE].astype(BF16)
        cbm = lax.dot_general(cg, bg, (((1,), (1,)), ((), ())), preferred_element_type=F32)
        for j in range(SSD_HEADS_PER_GROUP):
            hd = g * SSD_HEADS_PER_GROUP + j
            xs = slice(hd * SSD_HEAD_DIM, (hd + 1) * SSD_HEAD_DIM)
            seg = acs[:, hd:hd + 1] - acs_t[hd:hd + 1, :]
            lmat = jnp.exp(jnp.where(causal, seg, -jnp.inf))
            xh = act_ref[:, xs]
            xdt = xh * dt[:, hd:hd + 1]
            y = jnp.dot((cbm * lmat).astype(BF16), xdt.astype(BF16), preferred_element_type=F32)
            hprev = h_ref[hd]
            yoff = lax.dot_general(cg, hprev.astype(BF16), (((1,), (1,)), ((), ())),
                                   preferred_element_type=F32)
            y = y + yoff * dec_in[:, hd:hd + 1] + xh * dsk_ref[:, xs]
            y_ref[:, xs] = y
            st = lax.dot_general((xdt * dec_out[:, hd:hd + 1]).astype(BF16), bg,
                                 (((0,), (0,)), ((), ())), preferred_element_type=F32)
            h_ref[hd] = hprev * dec_chunk[:, hd:hd + 1] + st

    gw = width // SSD_GROUPS
    for g in range(SSD_GROUPS):
        gs = slice(g * gw, (g + 1) * gw)
        gated = y_ref[:, gs] * _silu(z_ref[:, gs])
        ms = jnp.mean(jnp.square(gated), axis=-1, keepdims=True)
        o_ref[:, gs] = (gated * lax.rsqrt(ms + RMS_EPS) * ng_ref[:, gs]).astype(o_ref.dtype)


def _ssd_mixer(ssd_in, z_off, dt_off, dt_w, conv_w, conv_b, dt_bias, a_log, d_skip, norm_g, bsz, seq):
    nc = seq // SSD_CHUNK
    width = SSD_HEADS * SSD_HEAD_DIM
    xbc_w = width + 2 * SSD_GROUPS * SSD_STATE
    assert z_off % width == 0 and dt_off % dt_w == 0 and dt_w >= LANES
    pad = LANES - SSD_HEADS
    dtb = jnp.pad(dt_bias.astype(F32), (0, pad)).reshape(1, LANES)
    alog = jnp.pad(a_log.astype(F32), (0, pad)).reshape(1, LANES)
    dsk = jnp.repeat(d_skip.astype(F32), SSD_HEAD_DIM).reshape(1, width)
    const = lambda b, c: (0, 0)
    return pl.pallas_call(
        _ssd_kernel,
        grid=(bsz, nc),
        in_specs=[pl.BlockSpec((SSD_CHUNK, xbc_w), lambda b, c: (b * nc + c, 0)),
                  pl.BlockSpec((SSD_CHUNK, width), lambda b, c: (b * nc + c, z_off // width)),
                  pl.BlockSpec((SSD_CHUNK, dt_w), lambda b, c: (b * nc + c, dt_off // dt_w)),
                  pl.BlockSpec((SSD_CONV, xbc_w), const),
                  pl.BlockSpec((1, xbc_w), const),
                  pl.BlockSpec((1, LANES), const),
                  pl.BlockSpec((1, LANES), const),
                  pl.BlockSpec((1, width), const),
                  pl.BlockSpec((1, width), const)],
        out_specs=pl.BlockSpec((SSD_CHUNK, width), lambda b, c: (b * nc + c, 0)),
        out_shape=jax.ShapeDtypeStruct((bsz * seq, width), BF16),
        scratch_shapes=[pltpu.VMEM((SSD_CHUNK + SUBLANES, xbc_w), F32),
                        pltpu.VMEM((SSD_CHUNK, xbc_w), F32),
                        pltpu.VMEM((SSD_CHUNK, width), F32),
                        pltpu.VMEM((SSD_HEADS, SSD_HEAD_DIM, SSD_STATE), F32)],
        compiler_params=_params(("parallel", "arbitrary")),
        name="ssd_mixer",
    )(ssd_in, ssd_in, ssd_in, conv_w.astype(F32), conv_b.astype(F32).reshape(1, xbc_w), dtb, alog, dsk,
      norm_g.astype(F32).reshape(1, width))


def _layer_norm_rows(x, g, b):
    mu = jnp.mean(x, axis=-1, keepdims=True)
    var = jnp.mean(jnp.square(x - mu), axis=-1, keepdims=True)
    return (x - mu) * lax.rsqrt(var + LN_EPS) * g + b


def _ln_router_kernel(x_ref, mix_ref, g_ref, b_ref, wr_ref, br_ref, o_ref, xp_ref, r_ref, *, alpha):
    x1 = _layer_norm_rows(alpha * x_ref[...] + mix_ref[...], g_ref[...], b_ref[...])
    o_ref[...] = x1
    half = xp_ref.shape[1]
    bits = pltpu.bitcast(x1.astype(BF16).astype(F32), jnp.uint32)
    xp_ref[...] = (bits[:, :half] >> 16) | bits[:, half:]
    logits = jnp.dot(x1, wr_ref[...], preferred_element_type=F32, precision=lax.Precision.HIGHEST)
    logits = logits + br_ref[...]
    rows = logits.shape[0]
    lane = lax.broadcasted_iota(I32, (rows, LANES), 1).astype(F32)
    ng, epg = float(N_EXPERT_GROUPS), float(EXPERTS_PER_GROUP)
    far = float(LANES)

    gmask = lane < ng
    gl = jnp.where(gmask, logits, -jnp.inf)
    ge = jnp.exp(gl - jnp.max(gl, axis=1, keepdims=True))
    gprob = ge / jnp.sum(ge, axis=1, keepdims=True)
    gprob = jnp.where(gmask, gprob, -1.0)
    gw = jnp.max(gprob, axis=1, keepdims=True)
    gsel = jnp.min(jnp.where(gprob == gw, lane, far), axis=1, keepdims=True)

    e_lo = ng + gsel * epg
    emask = (lane >= e_lo) & (lane < e_lo + epg)
    el = jnp.where(emask, logits, -jnp.inf)
    v0 = jnp.max(el, axis=1, keepdims=True)
    i0 = jnp.min(jnp.where(emask & (el == v0), lane, far), axis=1, keepdims=True)
    emask1 = emask & (lane != i0)
    el1 = jnp.where(emask1, logits, -jnp.inf)
    v1 = jnp.max(el1, axis=1, keepdims=True)
    i1 = jnp.min(jnp.where(emask1 & (el1 == v1), lane, far), axis=1, keepdims=True)
    e = jnp.exp(v1 - v0)
    p0 = 1.0 / (1.0 + e)
    p1 = e / (1.0 + e)
    out = jnp.where(lane == 0.0, i0 - ng,
                    jnp.where(lane == 1.0, i1 - ng,
                              jnp.where(lane == 2.0, gw * p0, jnp.where(lane == 3.0, gw * p1, 0.0))))
    r_ref[...] = out


def _ln_router(x, mixed, g, b, w_rg, b_rg, w_re, b_re, alpha, tr=256):
    t, d = x.shape
    ncol = N_EXPERT_GROUPS + N_EXPERTS
    wr = jnp.pad(jnp.concatenate([w_rg, w_re], axis=1).astype(F32), ((0, 0), (0, LANES - ncol)))
    br = jnp.pad(jnp.concatenate([b_rg, b_re]).astype(F32), (0, LANES - ncol)).reshape(1, LANES)
    const = lambda i: (0, 0)
    return pl.pallas_call(
        functools.partial(_ln_router_kernel, alpha=alpha),
        grid=(t // tr,),
        in_specs=[pl.BlockSpec((tr, d), lambda i: (i, 0)),
                  pl.BlockSpec((tr, d), lambda i: (i, 0)),
                  pl.BlockSpec((1, d), const), pl.BlockSpec((1, d), const),
                  pl.BlockSpec((d, LANES), const), pl.BlockSpec((1, LANES), const)],
        out_specs=[pl.BlockSpec((tr, d), lambda i: (i, 0)),
                   pl.BlockSpec((tr, d // 2), lambda i: (i, 0)),
                   pl.BlockSpec((tr, LANES), lambda i: (i, 0))],
        out_shape=[jax.ShapeDtypeStruct((t, d), F32), jax.ShapeDtypeStruct((t, d // 2), jnp.uint32),
                   jax.ShapeDtypeStruct((t, LANES), F32)],
        compiler_params=_params(("parallel",)),
        name="ln1_router",
    )(x, mixed, g.astype(F32).reshape(1, d), b.astype(F32).reshape(1, d), wr, br)


def _row_copy(src_ref, dst_ref, sem, tok, row):
    return pltpu.make_async_copy(src_ref.at[pl.ds(tok, 1)], dst_ref.at[pl.ds(row, 1)], sem)


def _moe_kernel(tok_ref, be_ref, bv_ref, nu_ref, x_ref, wg_ref, wu_ref, wd_ref, o_ref,
                xq_ref, xbf_ref, act_ref, gbuf_ref, dbuf_ref, sem, gsem, dsem, *, nf, nn):
    s = pl.program_id(0)
    p = pl.program_id(1)
    n_blk = pl.num_programs(0)
    n_used = nu_ref[0]
    steps = nf + nn
    sub = MOE_SUB
    n_sub = MOE_ROWS // sub
    half = xq_ref.shape[1]
    ks = MOE_K_SPLIT
    kq = wg_ref.shape[1] // ks

    def weights_dma(blk, ph, start):
        e = be_ref[blk]

        @pl.when(ph < nf)
        def _():
            slot = lax.rem(blk * nf + ph, MOE_RING)
            col = pl.multiple_of(ph * MOE_F_CHUNK, MOE_F_CHUNK)
            for q in range(ks):
                rows = pl.ds(q * kq, kq)
                for j, w_ref in enumerate((wg_ref, wu_ref)):
                    cp = pltpu.make_async_copy(w_ref.at[e, rows, pl.ds(col, MOE_F_CHUNK)],
                                               gbuf_ref.at[slot, j, rows], gsem.at[slot])
                    cp.start() if start else cp.wait()

        @pl.when(ph >= nf)
        def _():
            k = ph - nf
            slot = lax.rem(blk * nn + k, MOE_RING)
            for j in range(2):
                col = pl.multiple_of((k + j * nn) * MOE_N_CHUNK, MOE_N_CHUNK)
                cp = pltpu.make_async_copy(wd_ref.at[e, :, pl.ds(col, MOE_N_CHUNK)],
                                           dbuf_ref.at[slot, j], dsem.at[slot])
                cp.start() if start else cp.wait()

    @pl.when((s == 0) & (p == 0))
    def _():
        for a in range(MOE_LOOKAHEAD):
            weights_dma(s, p + a, True)

    ahead = p + MOE_LOOKAHEAD
    blk_a = jnp.where(ahead >= steps, s + 1, s)
    ph_a = jnp.where(ahead >= steps, ahead - steps, ahead)

    @pl.when(blk_a < n_used)
    def _():
        weights_dma(blk_a, ph_a, True)

    @pl.when(s < n_used)
    def _():
        weights_dma(s, p, False)

    def nsub_of(blk):
        return (bv_ref[blk] + (sub - 1)) // sub

    def issue(blk, r):
        base = blk * MOE_ROWS + r * sub

        def body(i, carry):
            for u in range(DMA_UNROLL):
                j = i * DMA_UNROLL + u
                _row_copy(x_ref, xq_ref, sem, tok_ref[base + j], r * sub + j).start()
            return carry
        lax.fori_loop(0, sub // DMA_UNROLL, body, 0)

    nsub = nsub_of(s)

    @pl.when((s == 0) & (p == 0))
    def _():
        for r in range(n_sub):
            @pl.when(r < nsub)
            def _(r=r):
                issue(s, r)

    @pl.when(p == 0)
    def _():
        for r in range(n_sub):
            @pl.when(r < nsub)
            def _(r=r):
                pltpu.make_async_copy(x_ref.at[pl.ds(0, sub)], xq_ref.at[pl.ds(r * sub, sub)], sem).wait()
        for r in range(n_sub):
            @pl.when(r < nsub)
            def _(r=r):
                rs = slice(r * sub, (r + 1) * sub)
                word = xq_ref[rs, :]
                xbf_ref[rs, 0:half] = pltpu.bitcast(word << 16, F32).astype(BF16)
                xbf_ref[rs, half:2 * half] = pltpu.bitcast(word & jnp.uint32(0xFFFF0000), F32).astype(BF16)

    nxt = jnp.minimum(s + 1, n_blk - 1)
    for r in range(n_sub):
        @pl.when((p == r + 1) & (s + 1 < n_blk) & (r < nsub_of(nxt)))
        def _(r=r):
            issue(nxt, r)

    for k in range(1, n_sub + 1):
        m = k * sub

        @pl.when((p < nf) & (nsub == k))
        def _(m=m):
            slot = lax.rem(s * nf + p, MOE_RING)
            gate = up = None
            for q in range(ks):
                xr = xbf_ref[0:m, q * kq:(q + 1) * kq]
                gq = jnp.dot(xr, gbuf_ref[slot, 0, q * kq:(q + 1) * kq, :].astype(BF16), preferred_element_type=F32)
                uq = jnp.dot(xr, gbuf_ref[slot, 1, q * kq:(q + 1) * kq, :].astype(BF16), preferred_element_type=F32)
                gate = gq if gate is None else gate + gq
                up = uq if up is None else up + uq
            act_ref[p, 0:m, :] = (_silu(gate) * up).astype(BF16)

    def down(slot, j, m):
        acc = None
        for q in range(nf):
            wq = dbuf_ref[slot, j, q * MOE_F_CHUNK:(q + 1) * MOE_F_CHUNK, :].astype(BF16)
            t = jnp.dot(act_ref[q, 0:m, :], wq, preferred_element_type=F32)
            acc = t if acc is None else acc + t
        return pltpu.bitcast(acc.astype(BF16).astype(F32), jnp.uint32)

    for k in range(0, n_sub + 1):
        m = k * sub

        @pl.when((p >= nf) & (nsub == k))
        def _(m=m):
            if m > 0:
                slot = lax.rem(s * nn + p - nf, MOE_RING)
                o_ref[0:m, :] = (down(slot, 0, m) >> 16) | down(slot, 1, m)
            if m < MOE_ROWS:
                o_ref[m:MOE_ROWS, :] = jnp.zeros((MOE_ROWS - m, o_ref.shape[1]), jnp.uint32)


def _moe_ffn(xp, row_tok, blk_e, blk_valid, n_used, w_gate, w_up, w_down):
    n_rows = row_tok.shape[0]
    n_e, d, f = w_gate.shape
    nf = f // MOE_F_CHUNK
    nn = (d // 2) // MOE_N_CHUNK
    n_blk = n_rows // MOE_ROWS
    assert nf + nn > MOE_ROWS // MOE_SUB and d % MOE_K_SPLIT == 0
    assert MOE_LOOKAHEAD < MOE_RING and MOE_LOOKAHEAD <= min(nf, nn)

    def out_map(s, p, tok, be, bv, nu):
        return (s, jnp.maximum(p - nf, 0))

    hbm = pl.BlockSpec(memory_space=pl.ANY)
    return pl.pallas_call(
        functools.partial(_moe_kernel, nf=nf, nn=nn),
        grid_spec=pltpu.PrefetchScalarGridSpec(
            num_scalar_prefetch=4,
            grid=(n_blk, nf + nn),
            in_specs=[hbm, hbm, hbm, hbm],
            out_specs=pl.BlockSpec((MOE_ROWS, MOE_N_CHUNK), out_map),
            scratch_shapes=[pltpu.VMEM((MOE_ROWS, d // 2), jnp.uint32),
                            pltpu.VMEM((MOE_ROWS, d), BF16),
                            pltpu.VMEM((nf, MOE_ROWS, MOE_F_CHUNK), BF16),
                            pltpu.VMEM((MOE_RING, 2, d, MOE_F_CHUNK), F32),
                            pltpu.VMEM((MOE_RING, 2, f, MOE_N_CHUNK), F32),
                            pltpu.SemaphoreType.DMA(()),
                            pltpu.SemaphoreType.DMA((MOE_RING,)),
                            pltpu.SemaphoreType.DMA((MOE_RING,))]),
        out_shape=jax.ShapeDtypeStruct((n_rows, d // 2), jnp.uint32),
        compiler_params=_params(("arbitrary", "arbitrary")),
        name="moe_ffn",
    )(row_tok, blk_e, blk_valid, n_used, xp, w_gate, w_up, w_down)


def _combine_kernel(dest_ref, y_ref, x_ref, r_ref, g_ref, b_ref, o_ref, buf_ref, sem, *, alpha, tt):
    s = pl.program_id(0)
    n = pl.num_programs(0)

    rows = 2 * tt

    def start(blk, slot):
        base = blk * rows

        def body(i, carry):
            for u in range(DMA_UNROLL):
                r = i * DMA_UNROLL + u
                pltpu.make_async_copy(y_ref.at[pl.ds(dest_ref[base + r], 1)],
                                      buf_ref.at[slot, pl.ds(r, 1)], sem.at[slot]).start()
            return carry
        lax.fori_loop(0, rows // DMA_UNROLL, body, 0)

    @pl.when(s == 0)
    def _():
        start(s, 0)

    for par in range(2):
        @pl.when((s % 2 == par) & (s + 1 < n))
        def _(par=par):
            start(s + 1, 1 - par)

    slot = s % 2
    pltpu.make_async_copy(y_ref.at[pl.ds(0, rows)], buf_ref.at[slot], sem.at[slot]).wait()
    g0 = r_ref[:, 2:3]
    g1 = r_ref[:, 3:4]
    w0 = buf_ref[slot, 0:tt, :]
    w1 = buf_ref[slot, tt:2 * tt, :]
    half = w0.shape[1]
    himask = jnp.uint32(0xFFFF0000)
    y_lo = g0 * pltpu.bitcast(w0 << 16, F32) + g1 * pltpu.bitcast(w1 << 16, F32)
    y_hi = g0 * pltpu.bitcast(w0 & himask, F32) + g1 * pltpu.bitcast(w1 & himask, F32)
    r_lo = alpha * x_ref[:, 0:half] + y_lo
    r_hi = alpha * x_ref[:, half:2 * half] + y_hi
    inv_d = 1.0 / (2 * half)
    mu = (jnp.sum(r_lo, axis=-1, keepdims=True) + jnp.sum(r_hi, axis=-1, keepdims=True)) * inv_d
    var = (jnp.sum(jnp.square(r_lo - mu), axis=-1, keepdims=True)
           + jnp.sum(jnp.square(r_hi - mu), axis=-1, keepdims=True)) * inv_d
    rstd = lax.rsqrt(var + LN_EPS)
    o_ref[:, 0:half] = (r_lo - mu) * rstd * g_ref[:, 0:half] + b_ref[:, 0:half]
    o_ref[:, half:2 * half] = (r_hi - mu) * rstd * g_ref[:, half:2 * half] + b_ref[:, half:2 * half]


def _combine_ln(y_rows, dest, x1, route, g, b, alpha, tt=128):
    t, d = x1.shape
    const = lambda i, dr: (0, 0)
    return pl.pallas_call(
        functools.partial(_combine_kernel, alpha=alpha, tt=tt),
        grid_spec=pltpu.PrefetchScalarGridSpec(
            num_scalar_prefetch=1,
            grid=(t // tt,),
            in_specs=[pl.BlockSpec(memory_space=pl.ANY),
                      pl.BlockSpec((tt, d), lambda i, dr: (i, 0)),
                      pl.BlockSpec((tt, LANES), lambda i, dr: (i, 0)),
                      pl.BlockSpec((1, d), const), pl.BlockSpec((1, d), const)],
            out_specs=pl.BlockSpec((tt, d), lambda i, dr: (i, 0)),
            scratch_shapes=[pltpu.VMEM((2, 2 * tt, d // 2), jnp.uint32),
                            pltpu.SemaphoreType.DMA((2,))]),
        out_shape=jax.ShapeDtypeStruct((t, d), F32),
        compiler_params=_params(("arbitrary",)),
        name="moe_combine_ln2",
    )(dest, y_rows, x1, route, g.astype(F32).reshape(1, d), b.astype(F32).reshape(1, d))


def _plan_kernel(r_ref, dest_ref, meta_ref, cnt_ref, pstart_ref, carry_ref):
    ph = pl.program_id(0)
    i = pl.program_id(1)
    ts = r_ref.shape[0]
    rows = float(MOE_ROWS)
    lane = lax.broadcasted_iota(I32, (ts, LANES), 1).astype(F32)
    oh0 = jnp.where(lane == r_ref[:, 0:1], 1.0, 0.0)
    oh1 = jnp.where(lane == r_ref[:, 1:2], 1.0, 0.0)
    oh = oh0 + oh1

    @pl.when((ph == 0) & (i == 0))
    def _():
        cnt_ref[...] = jnp.zeros(cnt_ref.shape, F32)

    @pl.when(ph == 0)
    def _():
        cnt_ref[...] += jnp.sum(oh, axis=0, keepdims=True)

    @pl.when((ph == 0) & (i == pl.num_programs(1) - 1))
    def _():
        sq = (LANES, LANES)
        r_i = lax.broadcasted_iota(I32, sq, 0)
        c_i = lax.broadcasted_iota(I32, sq, 1)
        counts = jnp.broadcast_to(cnt_ref[...], sq)
        nblk_e = jnp.floor((counts + (rows - 0.5)) * (1.0 / rows))
        upper = jnp.where(r_i <= c_i, 1.0, 0.0)
        pend = jnp.dot(nblk_e.astype(BF16), upper.astype(BF16), preferred_element_type=F32)
        pstart = pend - nblk_e
        n_used = jnp.max(pend, axis=1, keepdims=True)
        pstart_ref[...] = pstart[0:1, :] * rows
        carry_ref[...] = jnp.zeros(carry_ref.shape, F32)
        b_eff = jnp.minimum(c_i.astype(F32), n_used - 1.0)
        pend_t, pstart_t, counts_t = pend.T, pstart.T, counts.T
        blk_e = jnp.sum(jnp.where(pend_t <= b_eff, 1.0, 0.0), axis=0, keepdims=True)
        blk_e = jnp.minimum(blk_e, float(N_EXPERTS - 1))
        pick = r_i.astype(F32) == blk_e
        cnt_b = jnp.sum(jnp.where(pick, counts_t, 0.0), axis=0, keepdims=True)
        first_b = jnp.sum(jnp.where(pick, pstart_t, 0.0), axis=0, keepdims=True)
        valid = jnp.clip(cnt_b - (b_eff[0:1, :] - first_b) * rows, 0.0, rows)
        valid = jnp.where(c_i[0:1, :].astype(F32) < n_used[0:1, :], valid, 0.0)
        sub_i = lax.broadcasted_iota(I32, meta_ref.shape, 0)
        meta = jnp.where(sub_i == 0, blk_e, jnp.where(sub_i == 1, valid, jnp.where(sub_i == 2, n_used[0:1, :], 0.0)))
        meta_ref[...] = meta.astype(I32)

    @pl.when(ph == 1)
    def _():
        below = lax.broadcasted_iota(I32, (ts, ts), 0) > lax.broadcasted_iota(I32, (ts, ts), 1)
        earlier = jnp.dot(below.astype(BF16), oh.astype(BF16), preferred_element_type=F32)
        base = pstart_ref[...] + carry_ref[...] + earlier
        d0 = jnp.sum(base * oh0, axis=1, keepdims=True)
        d1 = jnp.sum(base * oh1, axis=1, keepdims=True)
        carry_ref[...] += jnp.sum(oh, axis=0, keepdims=True)
        both = jnp.where(lane == 0.0, d0, jnp.where(lane == 1.0, d1, 0.0))
        dest_ref[...] = both.T[0:2, :].astype(I32)


def _dispatch_plan(route, tt):
    t = route.shape[0]
    m = 2 * t
    n_blk = -(-m // MOE_ROWS) + N_EXPERTS
    n_rows = n_blk * MOE_ROWS
    ts = min(PLAN_TILE, t)
    assert n_blk <= LANES and t % ts == 0 and N_EXPERTS <= LANES
    dest2, meta = pl.pallas_call(
        _plan_kernel,
        grid=(2, t // ts),
        in_specs=[pl.BlockSpec((ts, LANES), lambda ph, i: (i, 0))],
        out_specs=[pl.BlockSpec((2, ts), lambda ph, i: (0, i * ph)),
                   pl.BlockSpec((SUBLANES, LANES), lambda ph, i: (0, 0))],
        out_shape=[jax.ShapeDtypeStruct((2, t), I32), jax.ShapeDtypeStruct((SUBLANES, LANES), I32)],
        scratch_shapes=[pltpu.VMEM((1, LANES), F32), pltpu.VMEM((1, LANES), F32), pltpu.VMEM((1, LANES), F32)],
        compiler_params=_params(("arbitrary", "arbitrary")),
        name="moe_plan",
    )(route)
    tok = jnp.arange(t, dtype=I32)
    row_tok = jnp.zeros((n_rows,), I32).at[dest2.reshape(m)].set(jnp.concatenate([tok, tok]))
    dest_tiles = dest2.reshape(2, t // tt, tt).transpose(1, 0, 2).reshape(m)
    return row_tok, meta[0, :n_blk], meta[1, :n_blk], meta[2, 0:1], dest_tiles


def kernel(x, w_in, idx_kn_g, idx_kn_b, conv_w, conv_b, dt_bias, a_log, d_skip, ssd_norm_g, w_out,
           ln1_g, ln1_b, w_rg, b_rg, w_re, b_re, w_gate, w_up, w_down, ln2_g, ln2_b):
    bsz, seq, d = x.shape
    depth = w_in.shape[0]
    alpha = (2 * depth) ** 0.25
    att_w = ATT_HEADS * HEAD_DIM
    kv_w = KV_HEADS * HEAD_DIM
    qi_w = IDX_HEADS * IDX_DIM
    ssd_w = SSD_HEADS * SSD_HEAD_DIM
    xbc_w = ssd_w + 2 * SSD_GROUPS * SSD_STATE
    sizes = (att_w, kv_w, kv_w, qi_w, IDX_DIM, IDX_HEADS, ssd_w, xbc_w, SSD_HEADS)
    offs = [0]
    for sz in sizes:
        offs.append(offs[-1] + sz)
    tt = 128
    xf = x.reshape(bsz * seq, d)
    for l in range(depth):
        col = lambda a, b: w_in[l][:, offs[a]:offs[b]].astype(BF16)
        zpad = lambda n: jnp.zeros((d, n), BF16)
        dt_w = 2 * LANES
        z_off, qi_off = xbc_w, xbc_w + ssd_w
        dt_off = qi_off + qi_w
        kw_off = dt_off + dt_w
        w_rest = jnp.concatenate([col(7, 8), col(6, 7), col(3, 4), col(8, 9), zpad(dt_w - SSD_HEADS),
                                  col(4, 6), zpad(LANES - IDX_DIM - IDX_HEADS), zpad(LANES)], axis=1)
        qkv, xbf = _matmul(xf, col(0, 3), BF16, 512, 512)
        rest = _matmul(xbf, w_rest, F32, 1024, 256)
        att = _dsa_attention(qkv, rest, qi_off, kw_off, idx_kn_g[l], idx_kn_b[l], bsz, seq)
        ssd = _ssd_mixer(rest, z_off, dt_off, dt_w, conv_w[l], conv_b[l], dt_bias[l], a_log[l], d_skip[l],
                         ssd_norm_g[l], bsz, seq)
        mixed = _matmul_pair(att, ssd, w_out[l].astype(BF16), F32, 1024, 512)
        x1, x1p, route = _ln_router(xf, mixed, ln1_g[l], ln1_b[l], w_rg[l], b_rg[l], w_re[l], b_re[l], alpha)
        row_tok, blk_e, blk_valid, n_used, dest_tiles = _dispatch_plan(route, tt)
        y_rows = _moe_ffn(x1p, row_tok, blk_e, blk_valid, n_used, w_gate[l], w_up[l], w_down[l])
        xf = _combine_ln(y_rows, dest_tiles, x1, route, ln2_g[l], ln2_b[l], alpha, tt)
    return xf.reshape(bsz, seq, d)
```

```python
import functools

import jax
import jax.numpy as jnp
from jax import lax
from jax.experimental import pallas as pl
from jax.experimental.pallas import tpu as pltpu

F32 = jnp.float32
BF16 = jnp.bfloat16
I32 = jnp.int32

HEAD_DIM = 128
KV_HEADS = 4
GQA_GROUP = 4
ATT_HEADS = KV_HEADS * GQA_GROUP
IDX_HEADS = 16
IDX_DIM = 64
DSA_TOPK_MAX = 256
QUERY_BLOCK = 128
SSD_HEAD_DIM = 64
SSD_GROUPS = 8
SSD_HEADS_PER_GROUP = 4
SSD_HEADS = SSD_GROUPS * SSD_HEADS_PER_GROUP
SSD_STATE = 128
SSD_CONV = 4
SSD_CHUNK = 128
N_EXPERT_GROUPS = 8
EXPERTS_PER_GROUP = 8
N_EXPERTS = 64
LN_EPS = 1e-5
RMS_EPS = 1e-5

LANES = 128
SUBLANES = 8
VMEM_LIMIT = 56 * 1024 * 1024

KEY_CHUNK = 1024
MOE_ROWS = 768
MOE_SUB = 256
MOE_F_CHUNK = 256
MOE_N_CHUNK = 256
MOE_K_SPLIT = 4
MOE_RING = 3
MOE_LOOKAHEAD = 2
DMA_UNROLL = 8
PLAN_TILE = 512
NEG_BIG = -1e30
LOG2_E = 1.4426950408889634
INT_MIN = -2 ** 31
NEG_INF_KEY = -2139095041


def _params(sem):
    return pltpu.CompilerParams(dimension_semantics=sem, vmem_limit_bytes=VMEM_LIMIT)


def _mm_cast_kernel(a_ref, b_ref, o_ref, abf_ref):
    @pl.when(pl.program_id(1) == 0)
    def _():
        abf_ref[...] = a_ref[...].astype(BF16)

    o_ref[...] = jnp.dot(abf_ref[...], b_ref[...], preferred_element_type=F32).astype(o_ref.dtype)


def _mm_kernel(a_ref, b_ref, o_ref):
    o_ref[...] = jnp.dot(a_ref[...], b_ref[...], preferred_element_type=F32).astype(o_ref.dtype)


def _mm_pair_kernel(a1_ref, a2_ref, b_ref, o_ref):
    k1 = a1_ref.shape[1]
    acc = jnp.dot(a1_ref[...], b_ref[0:k1, :], preferred_element_type=F32)
    acc = acc + jnp.dot(a2_ref[...], b_ref[k1:, :], preferred_element_type=F32)
    o_ref[...] = acc.astype(o_ref.dtype)


def _matmul_pair(a1, a2, b, out_dtype, tm, tn):
    m, k1 = a1.shape
    k2 = a2.shape[1]
    n = b.shape[1]
    tm = min(tm, m)
    assert m % tm == 0 and n % tn == 0 and b.shape[0] == k1 + k2
    return pl.pallas_call(
        _mm_pair_kernel,
        grid=(m // tm, n // tn),
        in_specs=[pl.BlockSpec((tm, k1), lambda i, j: (i, 0)),
                  pl.BlockSpec((tm, k2), lambda i, j: (i, 0)),
                  pl.BlockSpec((k1 + k2, tn), lambda i, j: (0, j))],
        out_specs=pl.BlockSpec((tm, tn), lambda i, j: (i, j)),
        out_shape=jax.ShapeDtypeStruct((m, n), out_dtype),
        compiler_params=_params(("parallel", "arbitrary")),
        name="matmul_pair",
    )(a1, a2, b)


def _matmul(a, b, out_dtype, tm, tn):
    m, k = a.shape
    n = b.shape[1]
    tm = min(tm, m)
    assert m % tm == 0 and n % tn == 0
    cast = a.dtype != BF16
    out_specs = pl.BlockSpec((tm, tn), lambda i, j: (i, j))
    out_shape = jax.ShapeDtypeStruct((m, n), out_dtype)
    if cast:
        out_specs = [out_specs, pl.BlockSpec((tm, k), lambda i, j: (i, 0))]
        out_shape = [out_shape, jax.ShapeDtypeStruct((m, k), BF16)]
    return pl.pallas_call(
        _mm_cast_kernel if cast else _mm_kernel,
        grid=(m // tm, n // tn),
        in_specs=[pl.BlockSpec((tm, k), lambda i, j: (i, 0)),
                  pl.BlockSpec((k, tn), lambda i, j: (0, j))],
        out_specs=out_specs,
        out_shape=out_shape,
        compiler_params=_params(("parallel", "arbitrary")),
        name="matmul_cast" if cast else "matmul",
    )(a, b)


def _attn_kernel(q_ref, k_ref, v_ref, qi_ref, kw_ref, g_ref, b_ref, o_ref,
                 kln_ref, key_ref, bias_ref, s_ref, mrun_ref, lrun_ref, acc_ref, *, top_k):
    i = pl.program_id(1)
    tq = QUERY_BLOCK
    ck = KEY_CHUNK

    @pl.when(i == 0)
    def _():
        kx = kw_ref[:, 0:IDX_DIM]
        mu = jnp.mean(kx, axis=-1, keepdims=True)
        var = jnp.mean(jnp.square(kx - mu), axis=-1, keepdims=True)
        y = (kx - mu) * lax.rsqrt(var + LN_EPS)
        kln_ref[...] = (y * g_ref[...] + b_ref[...]).astype(BF16)

    q_start = i * tq
    n_chunks = (q_start + tq + ck - 1) // ck
    q_pos = q_start + lax.broadcasted_iota(I32, (tq, 1), 0)
    w = kw_ref[pl.ds(pl.multiple_of(q_start, tq), tq), IDX_DIM:IDX_DIM + IDX_HEADS]
    w = w * (IDX_HEADS ** -0.5 * IDX_DIM ** -0.5)
    qi = jnp.concatenate([qi_ref[:, h * IDX_DIM:(h + 1) * IDX_DIM] for h in range(IDX_HEADS)], axis=0).astype(BF16)

    def chunk_off(c):
        return pl.multiple_of(c * ck, ck)

    def key_pos(c):
        return c * ck + lax.broadcasted_iota(I32, (1, ck), 1)

    def score_chunk(c, carry):
        off = chunk_off(c)
        kc = kln_ref[pl.ds(off, ck), :]
        d = lax.dot_general(qi, kc, (((1,), (1,)), ((), ())), preferred_element_type=F32)
        acc = jnp.zeros((tq, ck), F32)
        for h in range(IDX_HEADS):
            acc = acc + jnp.maximum(d[h * tq:(h + 1) * tq, :], 0.0) * w[:, h:h + 1]
        acc = jnp.where(key_pos(c) <= q_pos, acc, -jnp.inf)
        bits = pltpu.bitcast(acc, I32)
        key_ref[:, pl.ds(off, ck)] = bits ^ ((bits >> 31) & 0x7FFFFFFF)
        return carry

    lax.fori_loop(0, n_chunks, score_chunk, 0)

    def bit_body(b, carry):
        cand, cnt_cand = carry
        trial = cand | lax.shift_left(jnp.int32(1), jnp.int32(31) - jnp.asarray(b, I32))
        trial_b = jnp.broadcast_to(trial ^ INT_MIN, (tq, LANES))

        def cnt_chunk(c, cnt):
            kc = key_ref[:, pl.ds(chunk_off(c), ck)]
            for s in range(ck // LANES):
                cnt = cnt + jnp.where(kc[:, s * LANES:(s + 1) * LANES] >= trial_b, 1.0, 0.0)
            return cnt

        cnt = lax.fori_loop(0, n_chunks, cnt_chunk, jnp.zeros((tq, LANES), F32))
        total = jnp.sum(cnt, axis=1, keepdims=True)
        ok = total >= float(top_k)
        return jnp.where(ok, trial, cand), jnp.where(ok, total, cnt_cand)

    n_keys = (jnp.zeros((tq, 1), I32) + n_chunks * ck).astype(F32)
    cand, cnt_ge = lax.fori_loop(0, 32, bit_body, (jnp.zeros((tq, 1), I32), n_keys))
    thr = cand ^ INT_MIN

    def bias_chunk(c, carry):
        off = chunk_off(c)
        sel = (key_ref[:, pl.ds(off, ck)] >= thr) & (key_pos(c) <= q_pos)
        bias_ref[:, pl.ds(off, ck)] = jnp.where(sel, 0.0, NEG_BIG)
        return carry

    lax.fori_loop(0, n_chunks, bias_chunk, 0)

    tie = (cnt_ge > float(top_k)) & (thr > NEG_INF_KEY)

    @pl.when(jnp.max(jnp.where(tie, 1.0, 0.0)) > 0.5)
    def _():
        tri = (lax.broadcasted_iota(I32, (ck, ck), 0) <= lax.broadcasted_iota(I32, (ck, ck), 1)).astype(BF16)

        def gt_chunk(c, cnt):
            kc = key_ref[:, pl.ds(chunk_off(c), ck)]
            return cnt + jnp.sum(jnp.where(kc > thr, 1.0, 0.0), axis=1, keepdims=True)

        need = float(top_k) - lax.fori_loop(0, n_chunks, gt_chunk, jnp.zeros((tq, 1), F32))

        def tie_chunk(c, seen):
            off = chunk_off(c)
            kc = key_ref[:, pl.ds(off, ck)]
            eq = jnp.where(kc == thr, 1.0, 0.0)
            rank = seen + jnp.dot(eq.astype(BF16), tri, preferred_element_type=F32)
            keep = (kc > thr) | ((kc == thr) & ((rank <= need) | jnp.logical_not(tie)))
            sel = keep & (key_pos(c) <= q_pos)
            bias_ref[:, pl.ds(off, ck)] = jnp.where(sel, 0.0, NEG_BIG)
            return seen + jnp.sum(eq, axis=1, keepdims=True)

        lax.fori_loop(0, n_chunks, tie_chunk, jnp.zeros((tq, 1), F32))

    scale = HEAD_DIM ** -0.5 * LOG2_E
    gq = GQA_GROUP
    for g in range(KV_HEADS):
        qg = jnp.concatenate([q_ref[:, (g * gq + j) * HEAD_DIM:(g * gq + j + 1) * HEAD_DIM] for j in range(gq)],
                             axis=0)

        mrun_ref[...] = jnp.full(mrun_ref.shape, NEG_BIG, F32)

        def logit_chunk(c, carry, g=g, qg=qg):
            off = chunk_off(c)
            kc = k_ref[pl.ds(off, ck), g * HEAD_DIM:(g + 1) * HEAD_DIM]
            s = lax.dot_general(qg, kc, (((1,), (1,)), ((), ())), preferred_element_type=F32)
            bias = bias_ref[:, pl.ds(off, ck)]
            s = s * scale + jnp.concatenate([bias] * gq, axis=0)
            s_ref[:, pl.ds(off, ck)] = s
            m = mrun_ref[...]
            for t in range(ck // LANES):
                m = jnp.maximum(m, s[:, t * LANES:(t + 1) * LANES])
            mrun_ref[...] = m
            return carry

        lax.fori_loop(0, n_chunks, logit_chunk, 0)
        m = jnp.max(mrun_ref[...], axis=1, keepdims=True)

        lrun_ref[...] = jnp.zeros(lrun_ref.shape, F32)
        acc_ref[...] = jnp.zeros(acc_ref.shape, F32)

        def prob_chunk(c, carry, g=g, m=m):
            off = chunk_off(c)
            vc = v_ref[pl.ds(off, ck), g * HEAD_DIM:(g + 1) * HEAD_DIM]
            p = jnp.exp2(s_ref[:, pl.ds(off, ck)] - m)
            l = lrun_ref[...]
            for t in range(ck // LANES):
                l = l + p[:, t * LANES:(t + 1) * LANES]
            lrun_ref[...] = l
            acc_ref[...] += jnp.dot(p.astype(BF16), vc, preferred_element_type=F32)
            return carry

        lax.fori_loop(0, n_chunks, prob_chunk, 0)
        out = acc_ref[...] / jnp.sum(lrun_ref[...], axis=1, keepdims=True)
        for j in range(gq):
            h = g * gq + j
            o_ref[:, h * HEAD_DIM:(h + 1) * HEAD_DIM] = out[j * tq:(j + 1) * tq, :].astype(o_ref.dtype)


def _dsa_attention(qkv, idx, qi_off, kw_off, kn_g, kn_b, bsz, seq):
    top_k = min(DSA_TOPK_MAX, seq // 4)
    nq = seq // QUERY_BLOCK
    att_w = ATT_HEADS * HEAD_DIM
    kv_w = KV_HEADS * HEAD_DIM
    qi_w = IDX_HEADS * IDX_DIM
    assert seq % KEY_CHUNK == 0 and att_w % kv_w == 0 and qi_off % qi_w == 0 and kw_off % LANES == 0
    return pl.pallas_call(
        functools.partial(_attn_kernel, top_k=top_k),
        grid=(bsz, nq),
        in_specs=[pl.BlockSpec((QUERY_BLOCK, att_w), lambda b, i: (b * nq + i, 0)),
                  pl.BlockSpec((seq, kv_w), lambda b, i: (b, att_w // kv_w)),
                  pl.BlockSpec((seq, kv_w), lambda b, i: (b, att_w // kv_w + 1)),
                  pl.BlockSpec((QUERY_BLOCK, qi_w), lambda b, i: (b * nq + i, qi_off // qi_w)),
                  pl.BlockSpec((seq, LANES), lambda b, i: (b, kw_off // LANES)),
                  pl.BlockSpec((1, IDX_DIM), lambda b, i: (0, 0)),
                  pl.BlockSpec((1, IDX_DIM), lambda b, i: (0, 0))],
        out_specs=pl.BlockSpec((QUERY_BLOCK, att_w), lambda b, i: (b * nq + i, 0)),
        out_shape=jax.ShapeDtypeStruct((bsz * seq, att_w), BF16),
        scratch_shapes=[pltpu.VMEM((seq, IDX_DIM), BF16),
                        pltpu.VMEM((QUERY_BLOCK, seq), I32),
                        pltpu.VMEM((QUERY_BLOCK, seq), F32),
                        pltpu.VMEM((GQA_GROUP * QUERY_BLOCK, seq), F32),
                        pltpu.VMEM((GQA_GROUP * QUERY_BLOCK, LANES), F32),
                        pltpu.VMEM((GQA_GROUP * QUERY_BLOCK, LANES), F32),
                        pltpu.VMEM((GQA_GROUP * QUERY_BLOCK, HEAD_DIM), F32)],
        compiler_params=_params(("parallel", "arbitrary")),
        name="dsa_attention",
    )(qkv, qkv, qkv, idx, idx, kn_g.reshape(1, IDX_DIM), kn_b.reshape(1, IDX_DIM))


def _silu(x):
    return x / (1.0 + jnp.exp(-x))


def _ssd_kernel(xbc_ref, z_ref, dt_ref, cw_ref, cb_ref, dtb_ref, alog_ref, dsk_ref, ng_ref, o_ref,
                xpad_ref, act_ref, *h_refs):
    c = pl.program_id(1)
    cq = SSD_CHUNK
    width = SSD_HEADS * SSD_HEAD_DIM
    b_off = width
    c_off = width + SSD_GROUPS * SSD_STATE

    @pl.when(c == 0)
    def _():
        xpad_ref[0:SUBLANES, :] = jnp.zeros((SUBLANES, xpad_ref.shape[1]), F32)
        for h_ref in h_refs:
            h_ref[...] = jnp.zeros(h_ref.shape, F32)

    xpad_ref[SUBLANES:SUBLANES + cq, :] = xbc_ref[...]
    col = 512
    for j in range(xpad_ref.shape[1] // col):
        cs = slice(j * col, (j + 1) * col)
        acc = cb_ref[:, cs] + jnp.zeros((cq, col), F32)
        for t in range(SSD_CONV):
            r0 = SUBLANES - (SSD_CONV - 1) + t
            acc = acc + xpad_ref[r0:r0 + cq, cs] * cw_ref[t:t + 1, cs]
        act_ref[:, cs] = _silu(acc)
    xpad_ref[0:SUBLANES, :] = xpad_ref[cq:cq + SUBLANES, :]

    xdt_in = dt_ref[:, 0:LANES] + dtb_ref[...]
    dt = jnp.maximum(xdt_in, 0.0) + jnp.log1p(jnp.exp(-jnp.abs(xdt_in)))
    da = dt * (-jnp.exp(alog_ref[...]))
    row = lax.broadcasted_iota(I32, (cq, cq), 0)
    coli = lax.broadcasted_iota(I32, (cq, cq), 1)
    causal = row >= coli
    tril = causal.astype(F32)
    acs = jnp.dot(tril, da, preferred_element_type=F32, precision=lax.Precision.HIGHEST)
    acs_t = acs.T
    dec_in = jnp.exp(acs)
    a_last = acs[cq - 1:cq, :]
    dec_out = jnp.exp(a_last - acs)
    dec_chunk = jnp.exp(a_last)

    gw = width // SSD_GROUPS
    for g in range(SSD_GROUPS):
        h_ref = h_refs[g]
        bg = act_ref[:, b_off + g * SSD_STATE:b_off + (g + 1) * SSD_STATE].astype(BF16)
        cg = act_ref[:, c_off + g * SSD_STATE:c_off + (g + 1) * SSD_STATE].astype(BF16)
        cbm = lax.dot_general(cg, bg, (((1,), (1,)), ((), ())), preferred_element_type=F32)
        hprevs = [h_ref[j] for j in range(SSD_HEADS_PER_GROUP)]
        ys, hnews = [], []
        for j in range(SSD_HEADS_PER_GROUP):
            hd = g * SSD_HEADS_PER_GROUP + j
            xs = slice(hd * SSD_HEAD_DIM, (hd + 1) * SSD_HEAD_DIM)
            seg = acs[:, hd:hd + 1] - acs_t[hd:hd + 1, :]
            lmat = jnp.exp(jnp.where(causal, seg, -jnp.inf))
            xh = act_ref[:, xs]
            xdt = xh * dt[:, hd:hd + 1]
            y = jnp.dot((cbm * lmat).astype(BF16), xdt.astype(BF16), preferred_element_type=F32)
            yoff = lax.dot_general(cg, hprevs[j].astype(BF16), (((1,), (1,)), ((), ())),
                                   preferred_element_type=F32)
            ys.append(y + yoff * dec_in[:, hd:hd + 1] + xh * dsk_ref[:, xs])
            st = lax.dot_general((xdt * dec_out[:, hd:hd + 1]).astype(BF16), bg,
                                 (((0,), (0,)), ((), ())), preferred_element_type=F32)
            hnews.append(hprevs[j] * dec_chunk[:, hd:hd + 1] + st)
        gs = slice(g * gw, (g + 1) * gw)
        gated = jnp.concatenate(ys, axis=1) * _silu(z_ref[:, gs])
        ms = jnp.mean(jnp.square(gated), axis=-1, keepdims=True)
        o_ref[:, gs] = (gated * lax.rsqrt(ms + RMS_EPS) * ng_ref[:, gs]).astype(o_ref.dtype)
        for j in range(SSD_HEADS_PER_GROUP):
            h_ref[j] = hnews[j]


def _ssd_mixer(ssd_in, z_off, dt_off, dt_w, conv_w, conv_b, dt_bias, a_log, d_skip, norm_g, bsz, seq):
    nc = seq // SSD_CHUNK
    width = SSD_HEADS * SSD_HEAD_DIM
    xbc_w = width + 2 * SSD_GROUPS * SSD_STATE
    assert z_off % width == 0 and dt_off % dt_w == 0 and dt_w >= LANES
    pad = LANES - SSD_HEADS
    dtb = jnp.pad(dt_bias.astype(F32), (0, pad)).reshape(1, LANES)
    alog = jnp.pad(a_log.astype(F32), (0, pad)).reshape(1, LANES)
    dsk = jnp.repeat(d_skip.astype(F32), SSD_HEAD_DIM).reshape(1, width)
    const = lambda b, c: (0, 0)
    return pl.pallas_call(
        _ssd_kernel,
        grid=(bsz, nc),
        in_specs=[pl.BlockSpec((SSD_CHUNK, xbc_w), lambda b, c: (b * nc + c, 0)),
                  pl.BlockSpec((SSD_CHUNK, width), lambda b, c: (b * nc + c, z_off // width)),
                  pl.BlockSpec((SSD_CHUNK, dt_w), lambda b, c: (b * nc + c, dt_off // dt_w)),
                  pl.BlockSpec((SSD_CONV, xbc_w), const),
                  pl.BlockSpec((1, xbc_w), const),
                  pl.BlockSpec((1, LANES), const),
                  pl.BlockSpec((1, LANES), const),
                  pl.BlockSpec((1, width), const),
                  pl.BlockSpec((1, width), const)],
        out_specs=pl.BlockSpec((SSD_CHUNK, width), lambda b, c: (b * nc + c, 0)),
        out_shape=jax.ShapeDtypeStruct((bsz * seq, width), BF16),
        scratch_shapes=[pltpu.VMEM((SSD_CHUNK + SUBLANES, xbc_w), F32),
                        pltpu.VMEM((SSD_CHUNK, xbc_w), F32)]
        + [pltpu.VMEM((SSD_HEADS_PER_GROUP, SSD_HEAD_DIM, SSD_STATE), F32) for _ in range(SSD_GROUPS)],
        compiler_params=_params(("parallel", "arbitrary")),
        name="ssd_mixer",
    )(ssd_in, ssd_in, ssd_in, conv_w.astype(F32), conv_b.astype(F32).reshape(1, xbc_w), dtb, alog, dsk,
      norm_g.astype(F32).reshape(1, width))


def _layer_norm_rows(x, g, b):
    mu = jnp.mean(x, axis=-1, keepdims=True)
    var = jnp.mean(jnp.square(x - mu), axis=-1, keepdims=True)
    return (x - mu) * lax.rsqrt(var + LN_EPS) * g + b


def _ln_router_kernel(x_ref, mix_ref, g_ref, b_ref, wr_ref, br_ref, o_ref, xp_ref, r_ref, *, alpha):
    x1 = _layer_norm_rows(alpha * x_ref[...] + mix_ref[...], g_ref[...], b_ref[...])
    o_ref[...] = x1
    half = xp_ref.shape[1]
    bits = pltpu.bitcast(x1.astype(BF16).astype(F32), jnp.uint32)
    xp_ref[...] = (bits[:, :half] >> 16) | bits[:, half:]
    logits = jnp.dot(x1, wr_ref[...], preferred_element_type=F32, precision=lax.Precision.HIGHEST)
    logits = logits + br_ref[...]
    rows = logits.shape[0]
    lane = lax.broadcasted_iota(I32, (rows, LANES), 1).astype(F32)
    ng, epg = float(N_EXPERT_GROUPS), float(EXPERTS_PER_GROUP)
    far = float(LANES)

    gmask = lane < ng
    gl = jnp.where(gmask, logits, -jnp.inf)
    ge = jnp.exp(gl - jnp.max(gl, axis=1, keepdims=True))
    gprob = ge / jnp.sum(ge, axis=1, keepdims=True)
    gprob = jnp.where(gmask, gprob, -1.0)
    gw = jnp.max(gprob, axis=1, keepdims=True)
    gsel = jnp.min(jnp.where(gprob == gw, lane, far), axis=1, keepdims=True)

    e_lo = ng + gsel * epg
    emask = (lane >= e_lo) & (lane < e_lo + epg)
    el = jnp.where(emask, logits, -jnp.inf)
    v0 = jnp.max(el, axis=1, keepdims=True)
    i0 = jnp.min(jnp.where(emask & (el == v0), lane, far), axis=1, keepdims=True)
    emask1 = emask & (lane != i0)
    el1 = jnp.where(emask1, logits, -jnp.inf)
    v1 = jnp.max(el1, axis=1, keepdims=True)
    i1 = jnp.min(jnp.where(emask1 & (el1 == v1), lane, far), axis=1, keepdims=True)
    e = jnp.exp(v1 - v0)
    p0 = 1.0 / (1.0 + e)
    p1 = e / (1.0 + e)
    out = jnp.where(lane == 0.0, i0 - ng,
                    jnp.where(lane == 1.0, i1 - ng,
                              jnp.where(lane == 2.0, gw * p0, jnp.where(lane == 3.0, gw * p1, 0.0))))
    r_ref[...] = out


def _ln_router(x, mixed, g, b, w_rg, b_rg, w_re, b_re, alpha, tr=256):
    t, d = x.shape
    ncol = N_EXPERT_GROUPS + N_EXPERTS
    wr = jnp.pad(jnp.concatenate([w_rg, w_re], axis=1).astype(F32), ((0, 0), (0, LANES - ncol)))
    br = jnp.pad(jnp.concatenate([b_rg, b_re]).astype(F32), (0, LANES - ncol)).reshape(1, LANES)
    const = lambda i: (0, 0)
    return pl.pallas_call(
        functools.partial(_ln_router_kernel, alpha=alpha),
        grid=(t // tr,),
        in_specs=[pl.BlockSpec((tr, d), lambda i: (i, 0)),
                  pl.BlockSpec((tr, d), lambda i: (i, 0)),
                  pl.BlockSpec((1, d), const), pl.BlockSpec((1, d), const),
                  pl.BlockSpec((d, LANES), const), pl.BlockSpec((1, LANES), const)],
        out_specs=[pl.BlockSpec((tr, d), lambda i: (i, 0)),
                   pl.BlockSpec((tr, d // 2), lambda i: (i, 0)),
                   pl.BlockSpec((tr, LANES), lambda i: (i, 0))],
        out_shape=[jax.ShapeDtypeStruct((t, d), F32), jax.ShapeDtypeStruct((t, d // 2), jnp.uint32),
                   jax.ShapeDtypeStruct((t, LANES), F32)],
        compiler_params=_params(("parallel",)),
        name="ln1_router",
    )(x, mixed, g.astype(F32).reshape(1, d), b.astype(F32).reshape(1, d), wr, br)


def _row_copy(src_ref, dst_ref, sem, tok, row):
    return pltpu.make_async_copy(src_ref.at[pl.ds(tok, 1)], dst_ref.at[pl.ds(row, 1)], sem)


def _moe_kernel(tok_ref, be_ref, bv_ref, nu_ref, x_ref, wg_ref, wu_ref, wd_ref, o_ref,
                xq_ref, xbf_ref, act_ref, gbuf_ref, dbuf_ref, sem, gsem, dsem, *, nf, nn):
    s = pl.program_id(0)
    p = pl.program_id(1)
    n_blk = pl.num_programs(0)
    n_used = nu_ref[0]
    steps = nf + nn
    sub = MOE_SUB
    n_sub = MOE_ROWS // sub
    half = xq_ref.shape[1]
    ks = MOE_K_SPLIT
    kq = wg_ref.shape[1] // ks

    def weights_dma(blk, ph, start):
        e = be_ref[blk]

        @pl.when(ph < nf)
        def _():
            slot = lax.rem(blk * nf + ph, MOE_RING)
            col = pl.multiple_of(ph * MOE_F_CHUNK, MOE_F_CHUNK)
            for q in range(ks):
                rows = pl.ds(q * kq, kq)
                for j, w_ref in enumerate((wg_ref, wu_ref)):
                    cp = pltpu.make_async_copy(w_ref.at[e, rows, pl.ds(col, MOE_F_CHUNK)],
                                               gbuf_ref.at[slot, j, rows], gsem.at[slot])
                    cp.start() if start else cp.wait()

        @pl.when(ph >= nf)
        def _():
            k = ph - nf
            slot = lax.rem(blk * nn + k, MOE_RING)
            for j in range(2):
                col = pl.multiple_of((k + j * nn) * MOE_N_CHUNK, MOE_N_CHUNK)
                cp = pltpu.make_async_copy(wd_ref.at[e, :, pl.ds(col, MOE_N_CHUNK)],
                                           dbuf_ref.at[slot, j], dsem.at[slot])
                cp.start() if start else cp.wait()

    @pl.when((s == 0) & (p == 0))
    def _():
        for a in range(MOE_LOOKAHEAD):
            weights_dma(s, p + a, True)

    ahead = p + MOE_LOOKAHEAD
    blk_a = jnp.where(ahead >= steps, s + 1, s)
    ph_a = jnp.where(ahead >= steps, ahead - steps, ahead)

    @pl.when(blk_a < n_used)
    def _():
        weights_dma(blk_a, ph_a, True)

    @pl.when(s < n_used)
    def _():
        weights_dma(s, p, False)

    def nsub_of(blk):
        return (bv_ref[blk] + (sub - 1)) // sub

    def issue(blk, r):
        base = blk * MOE_ROWS + r * sub

        def body(i, carry):
            for u in range(DMA_UNROLL):
                j = i * DMA_UNROLL + u
                _row_copy(x_ref, xq_ref, sem, tok_ref[base + j], r * sub + j).start()
            return carry
        lax.fori_loop(0, sub // DMA_UNROLL, body, 0)

    nsub = nsub_of(s)

    @pl.when((s == 0) & (p == 0))
    def _():
        for r in range(n_sub):
            @pl.when(r < nsub)
            def _(r=r):
                issue(s, r)

    @pl.when(p == 0)
    def _():
        for r in range(n_sub):
            @pl.when(r < nsub)
            def _(r=r):
                pltpu.make_async_copy(x_ref.at[pl.ds(0, sub)], xq_ref.at[pl.ds(r * sub, sub)], sem).wait()
        for r in range(n_sub):
            @pl.when(r < nsub)
            def _(r=r):
                rs = slice(r * sub, (r + 1) * sub)
                word = xq_ref[rs, :]
                xbf_ref[rs, 0:half] = pltpu.bitcast(word << 16, F32).astype(BF16)
                xbf_ref[rs, half:2 * half] = pltpu.bitcast(word & jnp.uint32(0xFFFF0000), F32).astype(BF16)

    nxt = jnp.minimum(s + 1, n_blk - 1)
    for r in range(n_sub):
        @pl.when((p == r + 1) & (s + 1 < n_blk) & (r < nsub_of(nxt)))
        def _(r=r):
            issue(nxt, r)

    for k in range(1, n_sub + 1):
        m = k * sub

        @pl.when((p < nf) & (nsub == k))
        def _(m=m):
            slot = lax.rem(s * nf + p, MOE_RING)
            gate = up = None
            for q in range(ks):
                xr = xbf_ref[0:m, q * kq:(q + 1) * kq]
                gq = jnp.dot(xr, gbuf_ref[slot, 0, q * kq:(q + 1) * kq, :].astype(BF16), preferred_element_type=F32)
                uq = jnp.dot(xr, gbuf_ref[slot, 1, q * kq:(q + 1) * kq, :].astype(BF16), preferred_element_type=F32)
                gate = gq if gate is None else gate + gq
                up = uq if up is None else up + uq
            act_ref[p, 0:m, :] = (_silu(gate) * up).astype(BF16)

    def down(slot, j, m):
        acc = None
        for q in range(nf):
            wq = dbuf_ref[slot, j, q * MOE_F_CHUNK:(q + 1) * MOE_F_CHUNK, :].astype(BF16)
            t = jnp.dot(act_ref[q, 0:m, :], wq, preferred_element_type=F32)
            acc = t if acc is None else acc + t
        return pltpu.bitcast(acc.astype(BF16).astype(F32), jnp.uint32)

    for k in range(0, n_sub + 1):
        m = k * sub

        @pl.when((p >= nf) & (nsub == k))
        def _(m=m):
            if m > 0:
                slot = lax.rem(s * nn + p - nf, MOE_RING)
                o_ref[0:m, :] = (down(slot, 0, m) >> 16) | down(slot, 1, m)
            if m < MOE_ROWS:
                o_ref[m:MOE_ROWS, :] = jnp.zeros((MOE_ROWS - m, o_ref.shape[1]), jnp.uint32)


def _moe_ffn(xp, row_tok, blk_e, blk_valid, n_used, w_gate, w_up, w_down):
    n_rows = row_tok.shape[0]
    n_e, d, f = w_gate.shape
    nf = f // MOE_F_CHUNK
    nn = (d // 2) // MOE_N_CHUNK
    n_blk = n_rows // MOE_ROWS
    assert nf + nn > MOE_ROWS // MOE_SUB and d % MOE_K_SPLIT == 0
    assert MOE_LOOKAHEAD < MOE_RING and MOE_LOOKAHEAD <= min(nf, nn)

    def out_map(s, p, tok, be, bv, nu):
        return (s, jnp.maximum(p - nf, 0))

    hbm = pl.BlockSpec(memory_space=pl.ANY)
    return pl.pallas_call(
        functools.partial(_moe_kernel, nf=nf, nn=nn),
        grid_spec=pltpu.PrefetchScalarGridSpec(
            num_scalar_prefetch=4,
            grid=(n_blk, nf + nn),
            in_specs=[hbm, hbm, hbm, hbm],
            out_specs=pl.BlockSpec((MOE_ROWS, MOE_N_CHUNK), out_map),
            scratch_shapes=[pltpu.VMEM((MOE_ROWS, d // 2), jnp.uint32),
                            pltpu.VMEM((MOE_ROWS, d), BF16),
                            pltpu.VMEM((nf, MOE_ROWS, MOE_F_CHUNK), BF16),
                            pltpu.VMEM((MOE_RING, 2, d, MOE_F_CHUNK), F32),
                            pltpu.VMEM((MOE_RING, 2, f, MOE_N_CHUNK), F32),
                            pltpu.SemaphoreType.DMA(()),
                            pltpu.SemaphoreType.DMA((MOE_RING,)),
                            pltpu.SemaphoreType.DMA((MOE_RING,))]),
        out_shape=jax.ShapeDtypeStruct((n_rows, d // 2), jnp.uint32),
        compiler_params=_params(("arbitrary", "arbitrary")),
        name="moe_ffn",
    )(row_tok, blk_e, blk_valid, n_used, xp, w_gate, w_up, w_down)


def _combine_kernel(dest_ref, y_ref, x_ref, r_ref, g_ref, b_ref, o_ref, buf_ref, sem, *, alpha, tt):
    s = pl.program_id(0)
    n = pl.num_programs(0)

    rows = 2 * tt

    def start(blk, slot):
        base = blk * rows

        def body(i, carry):
            for u in range(DMA_UNROLL):
                r = i * DMA_UNROLL + u
                pltpu.make_async_copy(y_ref.at[pl.ds(dest_ref[base + r], 1)],
                                      buf_ref.at[slot, pl.ds(r, 1)], sem.at[slot]).start()
            return carry
        lax.fori_loop(0, rows // DMA_UNROLL, body, 0)

    @pl.when(s == 0)
    def _():
        start(s, 0)

    for par in range(2):
        @pl.when((s % 2 == par) & (s + 1 < n))
        def _(par=par):
            start(s + 1, 1 - par)

    slot = s % 2
    pltpu.make_async_copy(y_ref.at[pl.ds(0, rows)], buf_ref.at[slot], sem.at[slot]).wait()
    g0 = r_ref[:, 2:3]
    g1 = r_ref[:, 3:4]
    w0 = buf_ref[slot, 0:tt, :]
    w1 = buf_ref[slot, tt:2 * tt, :]
    half = w0.shape[1]
    himask = jnp.uint32(0xFFFF0000)
    y_lo = g0 * pltpu.bitcast(w0 << 16, F32) + g1 * pltpu.bitcast(w1 << 16, F32)
    y_hi = g0 * pltpu.bitcast(w0 & himask, F32) + g1 * pltpu.bitcast(w1 & himask, F32)
    r_lo = alpha * x_ref[:, 0:half] + y_lo
    r_hi = alpha * x_ref[:, half:2 * half] + y_hi
    inv_d = 1.0 / (2 * half)
    mu = (jnp.sum(r_lo, axis=-1, keepdims=True) + jnp.sum(r_hi, axis=-1, keepdims=True)) * inv_d
    var = (jnp.sum(jnp.square(r_lo - mu), axis=-1, keepdims=True)
           + jnp.sum(jnp.square(r_hi - mu), axis=-1, keepdims=True)) * inv_d
    rstd = lax.rsqrt(var + LN_EPS)
    o_ref[:, 0:half] = (r_lo - mu) * rstd * g_ref[:, 0:half] + b_ref[:, 0:half]
    o_ref[:, half:2 * half] = (r_hi - mu) * rstd * g_ref[:, half:2 * half] + b_ref[:, half:2 * half]


def _combine_ln(y_rows, dest, x1, route, g, b, alpha, tt=128):
    t, d = x1.shape
    const = lambda i, dr: (0, 0)
    return pl.pallas_call(
        functools.partial(_combine_kernel, alpha=alpha, tt=tt),
        grid_spec=pltpu.PrefetchScalarGridSpec(
            num_scalar_prefetch=1,
            grid=(t // tt,),
            in_specs=[pl.BlockSpec(memory_space=pl.ANY),
                      pl.BlockSpec((tt, d), lambda i, dr: (i, 0)),
                      pl.BlockSpec((tt, LANES), lambda i, dr: (i, 0)),
                      pl.BlockSpec((1, d), const), pl.BlockSpec((1, d), const)],
            out_specs=pl.BlockSpec((tt, d), lambda i, dr: (i, 0)),
            scratch_shapes=[pltpu.VMEM((2, 2 * tt, d // 2), jnp.uint32),
                            pltpu.SemaphoreType.DMA((2,))]),
        out_shape=jax.ShapeDtypeStruct((t, d), F32),
        compiler_params=_params(("arbitrary",)),
        name="moe_combine_ln2",
    )(dest, y_rows, x1, route, g.astype(F32).reshape(1, d), b.astype(F32).reshape(1, d))


def _plan_kernel(r_ref, dest_ref, meta_ref, cnt_ref, pstart_ref, carry_ref):
    ph = pl.program_id(0)
    i = pl.program_id(1)
    ts = r_ref.shape[0]
    rows = float(MOE_ROWS)
    lane = lax.broadcasted_iota(I32, (ts, LANES), 1).astype(F32)
    oh0 = jnp.where(lane == r_ref[:, 0:1], 1.0, 0.0)
    oh1 = jnp.where(lane == r_ref[:, 1:2], 1.0, 0.0)
    oh = oh0 + oh1

    @pl.when((ph == 0) & (i == 0))
    def _():
        cnt_ref[...] = jnp.zeros(cnt_ref.shape, F32)

    @pl.when(ph == 0)
    def _():
        cnt_ref[...] += jnp.sum(oh, axis=0, keepdims=True)

    @pl.when((ph == 0) & (i == pl.num_programs(1) - 1))
    def _():
        sq = (LANES, LANES)
        r_i = lax.broadcasted_iota(I32, sq, 0)
        c_i = lax.broadcasted_iota(I32, sq, 1)
        counts = jnp.broadcast_to(cnt_ref[...], sq)
        nblk_e = jnp.floor((counts + (rows - 0.5)) * (1.0 / rows))
        upper = jnp.where(r_i <= c_i, 1.0, 0.0)
        pend = jnp.dot(nblk_e.astype(BF16), upper.astype(BF16), preferred_element_type=F32)
        pstart = pend - nblk_e
        n_used = jnp.max(pend, axis=1, keepdims=True)
        pstart_ref[...] = pstart[0:1, :] * rows
        carry_ref[...] = jnp.zeros(carry_ref.shape, F32)
        b_eff = jnp.minimum(c_i.astype(F32), n_used - 1.0)
        pend_t, pstart_t, counts_t = pend.T, pstart.T, counts.T
        blk_e = jnp.sum(jnp.where(pend_t <= b_eff, 1.0, 0.0), axis=0, keepdims=True)
        blk_e = jnp.minimum(blk_e, float(N_EXPERTS - 1))
        pick = r_i.astype(F32) == blk_e
        cnt_b = jnp.sum(jnp.where(pick, counts_t, 0.0), axis=0, keepdims=True)
        first_b = jnp.sum(jnp.where(pick, pstart_t, 0.0), axis=0, keepdims=True)
        valid = jnp.clip(cnt_b - (b_eff[0:1, :] - first_b) * rows, 0.0, rows)
        valid = jnp.where(c_i[0:1, :].astype(F32) < n_used[0:1, :], valid, 0.0)
        sub_i = lax.broadcasted_iota(I32, meta_ref.shape, 0)
        meta = jnp.where(sub_i == 0, blk_e, jnp.where(sub_i == 1, valid, jnp.where(sub_i == 2, n_used[0:1, :], 0.0)))
        meta_ref[...] = meta.astype(I32)

    @pl.when(ph == 1)
    def _():
        below = lax.broadcasted_iota(I32, (ts, ts), 0) > lax.broadcasted_iota(I32, (ts, ts), 1)
        earlier = jnp.dot(below.astype(BF16), oh.astype(BF16), preferred_element_type=F32)
        base = pstart_ref[...] + carry_ref[...] + earlier
        d0 = jnp.sum(base * oh0, axis=1, keepdims=True)
        d1 = jnp.sum(base * oh1, axis=1, keepdims=True)
        carry_ref[...] += jnp.sum(oh, axis=0, keepdims=True)
        both = jnp.where(lane == 0.0, d0, jnp.where(lane == 1.0, d1, 0.0))
        dest_ref[...] = both.T[0:2, :].astype(I32)


def _dispatch_plan(route, tt):
    t = route.shape[0]
    m = 2 * t
    n_blk = -(-m // MOE_ROWS) + N_EXPERTS
    n_rows = n_blk * MOE_ROWS
    ts = min(PLAN_TILE, t)
    assert n_blk <= LANES and t % ts == 0 and N_EXPERTS <= LANES
    dest2, meta = pl.pallas_call(
        _plan_kernel,
        grid=(2, t // ts),
        in_specs=[pl.BlockSpec((ts, LANES), lambda ph, i: (i, 0))],
        out_specs=[pl.BlockSpec((2, ts), lambda ph, i: (0, i * ph)),
                   pl.BlockSpec((SUBLANES, LANES), lambda ph, i: (0, 0))],
        out_shape=[jax.ShapeDtypeStruct((2, t), I32), jax.ShapeDtypeStruct((SUBLANES, LANES), I32)],
        scratch_shapes=[pltpu.VMEM((1, LANES), F32), pltpu.VMEM((1, LANES), F32), pltpu.VMEM((1, LANES), F32)],
        compiler_params=_params(("arbitrary", "arbitrary")),
        name="moe_plan",
    )(route)
    tok = jnp.arange(t, dtype=I32)
    row_tok = jnp.zeros((n_rows,), I32).at[dest2.reshape(m)].set(jnp.concatenate([tok, tok]))
    dest_tiles = dest2.reshape(2, t // tt, tt).transpose(1, 0, 2).reshape(m)
    return row_tok, meta[0, :n_blk], meta[1, :n_blk], meta[2, 0:1], dest_tiles


def kernel(x, w_in, idx_kn_g, idx_kn_b, conv_w, conv_b, dt_bias, a_log, d_skip, ssd_norm_g, w_out,
           ln1_g, ln1_b, w_rg, b_rg, w_re, b_re, w_gate, w_up, w_down, ln2_g, ln2_b):
    bsz, seq, d = x.shape
    depth = w_in.shape[0]
    alpha = (2 * depth) ** 0.25
    att_w = ATT_HEADS * HEAD_DIM
    kv_w = KV_HEADS * HEAD_DIM
    qi_w = IDX_HEADS * IDX_DIM
    ssd_w = SSD_HEADS * SSD_HEAD_DIM
    xbc_w = ssd_w + 2 * SSD_GROUPS * SSD_STATE
    sizes = (att_w, kv_w, kv_w, qi_w, IDX_DIM, IDX_HEADS, ssd_w, xbc_w, SSD_HEADS)
    offs = [0]
    for sz in sizes:
        offs.append(offs[-1] + sz)
    tt = 128
    xf = x.reshape(bsz * seq, d)
    for l in range(depth):
        col = lambda a, b: w_in[l][:, offs[a]:offs[b]].astype(BF16)
        zpad = lambda n: jnp.zeros((d, n), BF16)
        dt_w = 2 * LANES
        z_off, qi_off = xbc_w, xbc_w + ssd_w
        dt_off = qi_off + qi_w
        kw_off = dt_off + dt_w
        w_rest = jnp.concatenate([col(7, 8), col(6, 7), col(3, 4), col(8, 9), zpad(dt_w - SSD_HEADS),
                                  col(4, 6), zpad(LANES - IDX_DIM - IDX_HEADS), zpad(LANES)], axis=1)
        qkv, xbf = _matmul(xf, col(0, 3), BF16, 512, 512)
        rest = _matmul(xbf, w_rest, F32, 1024, 256)
        att = _dsa_attention(qkv, rest, qi_off, kw_off, idx_kn_g[l], idx_kn_b[l], bsz, seq)
        ssd = _ssd_mixer(rest, z_off, dt_off, dt_w, conv_w[l], conv_b[l], dt_bias[l], a_log[l], d_skip[l],
                         ssd_norm_g[l], bsz, seq)
        mixed = _matmul_pair(att, ssd, w_out[l].astype(BF16), F32, 1024, 512)
        x1, x1p, route = _ln_router(xf, mixed, ln1_g[l], ln1_b[l], w_rg[l], b_rg[l], w_re[l], b_re[l], alpha)
        row_tok, blk_e, blk_valid, n_used, dest_tiles = _dispatch_plan(route, tt)
        y_rows = _moe_ffn(x1p, row_tok, blk_e, blk_valid, n_used, w_gate[l], w_up[l], w_down[l])
        xf = _combine_ln(y_rows, dest_tiles, x1, route, ln2_g[l], ln2_b[l], alpha, tt)
    return xf.reshape(bsz, seq, d)
```

```python
import functools

import jax
import jax.numpy as jnp
from jax import lax
from jax.experimental import pallas as pl
from jax.experimental.pallas import tpu as pltpu

F32 = jnp.float32
BF16 = jnp.bfloat16
I32 = jnp.int32

HEAD_DIM = 128
KV_HEADS = 4
GQA_GROUP = 4
ATT_HEADS = KV_HEADS * GQA_GROUP
IDX_HEADS = 16
IDX_DIM = 64
DSA_TOPK_MAX = 256
QUERY_BLOCK = 128
SSD_HEAD_DIM = 64
SSD_GROUPS = 8
SSD_HEADS_PER_GROUP = 4
SSD_HEADS = SSD_GROUPS * SSD_HEADS_PER_GROUP
SSD_STATE = 128
SSD_CONV = 4
SSD_CHUNK = 128
N_EXPERT_GROUPS = 8
EXPERTS_PER_GROUP = 8
N_EXPERTS = 64
LN_EPS = 1e-5
RMS_EPS = 1e-5

LANES = 128
SUBLANES = 8
VMEM_LIMIT = 56 * 1024 * 1024

KEY_CHUNK = 1024
MOE_ROWS = 768
MOE_SUB = 256
MOE_F_CHUNK = 256
MOE_N_CHUNK = 256
MOE_K_SPLIT = 4
MOE_RING = 3
MOE_LOOKAHEAD = 2
WEIGHT_DMA_PRIORITY = 1
DMA_UNROLL = 8
PLAN_TILE = 512
MM_ROWS_F32 = 512
MM_ROWS = 1024
MM_COLS = 512
MM_COLS_REST = 256
LN_ROWS = 256
COMBINE_TOKENS = 128
CONV_COLS = 512
NEG_BIG = -1e30
LOG2_E = 1.4426950408889634
INT_MIN = -2 ** 31
NEG_INF_KEY = -2139095041


def _params(sem):
    return pltpu.CompilerParams(dimension_semantics=sem, vmem_limit_bytes=VMEM_LIMIT)


def _mm_cast_kernel(a_ref, b_ref, o_ref, abf_ref):
    @pl.when(pl.program_id(1) == 0)
    def _():
        abf_ref[...] = a_ref[...].astype(BF16)

    o_ref[...] = jnp.dot(abf_ref[...], b_ref[...], preferred_element_type=F32).astype(o_ref.dtype)


def _mm_kernel(a_ref, b_ref, o_ref):
    o_ref[...] = jnp.dot(a_ref[...], b_ref[...], preferred_element_type=F32).astype(o_ref.dtype)


def _mm_pair_kernel(a1_ref, a2_ref, b_ref, o_ref):
    k1 = a1_ref.shape[1]
    acc = jnp.dot(a1_ref[...], b_ref[0:k1, :], preferred_element_type=F32)
    acc = acc + jnp.dot(a2_ref[...], b_ref[k1:, :], preferred_element_type=F32)
    o_ref[...] = acc.astype(o_ref.dtype)


def _matmul_pair(a1, a2, b, out_dtype, tm, tn):
    m, k1 = a1.shape
    k2 = a2.shape[1]
    n = b.shape[1]
    tm = min(tm, m)
    assert m % tm == 0 and n % tn == 0 and b.shape[0] == k1 + k2
    return pl.pallas_call(
        _mm_pair_kernel,
        grid=(m // tm, n // tn),
        in_specs=[pl.BlockSpec((tm, k1), lambda i, j: (i, 0)),
                  pl.BlockSpec((tm, k2), lambda i, j: (i, 0)),
                  pl.BlockSpec((k1 + k2, tn), lambda i, j: (0, j))],
        out_specs=pl.BlockSpec((tm, tn), lambda i, j: (i, j)),
        out_shape=jax.ShapeDtypeStruct((m, n), out_dtype),
        compiler_params=_params(("parallel", "arbitrary")),
        name="matmul_pair",
    )(a1, a2, b)


def _matmul(a, b, out_dtype, tm, tn):
    m, k = a.shape
    n = b.shape[1]
    tm = min(tm, m)
    assert m % tm == 0 and n % tn == 0
    cast = a.dtype != BF16
    out_specs = pl.BlockSpec((tm, tn), lambda i, j: (i, j))
    out_shape = jax.ShapeDtypeStruct((m, n), out_dtype)
    if cast:
        out_specs = [out_specs, pl.BlockSpec((tm, k), lambda i, j: (i, 0))]
        out_shape = [out_shape, jax.ShapeDtypeStruct((m, k), BF16)]
    return pl.pallas_call(
        _mm_cast_kernel if cast else _mm_kernel,
        grid=(m // tm, n // tn),
        in_specs=[pl.BlockSpec((tm, k), lambda i, j: (i, 0)),
                  pl.BlockSpec((k, tn), lambda i, j: (0, j))],
        out_specs=out_specs,
        out_shape=out_shape,
        compiler_params=_params(("parallel", "arbitrary")),
        name="matmul_cast" if cast else "matmul",
    )(a, b)


def _attn_kernel(q_ref, k_ref, v_ref, qi_ref, kw_ref, g_ref, b_ref, o_ref,
                 kln_ref, key_ref, bias_ref, s_ref, mrun_ref, lrun_ref, acc_ref, *, top_k):
    i = pl.program_id(1)
    tq = QUERY_BLOCK
    ck = KEY_CHUNK

    @pl.when(i == 0)
    def _():
        kx = kw_ref[:, 0:IDX_DIM]
        mu = jnp.mean(kx, axis=-1, keepdims=True)
        var = jnp.mean(jnp.square(kx - mu), axis=-1, keepdims=True)
        y = (kx - mu) * lax.rsqrt(var + LN_EPS)
        kln_ref[...] = (y * g_ref[...] + b_ref[...]).astype(BF16)

    q_start = i * tq
    n_chunks = (q_start + tq + ck - 1) // ck
    q_pos = q_start + lax.broadcasted_iota(I32, (tq, 1), 0)
    w = kw_ref[pl.ds(pl.multiple_of(q_start, tq), tq), IDX_DIM:IDX_DIM + IDX_HEADS]
    w = w * (IDX_HEADS ** -0.5 * IDX_DIM ** -0.5)
    qi = jnp.concatenate([qi_ref[:, h * IDX_DIM:(h + 1) * IDX_DIM] for h in range(IDX_HEADS)], axis=0).astype(BF16)

    def chunk_off(c):
        return pl.multiple_of(c * ck, ck)

    def key_pos(c):
        return c * ck + lax.broadcasted_iota(I32, (1, ck), 1)

    def score_chunk(c, carry):
        off = chunk_off(c)
        kc = kln_ref[pl.ds(off, ck), :]
        d = lax.dot_general(qi, kc, (((1,), (1,)), ((), ())), preferred_element_type=F32)
        acc = jnp.zeros((tq, ck), F32)
        for h in range(IDX_HEADS):
            acc = acc + jnp.maximum(d[h * tq:(h + 1) * tq, :], 0.0) * w[:, h:h + 1]
        acc = jnp.where(key_pos(c) <= q_pos, acc, -jnp.inf)
        bits = pltpu.bitcast(acc, I32)
        key_ref[:, pl.ds(off, ck)] = bits ^ ((bits >> 31) & 0x7FFFFFFF)
        return carry

    lax.fori_loop(0, n_chunks, score_chunk, 0)

    def bit_body(b, carry):
        cand, cnt_cand = carry
        trial = cand | lax.shift_left(jnp.int32(1), jnp.int32(31) - jnp.asarray(b, I32))
        trial_b = jnp.broadcast_to(trial ^ INT_MIN, (tq, LANES))

        def cnt_chunk(c, cnt):
            kc = key_ref[:, pl.ds(chunk_off(c), ck)]
            for s in range(ck // LANES):
                cnt = cnt + jnp.where(kc[:, s * LANES:(s + 1) * LANES] >= trial_b, 1.0, 0.0)
            return cnt

        cnt = lax.fori_loop(0, n_chunks, cnt_chunk, jnp.zeros((tq, LANES), F32))
        total = jnp.sum(cnt, axis=1, keepdims=True)
        ok = total >= float(top_k)
        return jnp.where(ok, trial, cand), jnp.where(ok, total, cnt_cand)

    n_keys = (jnp.zeros((tq, 1), I32) + n_chunks * ck).astype(F32)
    cand, cnt_ge = lax.fori_loop(0, 32, bit_body, (jnp.zeros((tq, 1), I32), n_keys))
    thr = cand ^ INT_MIN

    def bias_chunk(c, carry):
        off = chunk_off(c)
        sel = (key_ref[:, pl.ds(off, ck)] >= thr) & (key_pos(c) <= q_pos)
        bias_ref[:, pl.ds(off, ck)] = jnp.where(sel, 0.0, NEG_BIG)
        return carry

    lax.fori_loop(0, n_chunks, bias_chunk, 0)

    tie = (cnt_ge > float(top_k)) & (thr > NEG_INF_KEY)

    @pl.when(jnp.max(jnp.where(tie, 1.0, 0.0)) > 0.5)
    def _():
        tri = (lax.broadcasted_iota(I32, (ck, ck), 0) <= lax.broadcasted_iota(I32, (ck, ck), 1)).astype(BF16)

        def gt_chunk(c, cnt):
            kc = key_ref[:, pl.ds(chunk_off(c), ck)]
            return cnt + jnp.sum(jnp.where(kc > thr, 1.0, 0.0), axis=1, keepdims=True)

        need = float(top_k) - lax.fori_loop(0, n_chunks, gt_chunk, jnp.zeros((tq, 1), F32))

        def tie_chunk(c, seen):
            off = chunk_off(c)
            kc = key_ref[:, pl.ds(off, ck)]
            eq = jnp.where(kc == thr, 1.0, 0.0)
            rank = seen + jnp.dot(eq.astype(BF16), tri, preferred_element_type=F32)
            keep = (kc > thr) | ((kc == thr) & ((rank <= need) | jnp.logical_not(tie)))
            sel = keep & (key_pos(c) <= q_pos)
            bias_ref[:, pl.ds(off, ck)] = jnp.where(sel, 0.0, NEG_BIG)
            return seen + jnp.sum(eq, axis=1, keepdims=True)

        lax.fori_loop(0, n_chunks, tie_chunk, jnp.zeros((tq, 1), F32))

    scale = HEAD_DIM ** -0.5 * LOG2_E
    gq = GQA_GROUP
    for g in range(KV_HEADS):
        qg = jnp.concatenate([q_ref[:, (g * gq + j) * HEAD_DIM:(g * gq + j + 1) * HEAD_DIM] for j in range(gq)],
                             axis=0)

        mrun_ref[...] = jnp.full(mrun_ref.shape, NEG_BIG, F32)

        def logit_chunk(c, carry, g=g, qg=qg):
            off = chunk_off(c)
            kc = k_ref[pl.ds(off, ck), g * HEAD_DIM:(g + 1) * HEAD_DIM]
            s = lax.dot_general(qg, kc, (((1,), (1,)), ((), ())), preferred_element_type=F32)
            bias = bias_ref[:, pl.ds(off, ck)]
            s = s * scale + jnp.concatenate([bias] * gq, axis=0)
            s_ref[:, pl.ds(off, ck)] = s
            m = mrun_ref[...]
            for t in range(ck // LANES):
                m = jnp.maximum(m, s[:, t * LANES:(t + 1) * LANES])
            mrun_ref[...] = m
            return carry

        lax.fori_loop(0, n_chunks, logit_chunk, 0)
        m = jnp.max(mrun_ref[...], axis=1, keepdims=True)

        lrun_ref[...] = jnp.zeros(lrun_ref.shape, F32)
        acc_ref[...] = jnp.zeros(acc_ref.shape, F32)

        def prob_chunk(c, carry, g=g, m=m):
            off = chunk_off(c)
            vc = v_ref[pl.ds(off, ck), g * HEAD_DIM:(g + 1) * HEAD_DIM]
            p = jnp.exp2(s_ref[:, pl.ds(off, ck)] - m)
            l = lrun_ref[...]
            for t in range(ck // LANES):
                l = l + p[:, t * LANES:(t + 1) * LANES]
            lrun_ref[...] = l
            acc_ref[...] += jnp.dot(p.astype(BF16), vc, preferred_element_type=F32)
            return carry

        lax.fori_loop(0, n_chunks, prob_chunk, 0)
        out = acc_ref[...] / jnp.sum(lrun_ref[...], axis=1, keepdims=True)
        for j in range(gq):
            h = g * gq + j
            o_ref[:, h * HEAD_DIM:(h + 1) * HEAD_DIM] = out[j * tq:(j + 1) * tq, :].astype(o_ref.dtype)


def _dsa_attention(qkv, idx, qi_off, kw_off, kn_g, kn_b, bsz, seq):
    top_k = min(DSA_TOPK_MAX, seq // 4)
    nq = seq // QUERY_BLOCK
    att_w = ATT_HEADS * HEAD_DIM
    kv_w = KV_HEADS * HEAD_DIM
    qi_w = IDX_HEADS * IDX_DIM
    assert seq % KEY_CHUNK == 0 and att_w % kv_w == 0 and qi_off % qi_w == 0 and kw_off % LANES == 0
    return pl.pallas_call(
        functools.partial(_attn_kernel, top_k=top_k),
        grid=(bsz, nq),
        in_specs=[pl.BlockSpec((QUERY_BLOCK, att_w), lambda b, i: (b * nq + i, 0)),
                  pl.BlockSpec((seq, kv_w), lambda b, i: (b, att_w // kv_w)),
                  pl.BlockSpec((seq, kv_w), lambda b, i: (b, att_w // kv_w + 1)),
                  pl.BlockSpec((QUERY_BLOCK, qi_w), lambda b, i: (b * nq + i, qi_off // qi_w)),
                  pl.BlockSpec((seq, LANES), lambda b, i: (b, kw_off // LANES)),
                  pl.BlockSpec((1, IDX_DIM), lambda b, i: (0, 0)),
                  pl.BlockSpec((1, IDX_DIM), lambda b, i: (0, 0))],
        out_specs=pl.BlockSpec((QUERY_BLOCK, att_w), lambda b, i: (b * nq + i, 0)),
        out_shape=jax.ShapeDtypeStruct((bsz * seq, att_w), BF16),
        scratch_shapes=[pltpu.VMEM((seq, IDX_DIM), BF16),
                        pltpu.VMEM((QUERY_BLOCK, seq), I32),
                        pltpu.VMEM((QUERY_BLOCK, seq), F32),
                        pltpu.VMEM((GQA_GROUP * QUERY_BLOCK, seq), F32),
                        pltpu.VMEM((GQA_GROUP * QUERY_BLOCK, LANES), F32),
                        pltpu.VMEM((GQA_GROUP * QUERY_BLOCK, LANES), F32),
                        pltpu.VMEM((GQA_GROUP * QUERY_BLOCK, HEAD_DIM), F32)],
        compiler_params=_params(("parallel", "arbitrary")),
        name="dsa_attention",
    )(qkv, qkv, qkv, idx, idx, kn_g.reshape(1, IDX_DIM), kn_b.reshape(1, IDX_DIM))


def _silu(x):
    return x / (1.0 + jnp.exp(-x))


def _ssd_kernel(xbc_ref, z_ref, dt_ref, cw_ref, cb_ref, dtb_ref, alog_ref, dsk_ref, ng_ref, o_ref,
                xpad_ref, act_ref, *h_refs):
    c = pl.program_id(1)
    cq = SSD_CHUNK
    width = SSD_HEADS * SSD_HEAD_DIM
    b_off = width
    c_off = width + SSD_GROUPS * SSD_STATE

    @pl.when(c == 0)
    def _():
        xpad_ref[0:SUBLANES, :] = jnp.zeros((SUBLANES, xpad_ref.shape[1]), F32)
        for h_ref in h_refs:
            h_ref[...] = jnp.zeros(h_ref.shape, F32)

    xpad_ref[SUBLANES:SUBLANES + cq, :] = xbc_ref[...]
    col = CONV_COLS
    for j in range(xpad_ref.shape[1] // col):
        cs = slice(j * col, (j + 1) * col)
        acc = cb_ref[:, cs] + jnp.zeros((cq, col), F32)
        for t in range(SSD_CONV):
            r0 = SUBLANES - (SSD_CONV - 1) + t
            acc = acc + xpad_ref[r0:r0 + cq, cs] * cw_ref[t:t + 1, cs]
        act_ref[:, cs] = _silu(acc)
    xpad_ref[0:SUBLANES, :] = xpad_ref[cq:cq + SUBLANES, :]

    xdt_in = dt_ref[:, 0:LANES] + dtb_ref[...]
    dt = jnp.maximum(xdt_in, 0.0) + jnp.log1p(jnp.exp(-jnp.abs(xdt_in)))
    da = dt * (-jnp.exp(alog_ref[...]))
    row = lax.broadcasted_iota(I32, (cq, cq), 0)
    coli = lax.broadcasted_iota(I32, (cq, cq), 1)
    causal = row >= coli
    tril = causal.astype(F32)
    acs = jnp.dot(tril, da, preferred_element_type=F32, precision=lax.Precision.HIGHEST)
    acs_t = acs.T
    dec_in = jnp.exp(acs)
    a_last = acs[cq - 1:cq, :]
    dec_out = jnp.exp(a_last - acs)
    dec_chunk = jnp.exp(a_last)

    gw = width // SSD_GROUPS
    for g in range(SSD_GROUPS):
        h_ref = h_refs[g]
        bg = act_ref[:, b_off + g * SSD_STATE:b_off + (g + 1) * SSD_STATE].astype(BF16)
        cg = act_ref[:, c_off + g * SSD_STATE:c_off + (g + 1) * SSD_STATE].astype(BF16)
        cbm = lax.dot_general(cg, bg, (((1,), (1,)), ((), ())), preferred_element_type=F32)
        hprevs = [h_ref[j] for j in range(SSD_HEADS_PER_GROUP)]
        ys, hnews = [], []
        for j in range(SSD_HEADS_PER_GROUP):
            hd = g * SSD_HEADS_PER_GROUP + j
            xs = slice(hd * SSD_HEAD_DIM, (hd + 1) * SSD_HEAD_DIM)
            seg = acs[:, hd:hd + 1] - acs_t[hd:hd + 1, :]
            lmat = jnp.exp(jnp.where(causal, seg, -jnp.inf))
            xh = act_ref[:, xs]
            xdt = xh * dt[:, hd:hd + 1]
            y = jnp.dot((cbm * lmat).astype(BF16), xdt.astype(BF16), preferred_element_type=F32)
            yoff = lax.dot_general(cg, hprevs[j].astype(BF16), (((1,), (1,)), ((), ())),
                                   preferred_element_type=F32)
            ys.append(y + yoff * dec_in[:, hd:hd + 1] + xh * dsk_ref[:, xs])
            st = lax.dot_general((xdt * dec_out[:, hd:hd + 1]).astype(BF16), bg,
                                 (((0,), (0,)), ((), ())), preferred_element_type=F32)
            hnews.append(hprevs[j] * dec_chunk[:, hd:hd + 1] + st)
        gs = slice(g * gw, (g + 1) * gw)
        gated = jnp.concatenate(ys, axis=1) * _silu(z_ref[:, gs])
        ms = jnp.mean(jnp.square(gated), axis=-1, keepdims=True)
        o_ref[:, gs] = (gated * lax.rsqrt(ms + RMS_EPS) * ng_ref[:, gs]).astype(o_ref.dtype)
        for j in range(SSD_HEADS_PER_GROUP):
            h_ref[j] = hnews[j]


def _ssd_mixer(ssd_in, z_off, dt_off, dt_w, conv_w, conv_b, dt_bias, a_log, d_skip, norm_g, bsz, seq):
    nc = seq // SSD_CHUNK
    width = SSD_HEADS * SSD_HEAD_DIM
    xbc_w = width + 2 * SSD_GROUPS * SSD_STATE
    assert z_off % width == 0 and dt_off % dt_w == 0 and dt_w >= LANES
    pad = LANES - SSD_HEADS
    dtb = jnp.pad(dt_bias.astype(F32), (0, pad)).reshape(1, LANES)
    alog = jnp.pad(a_log.astype(F32), (0, pad)).reshape(1, LANES)
    dsk = jnp.repeat(d_skip.astype(F32), SSD_HEAD_DIM).reshape(1, width)
    const = lambda b, c: (0, 0)
    return pl.pallas_call(
        _ssd_kernel,
        grid=(bsz, nc),
        in_specs=[pl.BlockSpec((SSD_CHUNK, xbc_w), lambda b, c: (b * nc + c, 0)),
                  pl.BlockSpec((SSD_CHUNK, width), lambda b, c: (b * nc + c, z_off // width)),
                  pl.BlockSpec((SSD_CHUNK, dt_w), lambda b, c: (b * nc + c, dt_off // dt_w)),
                  pl.BlockSpec((SSD_CONV, xbc_w), const),
                  pl.BlockSpec((1, xbc_w), const),
                  pl.BlockSpec((1, LANES), const),
                  pl.BlockSpec((1, LANES), const),
                  pl.BlockSpec((1, width), const),
                  pl.BlockSpec((1, width), const)],
        out_specs=pl.BlockSpec((SSD_CHUNK, width), lambda b, c: (b * nc + c, 0)),
        out_shape=jax.ShapeDtypeStruct((bsz * seq, width), BF16),
        scratch_shapes=[pltpu.VMEM((SSD_CHUNK + SUBLANES, xbc_w), F32),
                        pltpu.VMEM((SSD_CHUNK, xbc_w), F32)]
        + [pltpu.VMEM((SSD_HEADS_PER_GROUP, SSD_HEAD_DIM, SSD_STATE), F32) for _ in range(SSD_GROUPS)],
        compiler_params=_params(("parallel", "arbitrary")),
        name="ssd_mixer",
    )(ssd_in, ssd_in, ssd_in, conv_w.astype(F32), conv_b.astype(F32).reshape(1, xbc_w), dtb, alog, dsk,
      norm_g.astype(F32).reshape(1, width))


def _layer_norm_rows(x, g, b):
    mu = jnp.mean(x, axis=-1, keepdims=True)
    var = jnp.mean(jnp.square(x - mu), axis=-1, keepdims=True)
    return (x - mu) * lax.rsqrt(var + LN_EPS) * g + b


def _ln_router_kernel(x_ref, mix_ref, g_ref, b_ref, wr_ref, br_ref, o_ref, xp_ref, r_ref, *, alpha):
    x1 = _layer_norm_rows(alpha * x_ref[...] + mix_ref[...], g_ref[...], b_ref[...])
    o_ref[...] = x1
    half = xp_ref.shape[1]
    bits = pltpu.bitcast(x1.astype(BF16).astype(F32), jnp.uint32)
    xp_ref[...] = (bits[:, :half] >> 16) | bits[:, half:]
    logits = jnp.dot(x1, wr_ref[...], preferred_element_type=F32, precision=lax.Precision.HIGHEST)
    logits = logits + br_ref[...]
    rows = logits.shape[0]
    lane = lax.broadcasted_iota(I32, (rows, LANES), 1).astype(F32)
    ng, epg = float(N_EXPERT_GROUPS), float(EXPERTS_PER_GROUP)
    far = float(LANES)

    gmask = lane < ng
    gl = jnp.where(gmask, logits, -jnp.inf)
    ge = jnp.exp(gl - jnp.max(gl, axis=1, keepdims=True))
    gprob = ge / jnp.sum(ge, axis=1, keepdims=True)
    gprob = jnp.where(gmask, gprob, -1.0)
    gw = jnp.max(gprob, axis=1, keepdims=True)
    gsel = jnp.min(jnp.where(gprob == gw, lane, far), axis=1, keepdims=True)

    e_lo = ng + gsel * epg
    emask = (lane >= e_lo) & (lane < e_lo + epg)
    el = jnp.where(emask, logits, -jnp.inf)
    v0 = jnp.max(el, axis=1, keepdims=True)
    i0 = jnp.min(jnp.where(emask & (el == v0), lane, far), axis=1, keepdims=True)
    emask1 = emask & (lane != i0)
    el1 = jnp.where(emask1, logits, -jnp.inf)
    v1 = jnp.max(el1, axis=1, keepdims=True)
    i1 = jnp.min(jnp.where(emask1 & (el1 == v1), lane, far), axis=1, keepdims=True)
    e = jnp.exp(v1 - v0)
    p0 = 1.0 / (1.0 + e)
    p1 = e / (1.0 + e)
    out = jnp.where(lane == 0.0, i0 - ng,
                    jnp.where(lane == 1.0, i1 - ng,
                              jnp.where(lane == 2.0, gw * p0, jnp.where(lane == 3.0, gw * p1, 0.0))))
    r_ref[...] = out


def _ln_router(x, mixed, g, b, w_rg, b_rg, w_re, b_re, alpha, tr=LN_ROWS):
    t, d = x.shape
    ncol = N_EXPERT_GROUPS + N_EXPERTS
    wr = jnp.pad(jnp.concatenate([w_rg, w_re], axis=1).astype(F32), ((0, 0), (0, LANES - ncol)))
    br = jnp.pad(jnp.concatenate([b_rg, b_re]).astype(F32), (0, LANES - ncol)).reshape(1, LANES)
    const = lambda i: (0, 0)
    return pl.pallas_call(
        functools.partial(_ln_router_kernel, alpha=alpha),
        grid=(t // tr,),
        in_specs=[pl.BlockSpec((tr, d), lambda i: (i, 0)),
                  pl.BlockSpec((tr, d), lambda i: (i, 0)),
                  pl.BlockSpec((1, d), const), pl.BlockSpec((1, d), const),
                  pl.BlockSpec((d, LANES), const), pl.BlockSpec((1, LANES), const)],
        out_specs=[pl.BlockSpec((tr, d), lambda i: (i, 0)),
                   pl.BlockSpec((tr, d // 2), lambda i: (i, 0)),
                   pl.BlockSpec((tr, LANES), lambda i: (i, 0))],
        out_shape=[jax.ShapeDtypeStruct((t, d), F32), jax.ShapeDtypeStruct((t, d // 2), jnp.uint32),
                   jax.ShapeDtypeStruct((t, LANES), F32)],
        compiler_params=_params(("parallel",)),
        name="ln1_router",
    )(x, mixed, g.astype(F32).reshape(1, d), b.astype(F32).reshape(1, d), wr, br)


def _row_copy(src_ref, dst_ref, sem, tok, row):
    return pltpu.make_async_copy(src_ref.at[pl.ds(tok, 1)], dst_ref.at[pl.ds(row, 1)], sem)


def _moe_kernel(tok_ref, be_ref, bv_ref, nu_ref, x_ref, wg_ref, wu_ref, wd_ref, o_ref,
                xq_ref, xbf_ref, act_ref, gbuf_ref, dbuf_ref, sem, gsem, dsem, *, nf, nn):
    s = pl.program_id(0)
    p = pl.program_id(1)
    n_blk = pl.num_programs(0)
    n_used = nu_ref[0]
    steps = nf + nn
    sub = MOE_SUB
    n_sub = MOE_ROWS // sub
    half = xq_ref.shape[1]
    ks = MOE_K_SPLIT
    kq = wg_ref.shape[1] // ks

    def weights_dma(blk, ph, start):
        e = be_ref[blk]

        @pl.when(ph < nf)
        def _():
            slot = lax.rem(blk * nf + ph, MOE_RING)
            col = pl.multiple_of(ph * MOE_F_CHUNK, MOE_F_CHUNK)
            for q in range(ks):
                rows = pl.ds(q * kq, kq)
                for j, w_ref in enumerate((wg_ref, wu_ref)):
                    cp = pltpu.make_async_copy(w_ref.at[e, rows, pl.ds(col, MOE_F_CHUNK)],
                                               gbuf_ref.at[slot, j, rows], gsem.at[slot])
                    cp.start(priority=WEIGHT_DMA_PRIORITY) if start else cp.wait()

        @pl.when(ph >= nf)
        def _():
            k = ph - nf
            slot = lax.rem(blk * nn + k, MOE_RING)
            for j in range(2):
                col = pl.multiple_of((k + j * nn) * MOE_N_CHUNK, MOE_N_CHUNK)
                cp = pltpu.make_async_copy(wd_ref.at[e, :, pl.ds(col, MOE_N_CHUNK)],
                                           dbuf_ref.at[slot, j], dsem.at[slot])
                cp.start(priority=WEIGHT_DMA_PRIORITY) if start else cp.wait()

    @pl.when((s == 0) & (p == 0))
    def _():
        for a in range(MOE_LOOKAHEAD):
            weights_dma(s, p + a, True)

    ahead = p + MOE_LOOKAHEAD
    blk_a = jnp.where(ahead >= steps, s + 1, s)
    ph_a = jnp.where(ahead >= steps, ahead - steps, ahead)

    @pl.when(blk_a < n_used)
    def _():
        weights_dma(blk_a, ph_a, True)

    @pl.when(s < n_used)
    def _():
        weights_dma(s, p, False)

    def nsub_of(blk):
        return (bv_ref[blk] + (sub - 1)) // sub

    def issue(blk, r):
        base = blk * MOE_ROWS + r * sub

        def body(i, carry):
            for u in range(DMA_UNROLL):
                j = i * DMA_UNROLL + u
                _row_copy(x_ref, xq_ref, sem, tok_ref[base + j], r * sub + j).start()
            return carry
        lax.fori_loop(0, sub // DMA_UNROLL, body, 0)

    nsub = nsub_of(s)

    @pl.when((s == 0) & (p == 0))
    def _():
        for r in range(n_sub):
            @pl.when(r < nsub)
            def _(r=r):
                issue(s, r)

    @pl.when(p == 0)
    def _():
        for r in range(n_sub):
            @pl.when(r < nsub)
            def _(r=r):
                pltpu.make_async_copy(x_ref.at[pl.ds(0, sub)], xq_ref.at[pl.ds(r * sub, sub)], sem).wait()
        for r in range(n_sub):
            @pl.when(r < nsub)
            def _(r=r):
                rs = slice(r * sub, (r + 1) * sub)
                word = xq_ref[rs, :]
                xbf_ref[rs, 0:half] = pltpu.bitcast(word << 16, F32).astype(BF16)
                xbf_ref[rs, half:2 * half] = pltpu.bitcast(word & jnp.uint32(0xFFFF0000), F32).astype(BF16)

    nxt = jnp.minimum(s + 1, n_blk - 1)
    for r in range(n_sub):
        @pl.when((p == r + 1) & (s + 1 < n_blk) & (r < nsub_of(nxt)))
        def _(r=r):
            issue(nxt, r)

    for k in range(1, n_sub + 1):
        m = k * sub

        @pl.when((p < nf) & (nsub == k))
        def _(m=m):
            slot = lax.rem(s * nf + p, MOE_RING)
            gate = up = None
            for q in range(ks):
                xr = xbf_ref[0:m, q * kq:(q + 1) * kq]
                gq = jnp.dot(xr, gbuf_ref[slot, 0, q * kq:(q + 1) * kq, :].astype(BF16), preferred_element_type=F32)
                uq = jnp.dot(xr, gbuf_ref[slot, 1, q * kq:(q + 1) * kq, :].astype(BF16), preferred_element_type=F32)
                gate = gq if gate is None else gate + gq
                up = uq if up is None else up + uq
            act_ref[p, 0:m, :] = (_silu(gate) * up).astype(BF16)

    def down(slot, j, m):
        acc = None
        for q in range(nf):
            wq = dbuf_ref[slot, j, q * MOE_F_CHUNK:(q + 1) * MOE_F_CHUNK, :].astype(BF16)
            t = jnp.dot(act_ref[q, 0:m, :], wq, preferred_element_type=F32)
            acc = t if acc is None else acc + t
        return pltpu.bitcast(acc.astype(BF16).astype(F32), jnp.uint32)

    for k in range(0, n_sub + 1):
        m = k * sub

        @pl.when((p >= nf) & (nsub == k))
        def _(m=m):
            if m > 0:
                slot = lax.rem(s * nn + p - nf, MOE_RING)
                o_ref[0:m, :] = (down(slot, 0, m) >> 16) | down(slot, 1, m)
            if m < MOE_ROWS:
                o_ref[m:MOE_ROWS, :] = jnp.zeros((MOE_ROWS - m, o_ref.shape[1]), jnp.uint32)


def _moe_ffn(xp, row_tok, blk_e, blk_valid, n_used, w_gate, w_up, w_down):
    n_rows = row_tok.shape[0]
    n_e, d, f = w_gate.shape
    nf = f // MOE_F_CHUNK
    nn = (d // 2) // MOE_N_CHUNK
    n_blk = n_rows // MOE_ROWS
    assert nf + nn > MOE_ROWS // MOE_SUB and d % MOE_K_SPLIT == 0
    assert MOE_LOOKAHEAD < MOE_RING and MOE_LOOKAHEAD <= min(nf, nn)

    def out_map(s, p, tok, be, bv, nu):
        return (s, jnp.maximum(p - nf, 0))

    hbm = pl.BlockSpec(memory_space=pl.ANY)
    return pl.pallas_call(
        functools.partial(_moe_kernel, nf=nf, nn=nn),
        grid_spec=pltpu.PrefetchScalarGridSpec(
            num_scalar_prefetch=4,
            grid=(n_blk, nf + nn),
            in_specs=[hbm, hbm, hbm, hbm],
            out_specs=pl.BlockSpec((MOE_ROWS, MOE_N_CHUNK), out_map),
            scratch_shapes=[pltpu.VMEM((MOE_ROWS, d // 2), jnp.uint32),
                            pltpu.VMEM((MOE_ROWS, d), BF16),
                            pltpu.VMEM((nf, MOE_ROWS, MOE_F_CHUNK), BF16),
                            pltpu.VMEM((MOE_RING, 2, d, MOE_F_CHUNK), F32),
                            pltpu.VMEM((MOE_RING, 2, f, MOE_N_CHUNK), F32),
                            pltpu.SemaphoreType.DMA(()),
                            pltpu.SemaphoreType.DMA((MOE_RING,)),
                            pltpu.SemaphoreType.DMA((MOE_RING,))]),
        out_shape=jax.ShapeDtypeStruct((n_rows, d // 2), jnp.uint32),
        compiler_params=_params(("arbitrary", "arbitrary")),
        name="moe_ffn",
    )(row_tok, blk_e, blk_valid, n_used, xp, w_gate, w_up, w_down)


def _combine_kernel(dest_ref, y_ref, x_ref, r_ref, g_ref, b_ref, o_ref, buf_ref, sem, *, alpha, tt):
    s = pl.program_id(0)
    n = pl.num_programs(0)

    rows = 2 * tt

    def start(blk, slot):
        base = blk * rows

        def body(i, carry):
            for u in range(DMA_UNROLL):
                r = i * DMA_UNROLL + u
                pltpu.make_async_copy(y_ref.at[pl.ds(dest_ref[base + r], 1)],
                                      buf_ref.at[slot, pl.ds(r, 1)], sem.at[slot]).start()
            return carry
        lax.fori_loop(0, rows // DMA_UNROLL, body, 0)

    @pl.when(s == 0)
    def _():
        start(s, 0)

    for par in range(2):
        @pl.when((s % 2 == par) & (s + 1 < n))
        def _(par=par):
            start(s + 1, 1 - par)

    slot = s % 2
    pltpu.make_async_copy(y_ref.at[pl.ds(0, rows)], buf_ref.at[slot], sem.at[slot]).wait()
    g0 = r_ref[:, 2:3]
    g1 = r_ref[:, 3:4]
    w0 = buf_ref[slot, 0:tt, :]
    w1 = buf_ref[slot, tt:2 * tt, :]
    half = w0.shape[1]
    himask = jnp.uint32(0xFFFF0000)
    y_lo = g0 * pltpu.bitcast(w0 << 16, F32) + g1 * pltpu.bitcast(w1 << 16, F32)
    y_hi = g0 * pltpu.bitcast(w0 & himask, F32) + g1 * pltpu.bitcast(w1 & himask, F32)
    r_lo = alpha * x_ref[:, 0:half] + y_lo
    r_hi = alpha * x_ref[:, half:2 * half] + y_hi
    inv_d = 1.0 / (2 * half)
    mu = (jnp.sum(r_lo, axis=-1, keepdims=True) + jnp.sum(r_hi, axis=-1, keepdims=True)) * inv_d
    var = (jnp.sum(jnp.square(r_lo - mu), axis=-1, keepdims=True)
           + jnp.sum(jnp.square(r_hi - mu), axis=-1, keepdims=True)) * inv_d
    rstd = lax.rsqrt(var + LN_EPS)
    o_ref[:, 0:half] = (r_lo - mu) * rstd * g_ref[:, 0:half] + b_ref[:, 0:half]
    o_ref[:, half:2 * half] = (r_hi - mu) * rstd * g_ref[:, half:2 * half] + b_ref[:, half:2 * half]


def _combine_ln(y_rows, dest, x1, route, g, b, alpha, tt):
    t, d = x1.shape
    const = lambda i, dr: (0, 0)
    return pl.pallas_call(
        functools.partial(_combine_kernel, alpha=alpha, tt=tt),
        grid_spec=pltpu.PrefetchScalarGridSpec(
            num_scalar_prefetch=1,
            grid=(t // tt,),
            in_specs=[pl.BlockSpec(memory_space=pl.ANY),
                      pl.BlockSpec((tt, d), lambda i, dr: (i, 0)),
                      pl.BlockSpec((tt, LANES), lambda i, dr: (i, 0)),
                      pl.BlockSpec((1, d), const), pl.BlockSpec((1, d), const)],
            out_specs=pl.BlockSpec((tt, d), lambda i, dr: (i, 0)),
            scratch_shapes=[pltpu.VMEM((2, 2 * tt, d // 2), jnp.uint32),
                            pltpu.SemaphoreType.DMA((2,))]),
        out_shape=jax.ShapeDtypeStruct((t, d), F32),
        compiler_params=_params(("arbitrary",)),
        name="moe_combine_ln2",
    )(dest, y_rows, x1, route, g.astype(F32).reshape(1, d), b.astype(F32).reshape(1, d))


def _plan_kernel(r_ref, dest_ref, meta_ref, cnt_ref, pstart_ref, carry_ref):
    ph = pl.program_id(0)
    i = pl.program_id(1)
    ts = r_ref.shape[0]
    rows = float(MOE_ROWS)
    lane = lax.broadcasted_iota(I32, (ts, LANES), 1).astype(F32)
    oh0 = jnp.where(lane == r_ref[:, 0:1], 1.0, 0.0)
    oh1 = jnp.where(lane == r_ref[:, 1:2], 1.0, 0.0)
    oh = oh0 + oh1

    @pl.when((ph == 0) & (i == 0))
    def _():
        cnt_ref[...] = jnp.zeros(cnt_ref.shape, F32)

    @pl.when(ph == 0)
    def _():
        cnt_ref[...] += jnp.sum(oh, axis=0, keepdims=True)

    @pl.when((ph == 0) & (i == pl.num_programs(1) - 1))
    def _():
        sq = (LANES, LANES)
        r_i = lax.broadcasted_iota(I32, sq, 0)
        c_i = lax.broadcasted_iota(I32, sq, 1)
        counts = jnp.broadcast_to(cnt_ref[...], sq)
        nblk_e = jnp.floor((counts + (rows - 0.5)) * (1.0 / rows))
        upper = jnp.where(r_i <= c_i, 1.0, 0.0)
        pend = jnp.dot(nblk_e.astype(BF16), upper.astype(BF16), preferred_element_type=F32)
        pstart = pend - nblk_e
        n_used = jnp.max(pend, axis=1, keepdims=True)
        pstart_ref[...] = pstart[0:1, :] * rows
        carry_ref[...] = jnp.zeros(carry_ref.shape, F32)
        b_eff = jnp.minimum(c_i.astype(F32), n_used - 1.0)
        pend_t, pstart_t, counts_t = pend.T, pstart.T, counts.T
        blk_e = jnp.sum(jnp.where(pend_t <= b_eff, 1.0, 0.0), axis=0, keepdims=True)
        blk_e = jnp.minimum(blk_e, float(N_EXPERTS - 1))
        pick = r_i.astype(F32) == blk_e
        cnt_b = jnp.sum(jnp.where(pick, counts_t, 0.0), axis=0, keepdims=True)
        first_b = jnp.sum(jnp.where(pick, pstart_t, 0.0), axis=0, keepdims=True)
        valid = jnp.clip(cnt_b - (b_eff[0:1, :] - first_b) * rows, 0.0, rows)
        valid = jnp.where(c_i[0:1, :].astype(F32) < n_used[0:1, :], valid, 0.0)
        sub_i = lax.broadcasted_iota(I32, meta_ref.shape, 0)
        meta = jnp.where(sub_i == 0, blk_e, jnp.where(sub_i == 1, valid, jnp.where(sub_i == 2, n_used[0:1, :], 0.0)))
        meta_ref[...] = meta.astype(I32)

    @pl.when(ph == 1)
    def _():
        below = lax.broadcasted_iota(I32, (ts, ts), 0) > lax.broadcasted_iota(I32, (ts, ts), 1)
        earlier = jnp.dot(below.astype(BF16), oh.astype(BF16), preferred_element_type=F32)
        base = pstart_ref[...] + carry_ref[...] + earlier
        d0 = jnp.sum(base * oh0, axis=1, keepdims=True)
        d1 = jnp.sum(base * oh1, axis=1, keepdims=True)
        carry_ref[...] += jnp.sum(oh, axis=0, keepdims=True)
        both = jnp.where(lane == 0.0, d0, jnp.where(lane == 1.0, d1, 0.0))
        dest_ref[...] = both.T[0:2, :].astype(I32)


def _dispatch_plan(route, tt):
    t = route.shape[0]
    m = 2 * t
    n_blk = -(-m // MOE_ROWS) + N_EXPERTS
    n_rows = n_blk * MOE_ROWS
    ts = min(PLAN_TILE, t)
    assert n_blk <= LANES and t % ts == 0 and N_EXPERTS <= LANES
    dest2, meta = pl.pallas_call(
        _plan_kernel,
        grid=(2, t // ts),
        in_specs=[pl.BlockSpec((ts, LANES), lambda ph, i: (i, 0))],
        out_specs=[pl.BlockSpec((2, ts), lambda ph, i: (0, i * ph)),
                   pl.BlockSpec((SUBLANES, LANES), lambda ph, i: (0, 0))],
        out_shape=[jax.ShapeDtypeStruct((2, t), I32), jax.ShapeDtypeStruct((SUBLANES, LANES), I32)],
        scratch_shapes=[pltpu.VMEM((1, LANES), F32), pltpu.VMEM((1, LANES), F32), pltpu.VMEM((1, LANES), F32)],
        compiler_params=_params(("arbitrary", "arbitrary")),
        name="moe_plan",
    )(route)
    tok = jnp.arange(t, dtype=I32)
    row_tok = jnp.zeros((n_rows,), I32).at[dest2.reshape(m)].set(jnp.concatenate([tok, tok]))
    dest_tiles = dest2.reshape(2, t // tt, tt).transpose(1, 0, 2).reshape(m)
    return row_tok, meta[0, :n_blk], meta[1, :n_blk], meta[2, 0:1], dest_tiles


def kernel(x, w_in, idx_kn_g, idx_kn_b, conv_w, conv_b, dt_bias, a_log, d_skip, ssd_norm_g, w_out,
           ln1_g, ln1_b, w_rg, b_rg, w_re, b_re, w_gate, w_up, w_down, ln2_g, ln2_b):
    bsz, seq, d = x.shape
    depth = w_in.shape[0]
    alpha = (2 * depth) ** 0.25
    att_w = ATT_HEADS * HEAD_DIM
    kv_w = KV_HEADS * HEAD_DIM
    qi_w = IDX_HEADS * IDX_DIM
    ssd_w = SSD_HEADS * SSD_HEAD_DIM
    xbc_w = ssd_w + 2 * SSD_GROUPS * SSD_STATE
    sizes = (att_w, kv_w, kv_w, qi_w, IDX_DIM, IDX_HEADS, ssd_w, xbc_w, SSD_HEADS)
    offs = [0]
    for sz in sizes:
        offs.append(offs[-1] + sz)
    tt = COMBINE_TOKENS
    xf = x.reshape(bsz * seq, d)
    for l in range(depth):
        col = lambda a, b: w_in[l][:, offs[a]:offs[b]].astype(BF16)
        zpad = lambda n: jnp.zeros((d, n), BF16)
        dt_w = 2 * LANES
        z_off, qi_off = xbc_w, xbc_w + ssd_w
        dt_off = qi_off + qi_w
        kw_off = dt_off + dt_w
        w_rest = jnp.concatenate([col(7, 8), col(6, 7), col(3, 4), col(8, 9), zpad(dt_w - SSD_HEADS),
                                  col(4, 6), zpad(LANES - IDX_DIM - IDX_HEADS), zpad(LANES)], axis=1)
        qkv, xbf = _matmul(xf, col(0, 3), BF16, MM_ROWS_F32, MM_COLS)
        rest = _matmul(xbf, w_rest, F32, MM_ROWS, MM_COLS_REST)
        att = _dsa_attention(qkv, rest, qi_off, kw_off, idx_kn_g[l], idx_kn_b[l], bsz, seq)
        ssd = _ssd_mixer(rest, z_off, dt_off, dt_w, conv_w[l], conv_b[l], dt_bias[l], a_log[l], d_skip[l],
                         ssd_norm_g[l], bsz, seq)
        mixed = _matmul_pair(att, ssd, w_out[l].astype(BF16), F32, MM_ROWS, MM_COLS)
        x1, x1p, route = _ln_router(xf, mixed, ln1_g[l], ln1_b[l], w_rg[l], b_rg[l], w_re[l], b_re[l], alpha)
        row_tok, blk_e, blk_valid, n_used, dest_tiles = _dispatch_plan(route, tt)
        y_rows = _moe_ffn(x1p, row_tok, blk_e, blk_valid, n_used, w_gate[l], w_up[l], w_down[l])
        xf = _combine_ln(y_rows, dest_tiles, x1, route, ln2_g[l], ln2_b[l], alpha, tt)
    return xf.reshape(bsz, seq, d)
```

```python
import functools

import jax
import jax.numpy as jnp
from jax import lax
from jax.experimental import pallas as pl
from jax.experimental.pallas import tpu as pltpu

F32 = jnp.float32
BF16 = jnp.bfloat16
I32 = jnp.int32

HEAD_DIM = 128
KV_HEADS = 4
GQA_GROUP = 4
ATT_HEADS = KV_HEADS * GQA_GROUP
IDX_HEADS = 16
IDX_DIM = 64
DSA_TOPK_MAX = 256
QUERY_BLOCK = 128
SSD_HEAD_DIM = 64
SSD_GROUPS = 8
SSD_HEADS_PER_GROUP = 4
SSD_HEADS = SSD_GROUPS * SSD_HEADS_PER_GROUP
SSD_STATE = 128
SSD_CONV = 4
SSD_CHUNK = 128
N_EXPERT_GROUPS = 8
EXPERTS_PER_GROUP = 8
N_EXPERTS = 64
LN_EPS = 1e-5
RMS_EPS = 1e-5

LANES = 128
SUBLANES = 8
VMEM_LIMIT = 56 * 1024 * 1024

KEY_CHUNK = 1024
MOE_ROWS = 768
MOE_SUB = 256
MOE_K_CHUNK = 512
MOE_N_CHUNK = 256
MOE_K_SPLIT = 2
MOE_RING = 3
MOE_LOOKAHEAD = 2
WEIGHT_DMA_PRIORITY = 1
DMA_UNROLL = 8
PLAN_TILE = 512
MM_ROWS_F32 = 512
MM_ROWS = 1024
MM_COLS = 512
MM_COLS_REST = 256
LN_ROWS = 256
COMBINE_TOKENS = 128
CONV_COLS = 512
NEG_BIG = -1e30
LOG2_E = 1.4426950408889634
INT_MIN = -2 ** 31
NEG_INF_KEY = -2139095041


def _params(sem):
    return pltpu.CompilerParams(dimension_semantics=sem, vmem_limit_bytes=VMEM_LIMIT)


def _mm_cast_kernel(a_ref, b_ref, o_ref, abf_ref):
    @pl.when(pl.program_id(1) == 0)
    def _():
        abf_ref[...] = a_ref[...].astype(BF16)

    o_ref[...] = jnp.dot(abf_ref[...], b_ref[...], preferred_element_type=F32).astype(o_ref.dtype)


def _mm_kernel(a_ref, b_ref, o_ref):
    o_ref[...] = jnp.dot(a_ref[...], b_ref[...], preferred_element_type=F32).astype(o_ref.dtype)


def _mm_pair_kernel(a1_ref, a2_ref, b_ref, o_ref):
    k1 = a1_ref.shape[1]
    acc = jnp.dot(a1_ref[...], b_ref[0:k1, :], preferred_element_type=F32)
    acc = acc + jnp.dot(a2_ref[...], b_ref[k1:, :], preferred_element_type=F32)
    o_ref[...] = acc.astype(o_ref.dtype)


def _matmul_pair(a1, a2, b, out_dtype, tm, tn):
    m, k1 = a1.shape
    k2 = a2.shape[1]
    n = b.shape[1]
    tm = min(tm, m)
    assert m % tm == 0 and n % tn == 0 and b.shape[0] == k1 + k2
    return pl.pallas_call(
        _mm_pair_kernel,
        grid=(m // tm, n // tn),
        in_specs=[pl.BlockSpec((tm, k1), lambda i, j: (i, 0)),
                  pl.BlockSpec((tm, k2), lambda i, j: (i, 0)),
                  pl.BlockSpec((k1 + k2, tn), lambda i, j: (0, j))],
        out_specs=pl.BlockSpec((tm, tn), lambda i, j: (i, j)),
        out_shape=jax.ShapeDtypeStruct((m, n), out_dtype),
        compiler_params=_params(("parallel", "arbitrary")),
        name="matmul_pair",
    )(a1, a2, b)


def _matmul(a, b, out_dtype, tm, tn):
    m, k = a.shape
    n = b.shape[1]
    tm = min(tm, m)
    assert m % tm == 0 and n % tn == 0
    cast = a.dtype != BF16
    out_specs = pl.BlockSpec((tm, tn), lambda i, j: (i, j))
    out_shape = jax.ShapeDtypeStruct((m, n), out_dtype)
    if cast:
        out_specs = [out_specs, pl.BlockSpec((tm, k), lambda i, j: (i, 0))]
        out_shape = [out_shape, jax.ShapeDtypeStruct((m, k), BF16)]
    return pl.pallas_call(
        _mm_cast_kernel if cast else _mm_kernel,
        grid=(m // tm, n // tn),
        in_specs=[pl.BlockSpec((tm, k), lambda i, j: (i, 0)),
                  pl.BlockSpec((k, tn), lambda i, j: (0, j))],
        out_specs=out_specs,
        out_shape=out_shape,
        compiler_params=_params(("parallel", "arbitrary")),
        name="matmul_cast" if cast else "matmul",
    )(a, b)


def _attn_kernel(q_ref, k_ref, v_ref, qi_ref, kw_ref, g_ref, b_ref, o_ref,
                 kln_ref, key_ref, bias_ref, s_ref, mrun_ref, lrun_ref, acc_ref, *, top_k):
    i = pl.program_id(1)
    tq = QUERY_BLOCK
    ck = KEY_CHUNK

    @pl.when(i == 0)
    def _():
        kx = kw_ref[:, 0:IDX_DIM]
        mu = jnp.mean(kx, axis=-1, keepdims=True)
        var = jnp.mean(jnp.square(kx - mu), axis=-1, keepdims=True)
        y = (kx - mu) * lax.rsqrt(var + LN_EPS)
        kln_ref[...] = (y * g_ref[...] + b_ref[...]).astype(BF16)

    q_start = i * tq
    n_chunks = (q_start + tq + ck - 1) // ck
    q_pos = q_start + lax.broadcasted_iota(I32, (tq, 1), 0)
    w = kw_ref[pl.ds(pl.multiple_of(q_start, tq), tq), IDX_DIM:IDX_DIM + IDX_HEADS]
    w = w * (IDX_HEADS ** -0.5 * IDX_DIM ** -0.5)
    qi = jnp.concatenate([qi_ref[:, h * IDX_DIM:(h + 1) * IDX_DIM] for h in range(IDX_HEADS)], axis=0).astype(BF16)

    def chunk_off(c):
        return pl.multiple_of(c * ck, ck)

    def key_pos(c):
        return c * ck + lax.broadcasted_iota(I32, (1, ck), 1)

    def score_chunk(c, carry):
        off = chunk_off(c)
        kc = kln_ref[pl.ds(off, ck), :]
        d = lax.dot_general(qi, kc, (((1,), (1,)), ((), ())), preferred_element_type=F32)
        acc = jnp.zeros((tq, ck), F32)
        for h in range(IDX_HEADS):
            acc = acc + jnp.maximum(d[h * tq:(h + 1) * tq, :], 0.0) * w[:, h:h + 1]
        acc = jnp.where(key_pos(c) <= q_pos, acc, -jnp.inf)
        bits = pltpu.bitcast(acc, I32)
        key_ref[:, pl.ds(off, ck)] = bits ^ ((bits >> 31) & 0x7FFFFFFF)
        return carry

    lax.fori_loop(0, n_chunks, score_chunk, 0)

    def bit_body(b, carry):
        cand, cnt_cand = carry
        trial = cand | lax.shift_left(jnp.int32(1), jnp.int32(31) - jnp.asarray(b, I32))
        trial_b = jnp.broadcast_to(trial ^ INT_MIN, (tq, LANES))

        def cnt_chunk(c, cnt):
            kc = key_ref[:, pl.ds(chunk_off(c), ck)]
            for s in range(ck // LANES):
                cnt = cnt + jnp.where(kc[:, s * LANES:(s + 1) * LANES] >= trial_b, 1.0, 0.0)
            return cnt

        cnt = lax.fori_loop(0, n_chunks, cnt_chunk, jnp.zeros((tq, LANES), F32))
        total = jnp.sum(cnt, axis=1, keepdims=True)
        ok = total >= float(top_k)
        return jnp.where(ok, trial, cand), jnp.where(ok, total, cnt_cand)

    n_keys = (jnp.zeros((tq, 1), I32) + n_chunks * ck).astype(F32)
    cand, cnt_ge = lax.fori_loop(0, 32, bit_body, (jnp.zeros((tq, 1), I32), n_keys))
    thr = cand ^ INT_MIN

    def bias_chunk(c, carry):
        off = chunk_off(c)
        sel = (key_ref[:, pl.ds(off, ck)] >= thr) & (key_pos(c) <= q_pos)
        bias_ref[:, pl.ds(off, ck)] = jnp.where(sel, 0.0, NEG_BIG)
        return carry

    lax.fori_loop(0, n_chunks, bias_chunk, 0)

    tie = (cnt_ge > float(top_k)) & (thr > NEG_INF_KEY)

    @pl.when(jnp.max(jnp.where(tie, 1.0, 0.0)) > 0.5)
    def _():
        tri = (lax.broadcasted_iota(I32, (ck, ck), 0) <= lax.broadcasted_iota(I32, (ck, ck), 1)).astype(BF16)

        def gt_chunk(c, cnt):
            kc = key_ref[:, pl.ds(chunk_off(c), ck)]
            return cnt + jnp.sum(jnp.where(kc > thr, 1.0, 0.0), axis=1, keepdims=True)

        need = float(top_k) - lax.fori_loop(0, n_chunks, gt_chunk, jnp.zeros((tq, 1), F32))

        def tie_chunk(c, seen):
            off = chunk_off(c)
            kc = key_ref[:, pl.ds(off, ck)]
            eq = jnp.where(kc == thr, 1.0, 0.0)
            rank = seen + jnp.dot(eq.astype(BF16), tri, preferred_element_type=F32)
            keep = (kc > thr) | ((kc == thr) & ((rank <= need) | jnp.logical_not(tie)))
            sel = keep & (key_pos(c) <= q_pos)
            bias_ref[:, pl.ds(off, ck)] = jnp.where(sel, 0.0, NEG_BIG)
            return seen + jnp.sum(eq, axis=1, keepdims=True)

        lax.fori_loop(0, n_chunks, tie_chunk, jnp.zeros((tq, 1), F32))

    scale = HEAD_DIM ** -0.5 * LOG2_E
    gq = GQA_GROUP
    for g in range(KV_HEADS):
        qg = jnp.concatenate([q_ref[:, (g * gq + j) * HEAD_DIM:(g * gq + j + 1) * HEAD_DIM] for j in range(gq)],
                             axis=0)

        mrun_ref[...] = jnp.full(mrun_ref.shape, NEG_BIG, F32)

        def logit_chunk(c, carry, g=g, qg=qg):
            off = chunk_off(c)
            kc = k_ref[pl.ds(off, ck), g * HEAD_DIM:(g + 1) * HEAD_DIM]
            s = lax.dot_general(qg, kc, (((1,), (1,)), ((), ())), preferred_element_type=F32)
            bias = bias_ref[:, pl.ds(off, ck)]
            s = s * scale + jnp.concatenate([bias] * gq, axis=0)
            s_ref[:, pl.ds(off, ck)] = s
            m = mrun_ref[...]
            for t in range(ck // LANES):
                m = jnp.maximum(m, s[:, t * LANES:(t + 1) * LANES])
            mrun_ref[...] = m
            return carry

        lax.fori_loop(0, n_chunks, logit_chunk, 0)
        m = jnp.max(mrun_ref[...], axis=1, keepdims=True)

        lrun_ref[...] = jnp.zeros(lrun_ref.shape, F32)
        acc_ref[...] = jnp.zeros(acc_ref.shape, F32)

        def prob_chunk(c, carry, g=g, m=m):
            off = chunk_off(c)
            vc = v_ref[pl.ds(off, ck), g * HEAD_DIM:(g + 1) * HEAD_DIM]
            p = jnp.exp2(s_ref[:, pl.ds(off, ck)] - m)
            l = lrun_ref[...]
            for t in range(ck // LANES):
                l = l + p[:, t * LANES:(t + 1) * LANES]
            lrun_ref[...] = l
            acc_ref[...] += jnp.dot(p.astype(BF16), vc, preferred_element_type=F32)
            return carry

        lax.fori_loop(0, n_chunks, prob_chunk, 0)
        out = acc_ref[...] / jnp.sum(lrun_ref[...], axis=1, keepdims=True)
        for j in range(gq):
            h = g * gq + j
            o_ref[:, h * HEAD_DIM:(h + 1) * HEAD_DIM] = out[j * tq:(j + 1) * tq, :].astype(o_ref.dtype)


def _dsa_attention(qkv, idx, qi_off, kw_off, kn_g, kn_b, bsz, seq):
    top_k = min(DSA_TOPK_MAX, seq // 4)
    nq = seq // QUERY_BLOCK
    att_w = ATT_HEADS * HEAD_DIM
    kv_w = KV_HEADS * HEAD_DIM
    qi_w = IDX_HEADS * IDX_DIM
    assert seq % KEY_CHUNK == 0 and att_w % kv_w == 0 and qi_off % qi_w == 0 and kw_off % LANES == 0
    return pl.pallas_call(
        functools.partial(_attn_kernel, top_k=top_k),
        grid=(bsz, nq),
        in_specs=[pl.BlockSpec((QUERY_BLOCK, att_w), lambda b, i: (b * nq + i, 0)),
                  pl.BlockSpec((seq, kv_w), lambda b, i: (b, att_w // kv_w)),
                  pl.BlockSpec((seq, kv_w), lambda b, i: (b, att_w // kv_w + 1)),
                  pl.BlockSpec((QUERY_BLOCK, qi_w), lambda b, i: (b * nq + i, qi_off // qi_w)),
                  pl.BlockSpec((seq, LANES), lambda b, i: (b, kw_off // LANES)),
                  pl.BlockSpec((1, IDX_DIM), lambda b, i: (0, 0)),
                  pl.BlockSpec((1, IDX_DIM), lambda b, i: (0, 0))],
        out_specs=pl.BlockSpec((QUERY_BLOCK, att_w), lambda b, i: (b * nq + i, 0)),
        out_shape=jax.ShapeDtypeStruct((bsz * seq, att_w), BF16),
        scratch_shapes=[pltpu.VMEM((seq, IDX_DIM), BF16),
                        pltpu.VMEM((QUERY_BLOCK, seq), I32),
                        pltpu.VMEM((QUERY_BLOCK, seq), F32),
                        pltpu.VMEM((GQA_GROUP * QUERY_BLOCK, seq), F32),
                        pltpu.VMEM((GQA_GROUP * QUERY_BLOCK, LANES), F32),
                        pltpu.VMEM((GQA_GROUP * QUERY_BLOCK, LANES), F32),
                        pltpu.VMEM((GQA_GROUP * QUERY_BLOCK, HEAD_DIM), F32)],
        compiler_params=_params(("parallel", "arbitrary")),
        name="dsa_attention",
    )(qkv, qkv, qkv, idx, idx, kn_g.reshape(1, IDX_DIM), kn_b.reshape(1, IDX_DIM))


def _silu(x):
    return x / (1.0 + jnp.exp(-x))


def _ssd_kernel(xbc_ref, z_ref, dt_ref, cw_ref, cb_ref, dtb_ref, alog_ref, dsk_ref, ng_ref, o_ref,
                xpad_ref, act_ref, *h_refs):
    c = pl.program_id(1)
    cq = SSD_CHUNK
    width = SSD_HEADS * SSD_HEAD_DIM
    b_off = width
    c_off = width + SSD_GROUPS * SSD_STATE

    @pl.when(c == 0)
    def _():
        xpad_ref[0:SUBLANES, :] = jnp.zeros((SUBLANES, xpad_ref.shape[1]), F32)
        for h_ref in h_refs:
            h_ref[...] = jnp.zeros(h_ref.shape, F32)

    xpad_ref[SUBLANES:SUBLANES + cq, :] = xbc_ref[...]
    col = CONV_COLS
    for j in range(xpad_ref.shape[1] // col):
        cs = slice(j * col, (j + 1) * col)
        acc = cb_ref[:, cs] + jnp.zeros((cq, col), F32)
        for t in range(SSD_CONV):
            r0 = SUBLANES - (SSD_CONV - 1) + t
            acc = acc + xpad_ref[r0:r0 + cq, cs] * cw_ref[t:t + 1, cs]
        act_ref[:, cs] = _silu(acc)
    xpad_ref[0:SUBLANES, :] = xpad_ref[cq:cq + SUBLANES, :]

    xdt_in = dt_ref[:, 0:LANES] + dtb_ref[...]
    dt = jnp.maximum(xdt_in, 0.0) + jnp.log1p(jnp.exp(-jnp.abs(xdt_in)))
    da = dt * (-jnp.exp(alog_ref[...]))
    row = lax.broadcasted_iota(I32, (cq, cq), 0)
    coli = lax.broadcasted_iota(I32, (cq, cq), 1)
    causal = row >= coli
    tril = causal.astype(F32)
    acs = jnp.dot(tril, da, preferred_element_type=F32, precision=lax.Precision.HIGHEST)
    acs_t = acs.T
    dec_in = jnp.exp(acs)
    a_last = acs[cq - 1:cq, :]
    dec_out = jnp.exp(a_last - acs)
    dec_chunk = jnp.exp(a_last)

    gw = width // SSD_GROUPS
    for g in range(SSD_GROUPS):
        h_ref = h_refs[g]
        bg = act_ref[:, b_off + g * SSD_STATE:b_off + (g + 1) * SSD_STATE].astype(BF16)
        cg = act_ref[:, c_off + g * SSD_STATE:c_off + (g + 1) * SSD_STATE].astype(BF16)
        cbm = lax.dot_general(cg, bg, (((1,), (1,)), ((), ())), preferred_element_type=F32)
        hprevs = [h_ref[j] for j in range(SSD_HEADS_PER_GROUP)]
        ys, hnews = [], []
        for j in range(SSD_HEADS_PER_GROUP):
            hd = g * SSD_HEADS_PER_GROUP + j
            xs = slice(hd * SSD_HEAD_DIM, (hd + 1) * SSD_HEAD_DIM)
            seg = acs[:, hd:hd + 1] - acs_t[hd:hd + 1, :]
            lmat = jnp.exp(jnp.where(causal, seg, -jnp.inf))
            xh = act_ref[:, xs]
            xdt = xh * dt[:, hd:hd + 1]
            y = jnp.dot((cbm * lmat).astype(BF16), xdt.astype(BF16), preferred_element_type=F32)
            yoff = lax.dot_general(cg, hprevs[j].astype(BF16), (((1,), (1,)), ((), ())),
                                   preferred_element_type=F32)
            ys.append(y + yoff * dec_in[:, hd:hd + 1] + xh * dsk_ref[:, xs])
            st = lax.dot_general((xdt * dec_out[:, hd:hd + 1]).astype(BF16), bg,
                                 (((0,), (0,)), ((), ())), preferred_element_type=F32)
            hnews.append(hprevs[j] * dec_chunk[:, hd:hd + 1] + st)
        gs = slice(g * gw, (g + 1) * gw)
        gated = jnp.concatenate(ys, axis=1) * _silu(z_ref[:, gs])
        ms = jnp.mean(jnp.square(gated), axis=-1, keepdims=True)
        o_ref[:, gs] = (gated * lax.rsqrt(ms + RMS_EPS) * ng_ref[:, gs]).astype(o_ref.dtype)
        for j in range(SSD_HEADS_PER_GROUP):
            h_ref[j] = hnews[j]


def _ssd_mixer(ssd_in, z_off, dt_off, dt_w, conv_w, conv_b, dt_bias, a_log, d_skip, norm_g, bsz, seq):
    nc = seq // SSD_CHUNK
    width = SSD_HEADS * SSD_HEAD_DIM
    xbc_w = width + 2 * SSD_GROUPS * SSD_STATE
    assert z_off % width == 0 and dt_off % dt_w == 0 and dt_w >= LANES
    pad = LANES - SSD_HEADS
    dtb = jnp.pad(dt_bias.astype(F32), (0, pad)).reshape(1, LANES)
    alog = jnp.pad(a_log.astype(F32), (0, pad)).reshape(1, LANES)
    dsk = jnp.repeat(d_skip.astype(F32), SSD_HEAD_DIM).reshape(1, width)
    const = lambda b, c: (0, 0)
    return pl.pallas_call(
        _ssd_kernel,
        grid=(bsz, nc),
        in_specs=[pl.BlockSpec((SSD_CHUNK, xbc_w), lambda b, c: (b * nc + c, 0)),
                  pl.BlockSpec((SSD_CHUNK, width), lambda b, c: (b * nc + c, z_off // width)),
                  pl.BlockSpec((SSD_CHUNK, dt_w), lambda b, c: (b * nc + c, dt_off // dt_w)),
                  pl.BlockSpec((SSD_CONV, xbc_w), const),
                  pl.BlockSpec((1, xbc_w), const),
                  pl.BlockSpec((1, LANES), const),
                  pl.BlockSpec((1, LANES), const),
                  pl.BlockSpec((1, width), const),
                  pl.BlockSpec((1, width), const)],
        out_specs=pl.BlockSpec((SSD_CHUNK, width), lambda b, c: (b * nc + c, 0)),
        out_shape=jax.ShapeDtypeStruct((bsz * seq, width), BF16),
        scratch_shapes=[pltpu.VMEM((SSD_CHUNK + SUBLANES, xbc_w), F32),
                        pltpu.VMEM((SSD_CHUNK, xbc_w), F32)]
        + [pltpu.VMEM((SSD_HEADS_PER_GROUP, SSD_HEAD_DIM, SSD_STATE), F32) for _ in range(SSD_GROUPS)],
        compiler_params=_params(("parallel", "arbitrary")),
        name="ssd_mixer",
    )(ssd_in, ssd_in, ssd_in, conv_w.astype(F32), conv_b.astype(F32).reshape(1, xbc_w), dtb, alog, dsk,
      norm_g.astype(F32).reshape(1, width))


def _layer_norm_rows(x, g, b):
    mu = jnp.mean(x, axis=-1, keepdims=True)
    var = jnp.mean(jnp.square(x - mu), axis=-1, keepdims=True)
    return (x - mu) * lax.rsqrt(var + LN_EPS) * g + b


def _ln_router_kernel(x_ref, mix_ref, g_ref, b_ref, wr_ref, br_ref, o_ref, xp_ref, r_ref, *, alpha):
    x1 = _layer_norm_rows(alpha * x_ref[...] + mix_ref[...], g_ref[...], b_ref[...])
    o_ref[...] = x1
    half = xp_ref.shape[1]
    bits = pltpu.bitcast(x1.astype(BF16).astype(F32), jnp.uint32)
    xp_ref[...] = (bits[:, :half] >> 16) | bits[:, half:]
    logits = jnp.dot(x1, wr_ref[...], preferred_element_type=F32, precision=lax.Precision.HIGHEST)
    logits = logits + br_ref[...]
    rows = logits.shape[0]
    lane = lax.broadcasted_iota(I32, (rows, LANES), 1).astype(F32)
    ng, epg = float(N_EXPERT_GROUPS), float(EXPERTS_PER_GROUP)
    far = float(LANES)

    gmask = lane < ng
    gl = jnp.where(gmask, logits, -jnp.inf)
    ge = jnp.exp(gl - jnp.max(gl, axis=1, keepdims=True))
    gprob = ge / jnp.sum(ge, axis=1, keepdims=True)
    gprob = jnp.where(gmask, gprob, -1.0)
    gw = jnp.max(gprob, axis=1, keepdims=True)
    gsel = jnp.min(jnp.where(gprob == gw, lane, far), axis=1, keepdims=True)

    e_lo = ng + gsel * epg
    emask = (lane >= e_lo) & (lane < e_lo + epg)
    el = jnp.where(emask, logits, -jnp.inf)
    v0 = jnp.max(el, axis=1, keepdims=True)
    i0 = jnp.min(jnp.where(emask & (el == v0), lane, far), axis=1, keepdims=True)
    emask1 = emask & (lane != i0)
    el1 = jnp.where(emask1, logits, -jnp.inf)
    v1 = jnp.max(el1, axis=1, keepdims=True)
    i1 = jnp.min(jnp.where(emask1 & (el1 == v1), lane, far), axis=1, keepdims=True)
    e = jnp.exp(v1 - v0)
    p0 = 1.0 / (1.0 + e)
    p1 = e / (1.0 + e)
    out = jnp.where(lane == 0.0, i0 - ng,
                    jnp.where(lane == 1.0, i1 - ng,
                              jnp.where(lane == 2.0, gw * p0, jnp.where(lane == 3.0, gw * p1, 0.0))))
    r_ref[...] = out


def _ln_router(x, mixed, g, b, w_rg, b_rg, w_re, b_re, alpha, tr=LN_ROWS):
    t, d = x.shape
    ncol = N_EXPERT_GROUPS + N_EXPERTS
    wr = jnp.pad(jnp.concatenate([w_rg, w_re], axis=1).astype(F32), ((0, 0), (0, LANES - ncol)))
    br = jnp.pad(jnp.concatenate([b_rg, b_re]).astype(F32), (0, LANES - ncol)).reshape(1, LANES)
    const = lambda i: (0, 0)
    return pl.pallas_call(
        functools.partial(_ln_router_kernel, alpha=alpha),
        grid=(t // tr,),
        in_specs=[pl.BlockSpec((tr, d), lambda i: (i, 0)),
                  pl.BlockSpec((tr, d), lambda i: (i, 0)),
                  pl.BlockSpec((1, d), const), pl.BlockSpec((1, d), const),
                  pl.BlockSpec((d, LANES), const), pl.BlockSpec((1, LANES), const)],
        out_specs=[pl.BlockSpec((tr, d), lambda i: (i, 0)),
                   pl.BlockSpec((tr, d // 2), lambda i: (i, 0)),
                   pl.BlockSpec((tr, LANES), lambda i: (i, 0))],
        out_shape=[jax.ShapeDtypeStruct((t, d), F32), jax.ShapeDtypeStruct((t, d // 2), jnp.uint32),
                   jax.ShapeDtypeStruct((t, LANES), F32)],
        compiler_params=_params(("parallel",)),
        name="ln1_router",
    )(x, mixed, g.astype(F32).reshape(1, d), b.astype(F32).reshape(1, d), wr, br)


def _row_copy(src_ref, dst_ref, sem, tok, row):
    return pltpu.make_async_copy(src_ref.at[pl.ds(tok, 1)], dst_ref.at[pl.ds(row, 1)], sem)


def _moe_kernel(tok_ref, be_ref, bv_ref, nu_ref, x_ref, wg_ref, wu_ref, wd_ref, o_ref,
                xq_ref, xbf_ref, act_ref, gacc_ref, uacc_ref, gbuf_ref, dbuf_ref, sem, gsem, dsem, *, nf, nn):
    s = pl.program_id(0)
    p = pl.program_id(1)
    n_blk = pl.num_programs(0)
    n_used = nu_ref[0]
    steps = nf + nn
    sub = MOE_SUB
    n_sub = MOE_ROWS // sub
    half = xq_ref.shape[1]
    kc = MOE_K_CHUNK
    ks = MOE_K_SPLIT
    kp = kc // ks

    def weights_dma(blk, ph, start):
        e = be_ref[blk]

        @pl.when(ph < nf)
        def _():
            slot = lax.rem(blk * nf + ph, MOE_RING)
            for q in range(ks):
                rows = pl.ds(pl.multiple_of(ph * kc + q * kp, kp), kp)
                for j, w_ref in enumerate((wg_ref, wu_ref)):
                    cp = pltpu.make_async_copy(w_ref.at[e, rows], gbuf_ref.at[slot, j, pl.ds(q * kp, kp)],
                                               gsem.at[slot])
                    cp.start(priority=WEIGHT_DMA_PRIORITY) if start else cp.wait()

        @pl.when(ph >= nf)
        def _():
            k = ph - nf
            slot = lax.rem(blk * nn + k, MOE_RING)
            for j in range(2):
                col = pl.multiple_of((k + j * nn) * MOE_N_CHUNK, MOE_N_CHUNK)
                cp = pltpu.make_async_copy(wd_ref.at[e, :, pl.ds(col, MOE_N_CHUNK)],
                                           dbuf_ref.at[slot, j], dsem.at[slot])
                cp.start(priority=WEIGHT_DMA_PRIORITY) if start else cp.wait()

    @pl.when((s == 0) & (p == 0))
    def _():
        for a in range(MOE_LOOKAHEAD):
            weights_dma(s, p + a, True)

    ahead = p + MOE_LOOKAHEAD
    blk_a = jnp.where(ahead >= steps, s + 1, s)
    ph_a = jnp.where(ahead >= steps, ahead - steps, ahead)

    @pl.when(blk_a < n_used)
    def _():
        weights_dma(blk_a, ph_a, True)

    @pl.when(s < n_used)
    def _():
        weights_dma(s, p, False)

    def nsub_of(blk):
        return (bv_ref[blk] + (sub - 1)) // sub

    def issue(blk, r):
        base = blk * MOE_ROWS + r * sub

        def body(i, carry):
            for u in range(DMA_UNROLL):
                j = i * DMA_UNROLL + u
                _row_copy(x_ref, xq_ref, sem, tok_ref[base + j], r * sub + j).start()
            return carry
        lax.fori_loop(0, sub // DMA_UNROLL, body, 0)

    nsub = nsub_of(s)

    @pl.when((s == 0) & (p == 0))
    def _():
        for r in range(n_sub):
            @pl.when(r < nsub)
            def _(r=r):
                issue(s, r)

    @pl.when(p == 0)
    def _():
        for r in range(n_sub):
            @pl.when(r < nsub)
            def _(r=r):
                pltpu.make_async_copy(x_ref.at[pl.ds(0, sub)], xq_ref.at[pl.ds(r * sub, sub)], sem).wait()
        for r in range(n_sub):
            @pl.when(r < nsub)
            def _(r=r):
                rs = slice(r * sub, (r + 1) * sub)
                word = xq_ref[rs, :]
                lo = pltpu.bitcast(word << 16, F32).astype(BF16)
                hi = pltpu.bitcast(word & jnp.uint32(0xFFFF0000), F32).astype(BF16)
                for q in range(half // kc):
                    xbf_ref[q, rs, :] = lo[:, q * kc:(q + 1) * kc]
                    xbf_ref[half // kc + q, rs, :] = hi[:, q * kc:(q + 1) * kc]

    nxt = jnp.minimum(s + 1, n_blk - 1)
    for r in range(n_sub):
        @pl.when((p == r + 1) & (s + 1 < n_blk) & (r < nsub_of(nxt)))
        def _(r=r):
            issue(nxt, r)

    for k in range(1, n_sub + 1):
        m = k * sub

        @pl.when((p < nf) & (nsub == k))
        def _(m=m):
            slot = lax.rem(s * nf + p, MOE_RING)
            xr = xbf_ref[p, 0:m, :]
            gq = jnp.dot(xr, gbuf_ref[slot, 0].astype(BF16), preferred_element_type=F32)
            uq = jnp.dot(xr, gbuf_ref[slot, 1].astype(BF16), preferred_element_type=F32)

            @pl.when(p == 0)
            def _():
                gacc_ref[0:m, :] = gq
                uacc_ref[0:m, :] = uq

            @pl.when((p > 0) & (p < nf - 1))
            def _():
                gacc_ref[0:m, :] += gq
                uacc_ref[0:m, :] += uq

            @pl.when(p == nf - 1)
            def _():
                act_ref[0:m, :] = (_silu(gacc_ref[0:m, :] + gq) * (uacc_ref[0:m, :] + uq)).astype(BF16)

    def down(slot, j, m):
        acc = jnp.dot(act_ref[0:m, :], dbuf_ref[slot, j].astype(BF16), preferred_element_type=F32)
        return pltpu.bitcast(acc.astype(BF16).astype(F32), jnp.uint32)

    for k in range(0, n_sub + 1):
        m = k * sub

        @pl.when((p >= nf) & (nsub == k))
        def _(m=m):
            if m > 0:
                slot = lax.rem(s * nn + p - nf, MOE_RING)
                o_ref[0:m, :] = (down(slot, 0, m) >> 16) | down(slot, 1, m)
            if m < MOE_ROWS:
                o_ref[m:MOE_ROWS, :] = jnp.zeros((MOE_ROWS - m, o_ref.shape[1]), jnp.uint32)


def _moe_ffn(xp, row_tok, blk_e, blk_valid, n_used, w_gate, w_up, w_down):
    n_rows = row_tok.shape[0]
    n_e, d, f = w_gate.shape
    nf = d // MOE_K_CHUNK
    nn = (d // 2) // MOE_N_CHUNK
    n_blk = n_rows // MOE_ROWS
    assert nf + nn > MOE_ROWS // MOE_SUB and (d // 2) % MOE_K_CHUNK == 0 and nf > 2
    assert MOE_LOOKAHEAD < MOE_RING and MOE_LOOKAHEAD <= min(nf, nn)

    def out_map(s, p, tok, be, bv, nu):
        return (s, jnp.maximum(p - nf, 0))

    hbm = pl.BlockSpec(memory_space=pl.ANY)
    return pl.pallas_call(
        functools.partial(_moe_kernel, nf=nf, nn=nn),
        grid_spec=pltpu.PrefetchScalarGridSpec(
            num_scalar_prefetch=4,
            grid=(n_blk, nf + nn),
            in_specs=[hbm, hbm, hbm, hbm],
            out_specs=pl.BlockSpec((MOE_ROWS, MOE_N_CHUNK), out_map),
            scratch_shapes=[pltpu.VMEM((MOE_ROWS, d // 2), jnp.uint32),
                            pltpu.VMEM((nf, MOE_ROWS, MOE_K_CHUNK), BF16),
                            pltpu.VMEM((MOE_ROWS, f), BF16),
                            pltpu.VMEM((MOE_ROWS, f), F32),
                            pltpu.VMEM((MOE_ROWS, f), F32),
                            pltpu.VMEM((MOE_RING, 2, MOE_K_CHUNK, f), F32),
                            pltpu.VMEM((MOE_RING, 2, f, MOE_N_CHUNK), F32),
                            pltpu.SemaphoreType.DMA(()),
                            pltpu.SemaphoreType.DMA((MOE_RING,)),
                            pltpu.SemaphoreType.DMA((MOE_RING,))]),
        out_shape=jax.ShapeDtypeStruct((n_rows, d // 2), jnp.uint32),
        compiler_params=_params(("arbitrary", "arbitrary")),
        name="moe_ffn",
    )(row_tok, blk_e, blk_valid, n_used, xp, w_gate, w_up, w_down)


def _combine_kernel(dest_ref, y_ref, x_ref, r_ref, g_ref, b_ref, o_ref, buf_ref, sem, *, alpha, tt):
    s = pl.program_id(0)
    n = pl.num_programs(0)

    rows = 2 * tt

    def start(blk, slot):
        base = blk * rows

        def body(i, carry):
            for u in range(DMA_UNROLL):
                r = i * DMA_UNROLL + u
                pltpu.make_async_copy(y_ref.at[pl.ds(dest_ref[base + r], 1)],
                                      buf_ref.at[slot, pl.ds(r, 1)], sem.at[slot]).start()
            return carry
        lax.fori_loop(0, rows // DMA_UNROLL, body, 0)

    @pl.when(s == 0)
    def _():
        start(s, 0)

    for par in range(2):
        @pl.when((s % 2 == par) & (s + 1 < n))
        def _(par=par):
            start(s + 1, 1 - par)

    slot = s % 2
    pltpu.make_async_copy(y_ref.at[pl.ds(0, rows)], buf_ref.at[slot], sem.at[slot]).wait()
    g0 = r_ref[:, 2:3]
    g1 = r_ref[:, 3:4]
    w0 = buf_ref[slot, 0:tt, :]
    w1 = buf_ref[slot, tt:2 * tt, :]
    half = w0.shape[1]
    himask = jnp.uint32(0xFFFF0000)
    y_lo = g0 * pltpu.bitcast(w0 << 16, F32) + g1 * pltpu.bitcast(w1 << 16, F32)
    y_hi = g0 * pltpu.bitcast(w0 & himask, F32) + g1 * pltpu.bitcast(w1 & himask, F32)
    r_lo = alpha * x_ref[:, 0:half] + y_lo
    r_hi = alpha * x_ref[:, half:2 * half] + y_hi
    inv_d = 1.0 / (2 * half)
    mu = (jnp.sum(r_lo, axis=-1, keepdims=True) + jnp.sum(r_hi, axis=-1, keepdims=True)) * inv_d
    var = (jnp.sum(jnp.square(r_lo - mu), axis=-1, keepdims=True)
           + jnp.sum(jnp.square(r_hi - mu), axis=-1, keepdims=True)) * inv_d
    rstd = lax.rsqrt(var + LN_EPS)
    o_ref[:, 0:half] = (r_lo - mu) * rstd * g_ref[:, 0:half] + b_ref[:, 0:half]
    o_ref[:, half:2 * half] = (r_hi - mu) * rstd * g_ref[:, half:2 * half] + b_ref[:, half:2 * half]


def _combine_ln(y_rows, dest, x1, route, g, b, alpha, tt):
    t, d = x1.shape
    const = lambda i, dr: (0, 0)
    return pl.pallas_call(
        functools.partial(_combine_kernel, alpha=alpha, tt=tt),
        grid_spec=pltpu.PrefetchScalarGridSpec(
            num_scalar_prefetch=1,
            grid=(t // tt,),
            in_specs=[pl.BlockSpec(memory_space=pl.ANY),
                      pl.BlockSpec((tt, d), lambda i, dr: (i, 0)),
                      pl.BlockSpec((tt, LANES), lambda i, dr: (i, 0)),
                      pl.BlockSpec((1, d), const), pl.BlockSpec((1, d), const)],
            out_specs=pl.BlockSpec((tt, d), lambda i, dr: (i, 0)),
            scratch_shapes=[pltpu.VMEM((2, 2 * tt, d // 2), jnp.uint32),
                            pltpu.SemaphoreType.DMA((2,))]),
        out_shape=jax.ShapeDtypeStruct((t, d), F32),
        compiler_params=_params(("arbitrary",)),
        name="moe_combine_ln2",
    )(dest, y_rows, x1, route, g.astype(F32).reshape(1, d), b.astype(F32).reshape(1, d))


def _plan_kernel(r_ref, dest_ref, meta_ref, cnt_ref, pstart_ref, carry_ref):
    ph = pl.program_id(0)
    i = pl.program_id(1)
    ts = r_ref.shape[0]
    rows = float(MOE_ROWS)
    lane = lax.broadcasted_iota(I32, (ts, LANES), 1).astype(F32)
    oh0 = jnp.where(lane == r_ref[:, 0:1], 1.0, 0.0)
    oh1 = jnp.where(lane == r_ref[:, 1:2], 1.0, 0.0)
    oh = oh0 + oh1

    @pl.when((ph == 0) & (i == 0))
    def _():
        cnt_ref[...] = jnp.zeros(cnt_ref.shape, F32)

    @pl.when(ph == 0)
    def _():
        cnt_ref[...] += jnp.sum(oh, axis=0, keepdims=True)

    @pl.when((ph == 0) & (i == pl.num_programs(1) - 1))
    def _():
        sq = (LANES, LANES)
        r_i = lax.broadcasted_iota(I32, sq, 0)
        c_i = lax.broadcasted_iota(I32, sq, 1)
        counts = jnp.broadcast_to(cnt_ref[...], sq)
        nblk_e = jnp.floor((counts + (rows - 0.5)) * (1.0 / rows))
        upper = jnp.where(r_i <= c_i, 1.0, 0.0)
        pend = jnp.dot(nblk_e.astype(BF16), upper.astype(BF16), preferred_element_type=F32)
        pstart = pend - nblk_e
        n_used = jnp.max(pend, axis=1, keepdims=True)
        pstart_ref[...] = pstart[0:1, :] * rows
        carry_ref[...] = jnp.zeros(carry_ref.shape, F32)
        b_eff = jnp.minimum(c_i.astype(F32), n_used - 1.0)
        pend_t, pstart_t, counts_t = pend.T, pstart.T, counts.T
        blk_e = jnp.sum(jnp.where(pend_t <= b_eff, 1.0, 0.0), axis=0, keepdims=True)
        blk_e = jnp.minimum(blk_e, float(N_EXPERTS - 1))
        pick = r_i.astype(F32) == blk_e
        cnt_b = jnp.sum(jnp.where(pick, counts_t, 0.0), axis=0, keepdims=True)
        first_b = jnp.sum(jnp.where(pick, pstart_t, 0.0), axis=0, keepdims=True)
        valid = jnp.clip(cnt_b - (b_eff[0:1, :] - first_b) * rows, 0.0, rows)
        valid = jnp.where(c_i[0:1, :].astype(F32) < n_used[0:1, :], valid, 0.0)
        sub_i = lax.broadcasted_iota(I32, meta_ref.shape, 0)
        meta = jnp.where(sub_i == 0, blk_e, jnp.where(sub_i == 1, valid, jnp.where(sub_i == 2, n_used[0:1, :], 0.0)))
        meta_ref[...] = meta.astype(I32)

    @pl.when(ph == 1)
    def _():
        below = lax.broadcasted_iota(I32, (ts, ts), 0) > lax.broadcasted_iota(I32, (ts, ts), 1)
        earlier = jnp.dot(below.astype(BF16), oh.astype(BF16), preferred_element_type=F32)
        base = pstart_ref[...] + carry_ref[...] + earlier
        d0 = jnp.sum(base * oh0, axis=1, keepdims=True)
        d1 = jnp.sum(base * oh1, axis=1, keepdims=True)
        carry_ref[...] += jnp.sum(oh, axis=0, keepdims=True)
        both = jnp.where(lane == 0.0, d0, jnp.where(lane == 1.0, d1, 0.0))
        dest_ref[...] = both.T[0:2, :].astype(I32)


def _dispatch_plan(route, tt):
    t = route.shape[0]
    m = 2 * t
    n_blk = -(-m // MOE_ROWS) + N_EXPERTS
    n_rows = n_blk * MOE_ROWS
    ts = min(PLAN_TILE, t)
    assert n_blk <= LANES and t % ts == 0 and N_EXPERTS <= LANES
    dest2, meta = pl.pallas_call(
        _plan_kernel,
        grid=(2, t // ts),
        in_specs=[pl.BlockSpec((ts, LANES), lambda ph, i: (i, 0))],
        out_specs=[pl.BlockSpec((2, ts), lambda ph, i: (0, i * ph)),
                   pl.BlockSpec((SUBLANES, LANES), lambda ph, i: (0, 0))],
        out_shape=[jax.ShapeDtypeStruct((2, t), I32), jax.ShapeDtypeStruct((SUBLANES, LANES), I32)],
        scratch_shapes=[pltpu.VMEM((1, LANES), F32), pltpu.VMEM((1, LANES), F32), pltpu.VMEM((1, LANES), F32)],
        compiler_params=_params(("arbitrary", "arbitrary")),
        name="moe_plan",
    )(route)
    tok = jnp.arange(t, dtype=I32)
    row_tok = jnp.zeros((n_rows,), I32).at[dest2.reshape(m)].set(jnp.concatenate([tok, tok]))
    dest_tiles = dest2.reshape(2, t // tt, tt).transpose(1, 0, 2).reshape(m)
    return row_tok, meta[0, :n_blk], meta[1, :n_blk], meta[2, 0:1], dest_tiles


def kernel(x, w_in, idx_kn_g, idx_kn_b, conv_w, conv_b, dt_bias, a_log, d_skip, ssd_norm_g, w_out,
           ln1_g, ln1_b, w_rg, b_rg, w_re, b_re, w_gate, w_up, w_down, ln2_g, ln2_b):
    bsz, seq, d = x.shape
    depth = w_in.shape[0]
    alpha = (2 * depth) ** 0.25
    att_w = ATT_HEADS * HEAD_DIM
    kv_w = KV_HEADS * HEAD_DIM
    qi_w = IDX_HEADS * IDX_DIM
    ssd_w = SSD_HEADS * SSD_HEAD_DIM
    xbc_w = ssd_w + 2 * SSD_GROUPS * SSD_STATE
    sizes = (att_w, kv_w, kv_w, qi_w, IDX_DIM, IDX_HEADS, ssd_w, xbc_w, SSD_HEADS)
    offs = [0]
    for sz in sizes:
        offs.append(offs[-1] + sz)
    tt = COMBINE_TOKENS
    xf = x.reshape(bsz * seq, d)
    for l in range(depth):
        col = lambda a, b: w_in[l][:, offs[a]:offs[b]].astype(BF16)
        zpad = lambda n: jnp.zeros((d, n), BF16)
        dt_w = 2 * LANES
        z_off, qi_off = xbc_w, xbc_w + ssd_w
        dt_off = qi_off + qi_w
        kw_off = dt_off + dt_w
        w_rest = jnp.concatenate([col(7, 8), col(6, 7), col(3, 4), col(8, 9), zpad(dt_w - SSD_HEADS),
                                  col(4, 6), zpad(LANES - IDX_DIM - IDX_HEADS), zpad(LANES)], axis=1)
        qkv, xbf = _matmul(xf, col(0, 3), BF16, MM_ROWS_F32, MM_COLS)
        rest = _matmul(xbf, w_rest, F32, MM_ROWS, MM_COLS_REST)
        att = _dsa_attention(qkv, rest, qi_off, kw_off, idx_kn_g[l], idx_kn_b[l], bsz, seq)
        ssd = _ssd_mixer(rest, z_off, dt_off, dt_w, conv_w[l], conv_b[l], dt_bias[l], a_log[l], d_skip[l],
                         ssd_norm_g[l], bsz, seq)
        mixed = _matmul_pair(att, ssd, w_out[l].astype(BF16), F32, MM_ROWS, MM_COLS)
        x1, x1p, route = _ln_router(xf, mixed, ln1_g[l], ln1_b[l], w_rg[l], b_rg[l], w_re[l], b_re[l], alpha)
        row_tok, blk_e, blk_valid, n_used, dest_tiles = _dispatch_plan(route, tt)
        y_rows = _moe_ffn(x1p, row_tok, blk_e, blk_valid, n_used, w_gate[l], w_up[l], w_down[l])
        xf = _combine_ln(y_rows, dest_tiles, x1, route, ln2_g[l], ln2_b[l], alpha, tt)
    return xf.reshape(bsz, seq, d)
```

```python
import functools

import jax
import jax.numpy as jnp
from jax import lax
from jax.experimental import pallas as pl
from jax.experimental.pallas import tpu as pltpu

F32 = jnp.float32
BF16 = jnp.bfloat16
I32 = jnp.int32

HEAD_DIM = 128
KV_HEADS = 4
GQA_GROUP = 4
ATT_HEADS = KV_HEADS * GQA_GROUP
IDX_HEADS = 16
IDX_DIM = 64
DSA_TOPK_MAX = 256
QUERY_BLOCK = 128
SSD_HEAD_DIM = 64
SSD_GROUPS = 8
SSD_HEADS_PER_GROUP = 4
SSD_HEADS = SSD_GROUPS * SSD_HEADS_PER_GROUP
SSD_STATE = 128
SSD_CONV = 4
SSD_CHUNK = 128
N_EXPERT_GROUPS = 8
EXPERTS_PER_GROUP = 8
N_EXPERTS = 64
LN_EPS = 1e-5
RMS_EPS = 1e-5

LANES = 128
SUBLANES = 8
VMEM_LIMIT = 56 * 1024 * 1024

KEY_CHUNK = 1024
MOE_ROWS = 768
MOE_SUB = 256
MOE_F_CHUNK = 256
MOE_N_CHUNK = 512
MOE_K_SPLIT = 4
MOE_RING = 3
MOE_LOOKAHEAD = 2
WEIGHT_DMA_PRIORITY = 1
DMA_UNROLL = 8
PLAN_TILE = 512
MM_ROWS_F32 = 512
MM_ROWS = 1024
MM_COLS = 512
MM_COLS_REST = 256
LN_ROWS = 256
COMBINE_TOKENS = 128
CONV_COLS = 512
NEG_BIG = -1e30
LOG2_E = 1.4426950408889634
INT_MIN = -2 ** 31
NEG_INF_KEY = -2139095041


def _params(sem):
    return pltpu.CompilerParams(dimension_semantics=sem, vmem_limit_bytes=VMEM_LIMIT)


def _mm_cast_kernel(a_ref, b_ref, o_ref, abf_ref):
    @pl.when(pl.program_id(1) == 0)
    def _():
        abf_ref[...] = a_ref[...].astype(BF16)

    o_ref[...] = jnp.dot(abf_ref[...], b_ref[...], preferred_element_type=F32).astype(o_ref.dtype)


def _mm_kernel(a_ref, b_ref, o_ref):
    o_ref[...] = jnp.dot(a_ref[...], b_ref[...], preferred_element_type=F32).astype(o_ref.dtype)


def _mm_pair_kernel(a1_ref, a2_ref, b_ref, o_ref):
    k1 = a1_ref.shape[1]
    acc = jnp.dot(a1_ref[...], b_ref[0:k1, :], preferred_element_type=F32)
    acc = acc + jnp.dot(a2_ref[...], b_ref[k1:, :], preferred_element_type=F32)
    o_ref[...] = acc.astype(o_ref.dtype)


def _matmul_pair(a1, a2, b, out_dtype, tm, tn):
    m, k1 = a1.shape
    k2 = a2.shape[1]
    n = b.shape[1]
    tm = min(tm, m)
    assert m % tm == 0 and n % tn == 0 and b.shape[0] == k1 + k2
    return pl.pallas_call(
        _mm_pair_kernel,
        grid=(m // tm, n // tn),
        in_specs=[pl.BlockSpec((tm, k1), lambda i, j: (i, 0)),
                  pl.BlockSpec((tm, k2), lambda i, j: (i, 0)),
                  pl.BlockSpec((k1 + k2, tn), lambda i, j: (0, j))],
        out_specs=pl.BlockSpec((tm, tn), lambda i, j: (i, j)),
        out_shape=jax.ShapeDtypeStruct((m, n), out_dtype),
        compiler_params=_params(("parallel", "arbitrary")),
        name="matmul_pair",
    )(a1, a2, b)


def _matmul(a, b, out_dtype, tm, tn):
    m, k = a.shape
    n = b.shape[1]
    tm = min(tm, m)
    assert m % tm == 0 and n % tn == 0
    cast = a.dtype != BF16
    out_specs = pl.BlockSpec((tm, tn), lambda i, j: (i, j))
    out_shape = jax.ShapeDtypeStruct((m, n), out_dtype)
    if cast:
        out_specs = [out_specs, pl.BlockSpec((tm, k), lambda i, j: (i, 0))]
        out_shape = [out_shape, jax.ShapeDtypeStruct((m, k), BF16)]
    return pl.pallas_call(
        _mm_cast_kernel if cast else _mm_kernel,
        grid=(m // tm, n // tn),
        in_specs=[pl.BlockSpec((tm, k), lambda i, j: (i, 0)),
                  pl.BlockSpec((k, tn), lambda i, j: (0, j))],
        out_specs=out_specs,
        out_shape=out_shape,
        compiler_params=_params(("parallel", "arbitrary")),
        name="matmul_cast" if cast else "matmul",
    )(a, b)


def _attn_kernel(q_ref, k_ref, v_ref, qi_ref, kw_ref, g_ref, b_ref, o_ref,
                 kln_ref, key_ref, bias_ref, s_ref, mrun_ref, lrun_ref, acc_ref, *, top_k):
    i = pl.program_id(1)
    tq = QUERY_BLOCK
    ck = KEY_CHUNK

    @pl.when(i == 0)
    def _():
        kx = kw_ref[:, 0:IDX_DIM]
        mu = jnp.mean(kx, axis=-1, keepdims=True)
        var = jnp.mean(jnp.square(kx - mu), axis=-1, keepdims=True)
        y = (kx - mu) * lax.rsqrt(var + LN_EPS)
        kln_ref[...] = (y * g_ref[...] + b_ref[...]).astype(BF16)

    q_start = i * tq
    n_chunks = (q_start + tq + ck - 1) // ck
    q_pos = q_start + lax.broadcasted_iota(I32, (tq, 1), 0)
    w = kw_ref[pl.ds(pl.multiple_of(q_start, tq), tq), IDX_DIM:IDX_DIM + IDX_HEADS]
    w = w * (IDX_HEADS ** -0.5 * IDX_DIM ** -0.5)
    qi = jnp.concatenate([qi_ref[:, h * IDX_DIM:(h + 1) * IDX_DIM] for h in range(IDX_HEADS)], axis=0).astype(BF16)

    def chunk_off(c):
        return pl.multiple_of(c * ck, ck)

    def key_pos(c):
        return c * ck + lax.broadcasted_iota(I32, (1, ck), 1)

    def score_chunk(c, carry):
        off = chunk_off(c)
        kc = kln_ref[pl.ds(off, ck), :]
        d = lax.dot_general(qi, kc, (((1,), (1,)), ((), ())), preferred_element_type=F32)
        acc = jnp.zeros((tq, ck), F32)
        for h in range(IDX_HEADS):
            acc = acc + jnp.maximum(d[h * tq:(h + 1) * tq, :], 0.0) * w[:, h:h + 1]
        acc = jnp.where(key_pos(c) <= q_pos, acc, -jnp.inf)
        bits = pltpu.bitcast(acc, I32)
        key_ref[:, pl.ds(off, ck)] = bits ^ ((bits >> 31) & 0x7FFFFFFF)
        return carry

    lax.fori_loop(0, n_chunks, score_chunk, 0)

    def bit_body(b, carry):
        cand, cnt_cand = carry
        trial = cand | lax.shift_left(jnp.int32(1), jnp.int32(31) - jnp.asarray(b, I32))
        trial_b = jnp.broadcast_to(trial ^ INT_MIN, (tq, LANES))

        def cnt_chunk(c, cnt):
            kc = key_ref[:, pl.ds(chunk_off(c), ck)]
            for s in range(ck // LANES):
                cnt = cnt + jnp.where(kc[:, s * LANES:(s + 1) * LANES] >= trial_b, 1.0, 0.0)
            return cnt

        cnt = lax.fori_loop(0, n_chunks, cnt_chunk, jnp.zeros((tq, LANES), F32))
        total = jnp.sum(cnt, axis=1, keepdims=True)
        ok = total >= float(top_k)
        return jnp.where(ok, trial, cand), jnp.where(ok, total, cnt_cand)

    n_keys = (jnp.zeros((tq, 1), I32) + n_chunks * ck).astype(F32)
    cand, cnt_ge = lax.fori_loop(0, 32, bit_body, (jnp.zeros((tq, 1), I32), n_keys))
    thr = cand ^ INT_MIN

    def bias_chunk(c, carry):
        off = chunk_off(c)
        sel = (key_ref[:, pl.ds(off, ck)] >= thr) & (key_pos(c) <= q_pos)
        bias_ref[:, pl.ds(off, ck)] = jnp.where(sel, 0.0, NEG_BIG)
        return carry

    lax.fori_loop(0, n_chunks, bias_chunk, 0)

    tie = (cnt_ge > float(top_k)) & (thr > NEG_INF_KEY)

    @pl.when(jnp.max(jnp.where(tie, 1.0, 0.0)) > 0.5)
    def _():
        tri = (lax.broadcasted_iota(I32, (ck, ck), 0) <= lax.broadcasted_iota(I32, (ck, ck), 1)).astype(BF16)

        def gt_chunk(c, cnt):
            kc = key_ref[:, pl.ds(chunk_off(c), ck)]
            return cnt + jnp.sum(jnp.where(kc > thr, 1.0, 0.0), axis=1, keepdims=True)

        need = float(top_k) - lax.fori_loop(0, n_chunks, gt_chunk, jnp.zeros((tq, 1), F32))

        def tie_chunk(c, seen):
            off = chunk_off(c)
            kc = key_ref[:, pl.ds(off, ck)]
            eq = jnp.where(kc == thr, 1.0, 0.0)
            rank = seen + jnp.dot(eq.astype(BF16), tri, preferred_element_type=F32)
            keep = (kc > thr) | ((kc == thr) & ((rank <= need) | jnp.logical_not(tie)))
            sel = keep & (key_pos(c) <= q_pos)
            bias_ref[:, pl.ds(off, ck)] = jnp.where(sel, 0.0, NEG_BIG)
            return seen + jnp.sum(eq, axis=1, keepdims=True)

        lax.fori_loop(0, n_chunks, tie_chunk, jnp.zeros((tq, 1), F32))

    scale = HEAD_DIM ** -0.5 * LOG2_E
    gq = GQA_GROUP
    for g in range(KV_HEADS):
        qg = jnp.concatenate([q_ref[:, (g * gq + j) * HEAD_DIM:(g * gq + j + 1) * HEAD_DIM] for j in range(gq)],
                             axis=0)

        mrun_ref[...] = jnp.full(mrun_ref.shape, NEG_BIG, F32)

        def logit_chunk(c, carry, g=g, qg=qg):
            off = chunk_off(c)
            kc = k_ref[pl.ds(off, ck), g * HEAD_DIM:(g + 1) * HEAD_DIM]
            s = lax.dot_general(qg, kc, (((1,), (1,)), ((), ())), preferred_element_type=F32)
            bias = bias_ref[:, pl.ds(off, ck)]
            s = s * scale + jnp.concatenate([bias] * gq, axis=0)
            s_ref[:, pl.ds(off, ck)] = s
            m = mrun_ref[...]
            for t in range(ck // LANES):
                m = jnp.maximum(m, s[:, t * LANES:(t + 1) * LANES])
            mrun_ref[...] = m
            return carry

        lax.fori_loop(0, n_chunks, logit_chunk, 0)
        m = jnp.max(mrun_ref[...], axis=1, keepdims=True)

        lrun_ref[...] = jnp.zeros(lrun_ref.shape, F32)
        acc_ref[...] = jnp.zeros(acc_ref.shape, F32)

        def prob_chunk(c, carry, g=g, m=m):
            off = chunk_off(c)
            vc = v_ref[pl.ds(off, ck), g * HEAD_DIM:(g + 1) * HEAD_DIM]
            p = jnp.exp2(s_ref[:, pl.ds(off, ck)] - m)
            l = lrun_ref[...]
            for t in range(ck // LANES):
                l = l + p[:, t * LANES:(t + 1) * LANES]
            lrun_ref[...] = l
            acc_ref[...] += jnp.dot(p.astype(BF16), vc, preferred_element_type=F32)
            return carry

        lax.fori_loop(0, n_chunks, prob_chunk, 0)
        out = acc_ref[...] / jnp.sum(lrun_ref[...], axis=1, keepdims=True)
        for j in range(gq):
            h = g * gq + j
            o_ref[:, h * HEAD_DIM:(h + 1) * HEAD_DIM] = out[j * tq:(j + 1) * tq, :].astype(o_ref.dtype)


def _dsa_attention(qkv, idx, qi_off, kw_off, kn_g, kn_b, bsz, seq):
    top_k = min(DSA_TOPK_MAX, seq // 4)
    nq = seq // QUERY_BLOCK
    att_w = ATT_HEADS * HEAD_DIM
    kv_w = KV_HEADS * HEAD_DIM
    qi_w = IDX_HEADS * IDX_DIM
    assert seq % KEY_CHUNK == 0 and att_w % kv_w == 0 and qi_off % qi_w == 0 and kw_off % LANES == 0
    return pl.pallas_call(
        functools.partial(_attn_kernel, top_k=top_k),
        grid=(bsz, nq),
        in_specs=[pl.BlockSpec((QUERY_BLOCK, att_w), lambda b, i: (b * nq + i, 0)),
                  pl.BlockSpec((seq, kv_w), lambda b, i: (b, att_w // kv_w)),
                  pl.BlockSpec((seq, kv_w), lambda b, i: (b, att_w // kv_w + 1)),
                  pl.BlockSpec((QUERY_BLOCK, qi_w), lambda b, i: (b * nq + i, qi_off // qi_w)),
                  pl.BlockSpec((seq, LANES), lambda b, i: (b, kw_off // LANES)),
                  pl.BlockSpec((1, IDX_DIM), lambda b, i: (0, 0)),
                  pl.BlockSpec((1, IDX_DIM), lambda b, i: (0, 0))],
        out_specs=pl.BlockSpec((QUERY_BLOCK, att_w), lambda b, i: (b * nq + i, 0)),
        out_shape=jax.ShapeDtypeStruct((bsz * seq, att_w), BF16),
        scratch_shapes=[pltpu.VMEM((seq, IDX_DIM), BF16),
                        pltpu.VMEM((QUERY_BLOCK, seq), I32),
                        pltpu.VMEM((QUERY_BLOCK, seq), F32),
                        pltpu.VMEM((GQA_GROUP * QUERY_BLOCK, seq), F32),
                        pltpu.VMEM((GQA_GROUP * QUERY_BLOCK, LANES), F32),
                        pltpu.VMEM((GQA_GROUP * QUERY_BLOCK, LANES), F32),
                        pltpu.VMEM((GQA_GROUP * QUERY_BLOCK, HEAD_DIM), F32)],
        compiler_params=_params(("parallel", "arbitrary")),
        name="dsa_attention",
    )(qkv, qkv, qkv, idx, idx, kn_g.reshape(1, IDX_DIM), kn_b.reshape(1, IDX_DIM))


def _silu(x):
    return x / (1.0 + jnp.exp(-x))


def _ssd_kernel(xbc_ref, z_ref, dt_ref, cw_ref, cb_ref, dtb_ref, alog_ref, dsk_ref, ng_ref, o_ref,
                xpad_ref, act_ref, *h_refs):
    c = pl.program_id(1)
    cq = SSD_CHUNK
    width = SSD_HEADS * SSD_HEAD_DIM
    b_off = width
    c_off = width + SSD_GROUPS * SSD_STATE

    @pl.when(c == 0)
    def _():
        xpad_ref[0:SUBLANES, :] = jnp.zeros((SUBLANES, xpad_ref.shape[1]), F32)
        for h_ref in h_refs:
            h_ref[...] = jnp.zeros(h_ref.shape, F32)

    xpad_ref[SUBLANES:SUBLANES + cq, :] = xbc_ref[...]
    col = CONV_COLS
    for j in range(xpad_ref.shape[1] // col):
        cs = slice(j * col, (j + 1) * col)
        acc = cb_ref[:, cs] + jnp.zeros((cq, col), F32)
        for t in range(SSD_CONV):
            r0 = SUBLANES - (SSD_CONV - 1) + t
            acc = acc + xpad_ref[r0:r0 + cq, cs] * cw_ref[t:t + 1, cs]
        act_ref[:, cs] = _silu(acc)
    xpad_ref[0:SUBLANES, :] = xpad_ref[cq:cq + SUBLANES, :]

    xdt_in = dt_ref[:, 0:LANES] + dtb_ref[...]
    dt = jnp.maximum(xdt_in, 0.0) + jnp.log1p(jnp.exp(-jnp.abs(xdt_in)))
    da = dt * (-jnp.exp(alog_ref[...]))
    row = lax.broadcasted_iota(I32, (cq, cq), 0)
    coli = lax.broadcasted_iota(I32, (cq, cq), 1)
    causal = row >= coli
    tril = causal.astype(F32)
    acs = jnp.dot(tril, da, preferred_element_type=F32, precision=lax.Precision.HIGHEST)
    acs_t = acs.T
    dec_in = jnp.exp(acs)
    a_last = acs[cq - 1:cq, :]
    dec_out = jnp.exp(a_last - acs)
    dec_chunk = jnp.exp(a_last)

    gw = width // SSD_GROUPS
    for g in range(SSD_GROUPS):
        h_ref = h_refs[g]
        bg = act_ref[:, b_off + g * SSD_STATE:b_off + (g + 1) * SSD_STATE].astype(BF16)
        cg = act_ref[:, c_off + g * SSD_STATE:c_off + (g + 1) * SSD_STATE].astype(BF16)
        cbm = lax.dot_general(cg, bg, (((1,), (1,)), ((), ())), preferred_element_type=F32)
        hprevs = [h_ref[j] for j in range(SSD_HEADS_PER_GROUP)]
        ys, hnews = [], []
        for j in range(SSD_HEADS_PER_GROUP):
            hd = g * SSD_HEADS_PER_GROUP + j
            xs = slice(hd * SSD_HEAD_DIM, (hd + 1) * SSD_HEAD_DIM)
            seg = acs[:, hd:hd + 1] - acs_t[hd:hd + 1, :]
            lmat = jnp.exp(jnp.where(causal, seg, -jnp.inf))
            xh = act_ref[:, xs]
            xdt = xh * dt[:, hd:hd + 1]
            y = jnp.dot((cbm * lmat).astype(BF16), xdt.astype(BF16), preferred_element_type=F32)
            yoff = lax.dot_general(cg, hprevs[j].astype(BF16), (((1,), (1,)), ((), ())),
                                   preferred_element_type=F32)
            ys.append(y + yoff * dec_in[:, hd:hd + 1] + xh * dsk_ref[:, xs])
            st = lax.dot_general((xdt * dec_out[:, hd:hd + 1]).astype(BF16), bg,
                                 (((0,), (0,)), ((), ())), preferred_element_type=F32)
            hnews.append(hprevs[j] * dec_chunk[:, hd:hd + 1] + st)
        gs = slice(g * gw, (g + 1) * gw)
        gated = jnp.concatenate(ys, axis=1) * _silu(z_ref[:, gs])
        ms = jnp.mean(jnp.square(gated), axis=-1, keepdims=True)
        o_ref[:, gs] = (gated * lax.rsqrt(ms + RMS_EPS) * ng_ref[:, gs]).astype(o_ref.dtype)
        for j in range(SSD_HEADS_PER_GROUP):
            h_ref[j] = hnews[j]


def _ssd_mixer(ssd_in, z_off, dt_off, dt_w, conv_w, conv_b, dt_bias, a_log, d_skip, norm_g, bsz, seq):
    nc = seq // SSD_CHUNK
    width = SSD_HEADS * SSD_HEAD_DIM
    xbc_w = width + 2 * SSD_GROUPS * SSD_STATE
    assert z_off % width == 0 and dt_off % dt_w == 0 and dt_w >= LANES
    pad = LANES - SSD_HEADS
    dtb = jnp.pad(dt_bias.astype(F32), (0, pad)).reshape(1, LANES)
    alog = jnp.pad(a_log.astype(F32), (0, pad)).reshape(1, LANES)
    dsk = jnp.repeat(d_skip.astype(F32), SSD_HEAD_DIM).reshape(1, width)
    const = lambda b, c: (0, 0)
    return pl.pallas_call(
        _ssd_kernel,
        grid=(bsz, nc),
        in_specs=[pl.BlockSpec((SSD_CHUNK, xbc_w), lambda b, c: (b * nc + c, 0)),
                  pl.BlockSpec((SSD_CHUNK, width), lambda b, c: (b * nc + c, z_off // width)),
                  pl.BlockSpec((SSD_CHUNK, dt_w), lambda b, c: (b * nc + c, dt_off // dt_w)),
                  pl.BlockSpec((SSD_CONV, xbc_w), const),
                  pl.BlockSpec((1, xbc_w), const),
                  pl.BlockSpec((1, LANES), const),
                  pl.BlockSpec((1, LANES), const),
                  pl.BlockSpec((1, width), const),
                  pl.BlockSpec((1, width), const)],
        out_specs=pl.BlockSpec((SSD_CHUNK, width), lambda b, c: (b * nc + c, 0)),
        out_shape=jax.ShapeDtypeStruct((bsz * seq, width), BF16),
        scratch_shapes=[pltpu.VMEM((SSD_CHUNK + SUBLANES, xbc_w), F32),
                        pltpu.VMEM((SSD_CHUNK, xbc_w), F32)]
        + [pltpu.VMEM((SSD_HEADS_PER_GROUP, SSD_HEAD_DIM, SSD_STATE), F32) for _ in range(SSD_GROUPS)],
        compiler_params=_params(("parallel", "arbitrary")),
        name="ssd_mixer",
    )(ssd_in, ssd_in, ssd_in, conv_w.astype(F32), conv_b.astype(F32).reshape(1, xbc_w), dtb, alog, dsk,
      norm_g.astype(F32).reshape(1, width))


def _layer_norm_rows(x, g, b):
    mu = jnp.mean(x, axis=-1, keepdims=True)
    var = jnp.mean(jnp.square(x - mu), axis=-1, keepdims=True)
    return (x - mu) * lax.rsqrt(var + LN_EPS) * g + b


def _ln_router_kernel(x_ref, mix_ref, g_ref, b_ref, wr_ref, br_ref, o_ref, xp_ref, r_ref, *, alpha):
    x1 = _layer_norm_rows(alpha * x_ref[...] + mix_ref[...], g_ref[...], b_ref[...])
    o_ref[...] = x1
    half = xp_ref.shape[1]
    bits = pltpu.bitcast(x1.astype(BF16).astype(F32), jnp.uint32)
    xp_ref[...] = (bits[:, :half] >> 16) | bits[:, half:]
    logits = jnp.dot(x1, wr_ref[...], preferred_element_type=F32, precision=lax.Precision.HIGHEST)
    logits = logits + br_ref[...]
    rows = logits.shape[0]
    lane = lax.broadcasted_iota(I32, (rows, LANES), 1).astype(F32)
    ng, epg = float(N_EXPERT_GROUPS), float(EXPERTS_PER_GROUP)
    far = float(LANES)

    gmask = lane < ng
    gl = jnp.where(gmask, logits, -jnp.inf)
    ge = jnp.exp(gl - jnp.max(gl, axis=1, keepdims=True))
    gprob = ge / jnp.sum(ge, axis=1, keepdims=True)
    gprob = jnp.where(gmask, gprob, -1.0)
    gw = jnp.max(gprob, axis=1, keepdims=True)
    gsel = jnp.min(jnp.where(gprob == gw, lane, far), axis=1, keepdims=True)

    e_lo = ng + gsel * epg
    emask = (lane >= e_lo) & (lane < e_lo + epg)
    el = jnp.where(emask, logits, -jnp.inf)
    v0 = jnp.max(el, axis=1, keepdims=True)
    i0 = jnp.min(jnp.where(emask & (el == v0), lane, far), axis=1, keepdims=True)
    emask1 = emask & (lane != i0)
    el1 = jnp.where(emask1, logits, -jnp.inf)
    v1 = jnp.max(el1, axis=1, keepdims=True)
    i1 = jnp.min(jnp.where(emask1 & (el1 == v1), lane, far), axis=1, keepdims=True)
    e = jnp.exp(v1 - v0)
    p0 = 1.0 / (1.0 + e)
    p1 = e / (1.0 + e)
    out = jnp.where(lane == 0.0, i0 - ng,
                    jnp.where(lane == 1.0, i1 - ng,
                              jnp.where(lane == 2.0, gw * p0, jnp.where(lane == 3.0, gw * p1, 0.0))))
    r_ref[...] = out


def _ln_router(x, mixed, g, b, w_rg, b_rg, w_re, b_re, alpha, tr=LN_ROWS):
    t, d = x.shape
    ncol = N_EXPERT_GROUPS + N_EXPERTS
    wr = jnp.pad(jnp.concatenate([w_rg, w_re], axis=1).astype(F32), ((0, 0), (0, LANES - ncol)))
    br = jnp.pad(jnp.concatenate([b_rg, b_re]).astype(F32), (0, LANES - ncol)).reshape(1, LANES)
    const = lambda i: (0, 0)
    return pl.pallas_call(
        functools.partial(_ln_router_kernel, alpha=alpha),
        grid=(t // tr,),
        in_specs=[pl.BlockSpec((tr, d), lambda i: (i, 0)),
                  pl.BlockSpec((tr, d), lambda i: (i, 0)),
                  pl.BlockSpec((1, d), const), pl.BlockSpec((1, d), const),
                  pl.BlockSpec((d, LANES), const), pl.BlockSpec((1, LANES), const)],
        out_specs=[pl.BlockSpec((tr, d), lambda i: (i, 0)),
                   pl.BlockSpec((tr, d // 2), lambda i: (i, 0)),
                   pl.BlockSpec((tr, LANES), lambda i: (i, 0))],
        out_shape=[jax.ShapeDtypeStruct((t, d), F32), jax.ShapeDtypeStruct((t, d // 2), jnp.uint32),
                   jax.ShapeDtypeStruct((t, LANES), F32)],
        compiler_params=_params(("parallel",)),
        name="ln1_router",
    )(x, mixed, g.astype(F32).reshape(1, d), b.astype(F32).reshape(1, d), wr, br)


def _row_copy(src_ref, dst_ref, sem, tok, row):
    return pltpu.make_async_copy(src_ref.at[pl.ds(tok, 1)], dst_ref.at[pl.ds(row, 1)], sem)


def _moe_kernel(tok_ref, be_ref, bv_ref, nu_ref, x_ref, wg_ref, wu_ref, wd_ref, o_ref,
                xq_ref, xbf_ref, act_ref, gbuf_ref, dbuf_ref, sem, gsem, dsem, *, nf, nn):
    s = pl.program_id(0)
    p = pl.program_id(1)
    n_blk = pl.num_programs(0)
    n_used = nu_ref[0]
    steps = nf + nn
    sub = MOE_SUB
    n_sub = MOE_ROWS // sub
    half = xq_ref.shape[1]
    ks = MOE_K_SPLIT
    kq = wg_ref.shape[1] // ks

    def weights_dma(blk, ph, start):
        e = be_ref[blk]

        @pl.when(ph < nf)
        def _():
            slot = lax.rem(blk * nf + ph, MOE_RING)
            col = pl.multiple_of(ph * MOE_F_CHUNK, MOE_F_CHUNK)
            for q in range(ks):
                rows = pl.ds(q * kq, kq)
                for j, w_ref in enumerate((wg_ref, wu_ref)):
                    cp = pltpu.make_async_copy(w_ref.at[e, rows, pl.ds(col, MOE_F_CHUNK)],
                                               gbuf_ref.at[slot, j, rows], gsem.at[slot])
                    cp.start(priority=WEIGHT_DMA_PRIORITY) if start else cp.wait()

        @pl.when(ph >= nf)
        def _():
            k = ph - nf
            slot = lax.rem(blk * nn + k, MOE_RING)
            for j in range(2):
                col = pl.multiple_of((k + j * nn) * MOE_N_CHUNK, MOE_N_CHUNK)
                cp = pltpu.make_async_copy(wd_ref.at[e, :, pl.ds(col, MOE_N_CHUNK)],
                                           dbuf_ref.at[slot, j], dsem.at[slot])
                cp.start(priority=WEIGHT_DMA_PRIORITY) if start else cp.wait()

    @pl.when((s == 0) & (p == 0))
    def _():
        for a in range(MOE_LOOKAHEAD):
            weights_dma(s, p + a, True)

    ahead = p + MOE_LOOKAHEAD
    blk_a = jnp.where(ahead >= steps, s + 1, s)
    ph_a = jnp.where(ahead >= steps, ahead - steps, ahead)

    @pl.when(blk_a < n_used)
    def _():
        weights_dma(blk_a, ph_a, True)

    @pl.when(s < n_used)
    def _():
        weights_dma(s, p, False)

    def nsub_of(blk):
        return (bv_ref[blk] + (sub - 1)) // sub

    def issue(blk, r):
        base = blk * MOE_ROWS + r * sub

        def body(i, carry):
            for u in range(DMA_UNROLL):
                j = i * DMA_UNROLL + u
                _row_copy(x_ref, xq_ref, sem, tok_ref[base + j], r * sub + j).start()
            return carry
        lax.fori_loop(0, sub // DMA_UNROLL, body, 0)

    nsub = nsub_of(s)

    @pl.when((s == 0) & (p == 0))
    def _():
        for r in range(n_sub):
            @pl.when(r < nsub)
            def _(r=r):
                issue(s, r)

    @pl.when(p == 0)
    def _():
        for r in range(n_sub):
            @pl.when(r < nsub)
            def _(r=r):
                pltpu.make_async_copy(x_ref.at[pl.ds(0, sub)], xq_ref.at[pl.ds(r * sub, sub)], sem).wait()
        for r in range(n_sub):
            @pl.when(r < nsub)
            def _(r=r):
                rs = slice(r * sub, (r + 1) * sub)
                word = xq_ref[rs, :]
                xbf_ref[rs, 0:half] = pltpu.bitcast(word << 16, F32).astype(BF16)
                xbf_ref[rs, half:2 * half] = pltpu.bitcast(word & jnp.uint32(0xFFFF0000), F32).astype(BF16)

    nxt = jnp.minimum(s + 1, n_blk - 1)
    for r in range(n_sub):
        @pl.when((p == r + 1) & (s + 1 < n_blk) & (r < nsub_of(nxt)))
        def _(r=r):
            issue(nxt, r)

    for k in range(1, n_sub + 1):
        m = k * sub

        @pl.when((p < nf) & (nsub == k))
        def _(m=m):
            slot = lax.rem(s * nf + p, MOE_RING)
            gate = up = None
            for q in range(ks):
                xr = xbf_ref[0:m, q * kq:(q + 1) * kq]
                gq = jnp.dot(xr, gbuf_ref[slot, 0, q * kq:(q + 1) * kq, :].astype(BF16), preferred_element_type=F32)
                uq = jnp.dot(xr, gbuf_ref[slot, 1, q * kq:(q + 1) * kq, :].astype(BF16), preferred_element_type=F32)
                gate = gq if gate is None else gate + gq
                up = uq if up is None else up + uq
            act_ref[p, 0:m, :] = (_silu(gate) * up).astype(BF16)

    def down(slot, j, m):
        acc = None
        for q in range(nf):
            wq = dbuf_ref[slot, j, q * MOE_F_CHUNK:(q + 1) * MOE_F_CHUNK, :].astype(BF16)
            t = jnp.dot(act_ref[q, 0:m, :], wq, preferred_element_type=F32)
            acc = t if acc is None else acc + t
        return pltpu.bitcast(acc.astype(BF16).astype(F32), jnp.uint32)

    for k in range(0, n_sub + 1):
        m = k * sub

        @pl.when((p >= nf) & (nsub == k))
        def _(m=m):
            if m > 0:
                slot = lax.rem(s * nn + p - nf, MOE_RING)
                o_ref[0:m, :] = (down(slot, 0, m) >> 16) | down(slot, 1, m)
            if m < MOE_ROWS:
                o_ref[m:MOE_ROWS, :] = jnp.zeros((MOE_ROWS - m, o_ref.shape[1]), jnp.uint32)


def _moe_ffn(xp, row_tok, blk_e, blk_valid, n_used, w_gate, w_up, w_down):
    n_rows = row_tok.shape[0]
    n_e, d, f = w_gate.shape
    nf = f // MOE_F_CHUNK
    nn = (d // 2) // MOE_N_CHUNK
    n_blk = n_rows // MOE_ROWS
    assert nf + nn > MOE_ROWS // MOE_SUB and d % MOE_K_SPLIT == 0
    assert MOE_LOOKAHEAD < MOE_RING and MOE_LOOKAHEAD <= min(nf, nn)

    def out_map(s, p, tok, be, bv, nu):
        return (s, jnp.maximum(p - nf, 0))

    hbm = pl.BlockSpec(memory_space=pl.ANY)
    return pl.pallas_call(
        functools.partial(_moe_kernel, nf=nf, nn=nn),
        grid_spec=pltpu.PrefetchScalarGridSpec(
            num_scalar_prefetch=4,
            grid=(n_blk, nf + nn),
            in_specs=[hbm, hbm, hbm, hbm],
            out_specs=pl.BlockSpec((MOE_ROWS, MOE_N_CHUNK), out_map),
            scratch_shapes=[pltpu.VMEM((MOE_ROWS, d // 2), jnp.uint32),
                            pltpu.VMEM((MOE_ROWS, d), BF16),
                            pltpu.VMEM((nf, MOE_ROWS, MOE_F_CHUNK), BF16),
                            pltpu.VMEM((MOE_RING, 2, d, MOE_F_CHUNK), F32),
                            pltpu.VMEM((MOE_RING, 2, f, MOE_N_CHUNK), F32),
                            pltpu.SemaphoreType.DMA(()),
                            pltpu.SemaphoreType.DMA((MOE_RING,)),
                            pltpu.SemaphoreType.DMA((MOE_RING,))]),
        out_shape=jax.ShapeDtypeStruct((n_rows, d // 2), jnp.uint32),
        compiler_params=_params(("arbitrary", "arbitrary")),
        name="moe_ffn",
    )(row_tok, blk_e, blk_valid, n_used, xp, w_gate, w_up, w_down)


def _combine_kernel(dest_ref, y_ref, x_ref, r_ref, g_ref, b_ref, o_ref, buf_ref, sem, *, alpha, tt):
    s = pl.program_id(0)
    n = pl.num_programs(0)

    rows = 2 * tt

    def start(blk, slot):
        base = blk * rows

        def body(i, carry):
            for u in range(DMA_UNROLL):
                r = i * DMA_UNROLL + u
                pltpu.make_async_copy(y_ref.at[pl.ds(dest_ref[base + r], 1)],
                                      buf_ref.at[slot, pl.ds(r, 1)], sem.at[slot]).start()
            return carry
        lax.fori_loop(0, rows // DMA_UNROLL, body, 0)

    @pl.when(s == 0)
    def _():
        start(s, 0)

    for par in range(2):
        @pl.when((s % 2 == par) & (s + 1 < n))
        def _(par=par):
            start(s + 1, 1 - par)

    slot = s % 2
    pltpu.make_async_copy(y_ref.at[pl.ds(0, rows)], buf_ref.at[slot], sem.at[slot]).wait()
    g0 = r_ref[:, 2:3]
    g1 = r_ref[:, 3:4]
    w0 = buf_ref[slot, 0:tt, :]
    w1 = buf_ref[slot, tt:2 * tt, :]
    half = w0.shape[1]
    himask = jnp.uint32(0xFFFF0000)
    y_lo = g0 * pltpu.bitcast(w0 << 16, F32) + g1 * pltpu.bitcast(w1 << 16, F32)
    y_hi = g0 * pltpu.bitcast(w0 & himask, F32) + g1 * pltpu.bitcast(w1 & himask, F32)
    r_lo = alpha * x_ref[:, 0:half] + y_lo
    r_hi = alpha * x_ref[:, half:2 * half] + y_hi
    inv_d = 1.0 / (2 * half)
    mu = (jnp.sum(r_lo, axis=-1, keepdims=True) + jnp.sum(r_hi, axis=-1, keepdims=True)) * inv_d
    var = (jnp.sum(jnp.square(r_lo - mu), axis=-1, keepdims=True)
           + jnp.sum(jnp.square(r_hi - mu), axis=-1, keepdims=True)) * inv_d
    rstd = lax.rsqrt(var + LN_EPS)
    o_ref[:, 0:half] = (r_lo - mu) * rstd * g_ref[:, 0:half] + b_ref[:, 0:half]
    o_ref[:, half:2 * half] = (r_hi - mu) * rstd * g_ref[:, half:2 * half] + b_ref[:, half:2 * half]


def _combine_ln(y_rows, dest, x1, route, g, b, alpha, tt):
    t, d = x1.shape
    const = lambda i, dr: (0, 0)
    return pl.pallas_call(
        functools.partial(_combine_kernel, alpha=alpha, tt=tt),
        grid_spec=pltpu.PrefetchScalarGridSpec(
            num_scalar_prefetch=1,
            grid=(t // tt,),
            in_specs=[pl.BlockSpec(memory_space=pl.ANY),
                      pl.BlockSpec((tt, d), lambda i, dr: (i, 0)),
                      pl.BlockSpec((tt, LANES), lambda i, dr: (i, 0)),
                      pl.BlockSpec((1, d), const), pl.BlockSpec((1, d), const)],
            out_specs=pl.BlockSpec((tt, d), lambda i, dr: (i, 0)),
            scratch_shapes=[pltpu.VMEM((2, 2 * tt, d // 2), jnp.uint32),
                            pltpu.SemaphoreType.DMA((2,))]),
        out_shape=jax.ShapeDtypeStruct((t, d), F32),
        compiler_params=_params(("arbitrary",)),
        name="moe_combine_ln2",
    )(dest, y_rows, x1, route, g.astype(F32).reshape(1, d), b.astype(F32).reshape(1, d))


def _plan_kernel(r_ref, dest_ref, meta_ref, cnt_ref, pstart_ref, carry_ref):
    ph = pl.program_id(0)
    i = pl.program_id(1)
    ts = r_ref.shape[0]
    rows = float(MOE_ROWS)
    lane = lax.broadcasted_iota(I32, (ts, LANES), 1).astype(F32)
    oh0 = jnp.where(lane == r_ref[:, 0:1], 1.0, 0.0)
    oh1 = jnp.where(lane == r_ref[:, 1:2], 1.0, 0.0)
    oh = oh0 + oh1

    @pl.when((ph == 0) & (i == 0))
    def _():
        cnt_ref[...] = jnp.zeros(cnt_ref.shape, F32)

    @pl.when(ph == 0)
    def _():
        cnt_ref[...] += jnp.sum(oh, axis=0, keepdims=True)

    @pl.when((ph == 0) & (i == pl.num_programs(1) - 1))
    def _():
        sq = (LANES, LANES)
        r_i = lax.broadcasted_iota(I32, sq, 0)
        c_i = lax.broadcasted_iota(I32, sq, 1)
        counts = jnp.broadcast_to(cnt_ref[...], sq)
        nblk_e = jnp.floor((counts + (rows - 0.5)) * (1.0 / rows))
        upper = jnp.where(r_i <= c_i, 1.0, 0.0)
        pend = jnp.dot(nblk_e.astype(BF16), upper.astype(BF16), preferred_element_type=F32)
        pstart = pend - nblk_e
        n_used = jnp.max(pend, axis=1, keepdims=True)
        pstart_ref[...] = pstart[0:1, :] * rows
        carry_ref[...] = jnp.zeros(carry_ref.shape, F32)
        b_eff = jnp.minimum(c_i.astype(F32), n_used - 1.0)
        pend_t, pstart_t, counts_t = pend.T, pstart.T, counts.T
        blk_e = jnp.sum(jnp.where(pend_t <= b_eff, 1.0, 0.0), axis=0, keepdims=True)
        blk_e = jnp.minimum(blk_e, float(N_EXPERTS - 1))
        pick = r_i.astype(F32) == blk_e
        cnt_b = jnp.sum(jnp.where(pick, counts_t, 0.0), axis=0, keepdims=True)
        first_b = jnp.sum(jnp.where(pick, pstart_t, 0.0), axis=0, keepdims=True)
        valid = jnp.clip(cnt_b - (b_eff[0:1, :] - first_b) * rows, 0.0, rows)
        valid = jnp.where(c_i[0:1, :].astype(F32) < n_used[0:1, :], valid, 0.0)
        sub_i = lax.broadcasted_iota(I32, meta_ref.shape, 0)
        meta = jnp.where(sub_i == 0, blk_e, jnp.where(sub_i == 1, valid, jnp.where(sub_i == 2, n_used[0:1, :], 0.0)))
        meta_ref[...] = meta.astype(I32)

    @pl.when(ph == 1)
    def _():
        below = lax.broadcasted_iota(I32, (ts, ts), 0) > lax.broadcasted_iota(I32, (ts, ts), 1)
        earlier = jnp.dot(below.astype(BF16), oh.astype(BF16), preferred_element_type=F32)
        base = pstart_ref[...] + carry_ref[...] + earlier
        d0 = jnp.sum(base * oh0, axis=1, keepdims=True)
        d1 = jnp.sum(base * oh1, axis=1, keepdims=True)
        carry_ref[...] += jnp.sum(oh, axis=0, keepdims=True)
        both = jnp.where(lane == 0.0, d0, jnp.where(lane == 1.0, d1, 0.0))
        dest_ref[...] = both.T[0:2, :].astype(I32)


def _dispatch_plan(route, tt):
    t = route.shape[0]
    m = 2 * t
    n_blk = -(-m // MOE_ROWS) + N_EXPERTS
    n_rows = n_blk * MOE_ROWS
    ts = min(PLAN_TILE, t)
    assert n_blk <= LANES and t % ts == 0 and N_EXPERTS <= LANES
    dest2, meta = pl.pallas_call(
        _plan_kernel,
        grid=(2, t // ts),
        in_specs=[pl.BlockSpec((ts, LANES), lambda ph, i: (i, 0))],
        out_specs=[pl.BlockSpec((2, ts), lambda ph, i: (0, i * ph)),
                   pl.BlockSpec((SUBLANES, LANES), lambda ph, i: (0, 0))],
        out_shape=[jax.ShapeDtypeStruct((2, t), I32), jax.ShapeDtypeStruct((SUBLANES, LANES), I32)],
        scratch_shapes=[pltpu.VMEM((1, LANES), F32), pltpu.VMEM((1, LANES), F32), pltpu.VMEM((1, LANES), F32)],
        compiler_params=_params(("arbitrary", "arbitrary")),
        name="moe_plan",
    )(route)
    tok = jnp.arange(t, dtype=I32)
    row_tok = jnp.zeros((n_rows,), I32).at[dest2.reshape(m)].set(jnp.concatenate([tok, tok]))
    dest_tiles = dest2.reshape(2, t // tt, tt).transpose(1, 0, 2).reshape(m)
    return row_tok, meta[0, :n_blk], meta[1, :n_blk], meta[2, 0:1], dest_tiles


def kernel(x, w_in, idx_kn_g, idx_kn_b, conv_w, conv_b, dt_bias, a_log, d_skip, ssd_norm_g, w_out,
           ln1_g, ln1_b, w_rg, b_rg, w_re, b_re, w_gate, w_up, w_down, ln2_g, ln2_b):
    bsz, seq, d = x.shape
    depth = w_in.shape[0]
    alpha = (2 * depth) ** 0.25
    att_w = ATT_HEADS * HEAD_DIM
    kv_w = KV_HEADS * HEAD_DIM
    qi_w = IDX_HEADS * IDX_DIM
    ssd_w = SSD_HEADS * SSD_HEAD_DIM
    xbc_w = ssd_w + 2 * SSD_GROUPS * SSD_STATE
    sizes = (att_w, kv_w, kv_w, qi_w, IDX_DIM, IDX_HEADS, ssd_w, xbc_w, SSD_HEADS)
    offs = [0]
    for sz in sizes:
        offs.append(offs[-1] + sz)
    tt = COMBINE_TOKENS
    xf = x.reshape(bsz * seq, d)
    for l in range(depth):
        col = lambda a, b: w_in[l][:, offs[a]:offs[b]].astype(BF16)
        zpad = lambda n: jnp.zeros((d, n), BF16)
        dt_w = 2 * LANES
        z_off, qi_off = xbc_w, xbc_w + ssd_w
        dt_off = qi_off + qi_w
        kw_off = dt_off + dt_w
        w_rest = jnp.concatenate([col(7, 8), col(6, 7), col(3, 4), col(8, 9), zpad(dt_w - SSD_HEADS),
                                  col(4, 6), zpad(LANES - IDX_DIM - IDX_HEADS), zpad(LANES)], axis=1)
        qkv, xbf = _matmul(xf, col(0, 3), BF16, MM_ROWS_F32, MM_COLS)
        rest = _matmul(xbf, w_rest, F32, MM_ROWS, MM_COLS_REST)
        att = _dsa_attention(qkv, rest, qi_off, kw_off, idx_kn_g[l], idx_kn_b[l], bsz, seq)
        ssd = _ssd_mixer(rest, z_off, dt_off, dt_w, conv_w[l], conv_b[l], dt_bias[l], a_log[l], d_skip[l],
                         ssd_norm_g[l], bsz, seq)
        mixed = _matmul_pair(att, ssd, w_out[l].astype(BF16), F32, MM_ROWS, MM_COLS)
        x1, x1p, route = _ln_router(xf, mixed, ln1_g[l], ln1_b[l], w_rg[l], b_rg[l], w_re[l], b_re[l], alpha)
        row_tok, blk_e, blk_valid, n_used, dest_tiles = _dispatch_plan(route, tt)
        y_rows = _moe_ffn(x1p, row_tok, blk_e, blk_valid, n_used, w_gate[l], w_up[l], w_down[l])
        xf = _combine_ln(y_rows, dest_tiles, x1, route, ln2_g[l], ln2_b[l], alpha, tt)
    return xf.reshape(bsz, seq, d)
```

```python
import functools

import jax
import jax.numpy as jnp
from jax import lax
from jax.experimental import pallas as pl
from jax.experimental.pallas import tpu as pltpu

F32 = jnp.float32
BF16 = jnp.bfloat16
I32 = jnp.int32

HEAD_DIM = 128
KV_HEADS = 4
GQA_GROUP = 4
ATT_HEADS = KV_HEADS * GQA_GROUP
IDX_HEADS = 16
IDX_DIM = 64
DSA_TOPK_MAX = 256
QUERY_BLOCK = 128
SSD_HEAD_DIM = 64
SSD_GROUPS = 8
SSD_HEADS_PER_GROUP = 4
SSD_HEADS = SSD_GROUPS * SSD_HEADS_PER_GROUP
SSD_STATE = 128
SSD_CONV = 4
SSD_CHUNK = 128
N_EXPERT_GROUPS = 8
EXPERTS_PER_GROUP = 8
N_EXPERTS = 64
LN_EPS = 1e-5
RMS_EPS = 1e-5

LANES = 128
SUBLANES = 8
VMEM_LIMIT = 56 * 1024 * 1024

KEY_CHUNK = 1024
MOE_ROWS = 768
MOE_SUB = 256
MOE_F_CHUNK = 256
MOE_N_CHUNK = 512
MOE_K_SPLIT = 4
MOE_RING = 3
MOE_LOOKAHEAD = 2
WEIGHT_DMA_PRIORITY = 1
DMA_UNROLL = 8
PLAN_TILE = 512
MM_ROWS_F32 = 512
MM_ROWS = 1024
MM_COLS = 512
MM_COLS_REST = 512
LN_ROWS = 256
COMBINE_TOKENS = 256
CONV_COLS = 512
NEG_BIG = -1e30
LOG2_E = 1.4426950408889634
INT_MIN = -2 ** 31
NEG_INF_KEY = -2139095041


def _params(sem):
    return pltpu.CompilerParams(dimension_semantics=sem, vmem_limit_bytes=VMEM_LIMIT)


def _mm_cast_kernel(a_ref, b_ref, o_ref, abf_ref):
    @pl.when(pl.program_id(1) == 0)
    def _():
        abf_ref[...] = a_ref[...].astype(BF16)

    o_ref[...] = jnp.dot(abf_ref[...], b_ref[...], preferred_element_type=F32).astype(o_ref.dtype)


def _mm_kernel(a_ref, b_ref, o_ref):
    o_ref[...] = jnp.dot(a_ref[...], b_ref[...], preferred_element_type=F32).astype(o_ref.dtype)


def _mm_pair_kernel(a1_ref, a2_ref, b_ref, o_ref):
    k1 = a1_ref.shape[1]
    acc = jnp.dot(a1_ref[...], b_ref[0:k1, :], preferred_element_type=F32)
    acc = acc + jnp.dot(a2_ref[...], b_ref[k1:, :], preferred_element_type=F32)
    o_ref[...] = acc.astype(o_ref.dtype)


def _matmul_pair(a1, a2, b, out_dtype, tm, tn):
    m, k1 = a1.shape
    k2 = a2.shape[1]
    n = b.shape[1]
    tm = min(tm, m)
    assert m % tm == 0 and n % tn == 0 and b.shape[0] == k1 + k2
    return pl.pallas_call(
        _mm_pair_kernel,
        grid=(m // tm, n // tn),
        in_specs=[pl.BlockSpec((tm, k1), lambda i, j: (i, 0)),
                  pl.BlockSpec((tm, k2), lambda i, j: (i, 0)),
                  pl.BlockSpec((k1 + k2, tn), lambda i, j: (0, j))],
        out_specs=pl.BlockSpec((tm, tn), lambda i, j: (i, j)),
        out_shape=jax.ShapeDtypeStruct((m, n), out_dtype),
        compiler_params=_params(("parallel", "arbitrary")),
        name="matmul_pair",
    )(a1, a2, b)


def _matmul(a, b, out_dtype, tm, tn):
    m, k = a.shape
    n = b.shape[1]
    tm = min(tm, m)
    assert m % tm == 0 and n % tn == 0
    cast = a.dtype != BF16
    out_specs = pl.BlockSpec((tm, tn), lambda i, j: (i, j))
    out_shape = jax.ShapeDtypeStruct((m, n), out_dtype)
    if cast:
        out_specs = [out_specs, pl.BlockSpec((tm, k), lambda i, j: (i, 0))]
        out_shape = [out_shape, jax.ShapeDtypeStruct((m, k), BF16)]
    return pl.pallas_call(
        _mm_cast_kernel if cast else _mm_kernel,
        grid=(m // tm, n // tn),
        in_specs=[pl.BlockSpec((tm, k), lambda i, j: (i, 0)),
                  pl.BlockSpec((k, tn), lambda i, j: (0, j))],
        out_specs=out_specs,
        out_shape=out_shape,
        compiler_params=_params(("parallel", "arbitrary")),
        name="matmul_cast" if cast else "matmul",
    )(a, b)


def _attn_kernel(q_ref, k_ref, v_ref, qi_ref, kw_ref, g_ref, b_ref, o_ref,
                 kln_ref, key_ref, bias_ref, s_ref, mrun_ref, lrun_ref, acc_ref, *, top_k):
    i = pl.program_id(1)
    tq = QUERY_BLOCK
    ck = KEY_CHUNK

    @pl.when(i == 0)
    def _():
        kx = kw_ref[:, 0:IDX_DIM]
        mu = jnp.mean(kx, axis=-1, keepdims=True)
        var = jnp.mean(jnp.square(kx - mu), axis=-1, keepdims=True)
        y = (kx - mu) * lax.rsqrt(var + LN_EPS)
        kln_ref[...] = (y * g_ref[...] + b_ref[...]).astype(BF16)

    q_start = i * tq
    n_chunks = (q_start + tq + ck - 1) // ck
    q_pos = q_start + lax.broadcasted_iota(I32, (tq, 1), 0)
    w = kw_ref[pl.ds(pl.multiple_of(q_start, tq), tq), IDX_DIM:IDX_DIM + IDX_HEADS]
    w = w * (IDX_HEADS ** -0.5 * IDX_DIM ** -0.5)
    qi = jnp.concatenate([qi_ref[:, h * IDX_DIM:(h + 1) * IDX_DIM] for h in range(IDX_HEADS)], axis=0).astype(BF16)

    def chunk_off(c):
        return pl.multiple_of(c * ck, ck)

    def key_pos(c):
        return c * ck + lax.broadcasted_iota(I32, (1, ck), 1)

    def score_chunk(c, carry):
        off = chunk_off(c)
        kc = kln_ref[pl.ds(off, ck), :]
        d = lax.dot_general(qi, kc, (((1,), (1,)), ((), ())), preferred_element_type=F32)
        acc = jnp.zeros((tq, ck), F32)
        for h in range(IDX_HEADS):
            acc = acc + jnp.maximum(d[h * tq:(h + 1) * tq, :], 0.0) * w[:, h:h + 1]
        acc = jnp.where(key_pos(c) <= q_pos, acc, -jnp.inf)
        bits = pltpu.bitcast(acc, I32)
        key_ref[:, pl.ds(off, ck)] = bits ^ ((bits >> 31) & 0x7FFFFFFF)
        return carry

    lax.fori_loop(0, n_chunks, score_chunk, 0)

    def bit_body(b, carry):
        cand, cnt_cand = carry
        trial = cand | lax.shift_left(jnp.int32(1), jnp.int32(31) - jnp.asarray(b, I32))
        trial_b = jnp.broadcast_to(trial ^ INT_MIN, (tq, LANES))

        def cnt_chunk(c, cnt):
            kc = key_ref[:, pl.ds(chunk_off(c), ck)]
            for s in range(ck // LANES):
                cnt = cnt + jnp.where(kc[:, s * LANES:(s + 1) * LANES] >= trial_b, 1.0, 0.0)
            return cnt

        cnt = lax.fori_loop(0, n_chunks, cnt_chunk, jnp.zeros((tq, LANES), F32))
        total = jnp.sum(cnt, axis=1, keepdims=True)
        ok = total >= float(top_k)
        return jnp.where(ok, trial, cand), jnp.where(ok, total, cnt_cand)

    n_keys = (jnp.zeros((tq, 1), I32) + n_chunks * ck).astype(F32)
    cand, cnt_ge = lax.fori_loop(0, 32, bit_body, (jnp.zeros((tq, 1), I32), n_keys))
    thr = cand ^ INT_MIN

    def bias_chunk(c, carry):
        off = chunk_off(c)
        sel = (key_ref[:, pl.ds(off, ck)] >= thr) & (key_pos(c) <= q_pos)
        bias_ref[:, pl.ds(off, ck)] = jnp.where(sel, 0.0, NEG_BIG)
        return carry

    lax.fori_loop(0, n_chunks, bias_chunk, 0)

    tie = (cnt_ge > float(top_k)) & (thr > NEG_INF_KEY)

    @pl.when(jnp.max(jnp.where(tie, 1.0, 0.0)) > 0.5)
    def _():
        tri = (lax.broadcasted_iota(I32, (ck, ck), 0) <= lax.broadcasted_iota(I32, (ck, ck), 1)).astype(BF16)

        def gt_chunk(c, cnt):
            kc = key_ref[:, pl.ds(chunk_off(c), ck)]
            return cnt + jnp.sum(jnp.where(kc > thr, 1.0, 0.0), axis=1, keepdims=True)

        need = float(top_k) - lax.fori_loop(0, n_chunks, gt_chunk, jnp.zeros((tq, 1), F32))

        def tie_chunk(c, seen):
            off = chunk_off(c)
            kc = key_ref[:, pl.ds(off, ck)]
            eq = jnp.where(kc == thr, 1.0, 0.0)
            rank = seen + jnp.dot(eq.astype(BF16), tri, preferred_element_type=F32)
            keep = (kc > thr) | ((kc == thr) & ((rank <= need) | jnp.logical_not(tie)))
            sel = keep & (key_pos(c) <= q_pos)
            bias_ref[:, pl.ds(off, ck)] = jnp.where(sel, 0.0, NEG_BIG)
            return seen + jnp.sum(eq, axis=1, keepdims=True)

        lax.fori_loop(0, n_chunks, tie_chunk, jnp.zeros((tq, 1), F32))

    scale = HEAD_DIM ** -0.5 * LOG2_E
    gq = GQA_GROUP
    for g in range(KV_HEADS):
        qg = jnp.concatenate([q_ref[:, (g * gq + j) * HEAD_DIM:(g * gq + j + 1) * HEAD_DIM] for j in range(gq)],
                             axis=0)

        mrun_ref[...] = jnp.full(mrun_ref.shape, NEG_BIG, F32)

        def logit_chunk(c, carry, g=g, qg=qg):
            off = chunk_off(c)
            kc = k_ref[pl.ds(off, ck), g * HEAD_DIM:(g + 1) * HEAD_DIM]
            s = lax.dot_general(qg, kc, (((1,), (1,)), ((), ())), preferred_element_type=F32)
            bias = bias_ref[:, pl.ds(off, ck)]
            s = s * scale + jnp.concatenate([bias] * gq, axis=0)
            s_ref[:, pl.ds(off, ck)] = s
            m = mrun_ref[...]
            for t in range(ck // LANES):
                m = jnp.maximum(m, s[:, t * LANES:(t + 1) * LANES])
            mrun_ref[...] = m
            return carry

        lax.fori_loop(0, n_chunks, logit_chunk, 0)
        m = jnp.max(mrun_ref[...], axis=1, keepdims=True)

        lrun_ref[...] = jnp.zeros(lrun_ref.shape, F32)
        acc_ref[...] = jnp.zeros(acc_ref.shape, F32)

        def prob_chunk(c, carry, g=g, m=m):
            off = chunk_off(c)
            vc = v_ref[pl.ds(off, ck), g * HEAD_DIM:(g + 1) * HEAD_DIM]
            p = jnp.exp2(s_ref[:, pl.ds(off, ck)] - m)
            l = lrun_ref[...]
            for t in range(ck // LANES):
                l = l + p[:, t * LANES:(t + 1) * LANES]
            lrun_ref[...] = l
            acc_ref[...] += jnp.dot(p.astype(BF16), vc, preferred_element_type=F32)
            return carry

        lax.fori_loop(0, n_chunks, prob_chunk, 0)
        out = acc_ref[...] / jnp.sum(lrun_ref[...], axis=1, keepdims=True)
        for j in range(gq):
            h = g * gq + j
            o_ref[:, h * HEAD_DIM:(h + 1) * HEAD_DIM] = out[j * tq:(j + 1) * tq, :].astype(o_ref.dtype)


def _dsa_attention(qkv, idx, qi_off, kw_off, kn_g, kn_b, bsz, seq):
    top_k = min(DSA_TOPK_MAX, seq // 4)
    nq = seq // QUERY_BLOCK
    att_w = ATT_HEADS * HEAD_DIM
    kv_w = KV_HEADS * HEAD_DIM
    qi_w = IDX_HEADS * IDX_DIM
    assert seq % KEY_CHUNK == 0 and att_w % kv_w == 0 and qi_off % qi_w == 0 and kw_off % LANES == 0
    return pl.pallas_call(
        functools.partial(_attn_kernel, top_k=top_k),
        grid=(bsz, nq),
        in_specs=[pl.BlockSpec((QUERY_BLOCK, att_w), lambda b, i: (b * nq + i, 0)),
                  pl.BlockSpec((seq, kv_w), lambda b, i: (b, att_w // kv_w)),
                  pl.BlockSpec((seq, kv_w), lambda b, i: (b, att_w // kv_w + 1)),
                  pl.BlockSpec((QUERY_BLOCK, qi_w), lambda b, i: (b * nq + i, qi_off // qi_w)),
                  pl.BlockSpec((seq, LANES), lambda b, i: (b, kw_off // LANES)),
                  pl.BlockSpec((1, IDX_DIM), lambda b, i: (0, 0)),
                  pl.BlockSpec((1, IDX_DIM), lambda b, i: (0, 0))],
        out_specs=pl.BlockSpec((QUERY_BLOCK, att_w), lambda b, i: (b * nq + i, 0)),
        out_shape=jax.ShapeDtypeStruct((bsz * seq, att_w), BF16),
        scratch_shapes=[pltpu.VMEM((seq, IDX_DIM), BF16),
                        pltpu.VMEM((QUERY_BLOCK, seq), I32),
                        pltpu.VMEM((QUERY_BLOCK, seq), F32),
                        pltpu.VMEM((GQA_GROUP * QUERY_BLOCK, seq), F32),
                        pltpu.VMEM((GQA_GROUP * QUERY_BLOCK, LANES), F32),
                        pltpu.VMEM((GQA_GROUP * QUERY_BLOCK, LANES), F32),
                        pltpu.VMEM((GQA_GROUP * QUERY_BLOCK, HEAD_DIM), F32)],
        compiler_params=_params(("parallel", "arbitrary")),
        name="dsa_attention",
    )(qkv, qkv, qkv, idx, idx, kn_g.reshape(1, IDX_DIM), kn_b.reshape(1, IDX_DIM))


def _silu(x):
    return x / (1.0 + jnp.exp(-x))


def _ssd_kernel(xbc_ref, z_ref, dt_ref, cw_ref, cb_ref, dtb_ref, alog_ref, dsk_ref, ng_ref, o_ref,
                xpad_ref, act_ref, *h_refs):
    c = pl.program_id(1)
    cq = SSD_CHUNK
    width = SSD_HEADS * SSD_HEAD_DIM
    b_off = width
    c_off = width + SSD_GROUPS * SSD_STATE

    @pl.when(c == 0)
    def _():
        xpad_ref[0:SUBLANES, :] = jnp.zeros((SUBLANES, xpad_ref.shape[1]), F32)
        for h_ref in h_refs:
            h_ref[...] = jnp.zeros(h_ref.shape, F32)

    xpad_ref[SUBLANES:SUBLANES + cq, :] = xbc_ref[...]
    col = CONV_COLS
    for j in range(xpad_ref.shape[1] // col):
        cs = slice(j * col, (j + 1) * col)
        acc = cb_ref[:, cs] + jnp.zeros((cq, col), F32)
        for t in range(SSD_CONV):
            r0 = SUBLANES - (SSD_CONV - 1) + t
            acc = acc + xpad_ref[r0:r0 + cq, cs] * cw_ref[t:t + 1, cs]
        act_ref[:, cs] = _silu(acc)
    xpad_ref[0:SUBLANES, :] = xpad_ref[cq:cq + SUBLANES, :]

    xdt_in = dt_ref[:, 0:LANES] + dtb_ref[...]
    dt = jnp.maximum(xdt_in, 0.0) + jnp.log1p(jnp.exp(-jnp.abs(xdt_in)))
    da = dt * (-jnp.exp(alog_ref[...]))
    row = lax.broadcasted_iota(I32, (cq, cq), 0)
    coli = lax.broadcasted_iota(I32, (cq, cq), 1)
    causal = row >= coli
    tril = causal.astype(F32)
    acs = jnp.dot(tril, da, preferred_element_type=F32, precision=lax.Precision.HIGHEST)
    acs_t = acs.T
    dec_in = jnp.exp(acs)
    a_last = acs[cq - 1:cq, :]
    dec_out = jnp.exp(a_last - acs)
    dec_chunk = jnp.exp(a_last)

    gw = width // SSD_GROUPS
    for g in range(SSD_GROUPS):
        h_ref = h_refs[g]
        bg = act_ref[:, b_off + g * SSD_STATE:b_off + (g + 1) * SSD_STATE].astype(BF16)
        cg = act_ref[:, c_off + g * SSD_STATE:c_off + (g + 1) * SSD_STATE].astype(BF16)
        cbm = lax.dot_general(cg, bg, (((1,), (1,)), ((), ())), preferred_element_type=F32)
        hprevs = [h_ref[j] for j in range(SSD_HEADS_PER_GROUP)]
        ys, hnews = [], []
        for j in range(SSD_HEADS_PER_GROUP):
            hd = g * SSD_HEADS_PER_GROUP + j
            xs = slice(hd * SSD_HEAD_DIM, (hd + 1) * SSD_HEAD_DIM)
            seg = acs[:, hd:hd + 1] - acs_t[hd:hd + 1, :]
            lmat = jnp.exp(jnp.where(causal, seg, -jnp.inf))
            xh = act_ref[:, xs]
            xdt = xh * dt[:, hd:hd + 1]
            y = jnp.dot((cbm * lmat).astype(BF16), xdt.astype(BF16), preferred_element_type=F32)
            yoff = lax.dot_general(cg, hprevs[j].astype(BF16), (((1,), (1,)), ((), ())),
                                   preferred_element_type=F32)
            ys.append(y + yoff * dec_in[:, hd:hd + 1] + xh * dsk_ref[:, xs])
            st = lax.dot_general((xdt * dec_out[:, hd:hd + 1]).astype(BF16), bg,
                                 (((0,), (0,)), ((), ())), preferred_element_type=F32)
            hnews.append(hprevs[j] * dec_chunk[:, hd:hd + 1] + st)
        gs = slice(g * gw, (g + 1) * gw)
        gated = jnp.concatenate(ys, axis=1) * _silu(z_ref[:, gs])
        ms = jnp.mean(jnp.square(gated), axis=-1, keepdims=True)
        o_ref[:, gs] = (gated * lax.rsqrt(ms + RMS_EPS) * ng_ref[:, gs]).astype(o_ref.dtype)
        for j in range(SSD_HEADS_PER_GROUP):
            h_ref[j] = hnews[j]


def _ssd_mixer(ssd_in, z_off, dt_off, dt_w, conv_w, conv_b, dt_bias, a_log, d_skip, norm_g, bsz, seq):
    nc = seq // SSD_CHUNK
    width = SSD_HEADS * SSD_HEAD_DIM
    xbc_w = width + 2 * SSD_GROUPS * SSD_STATE
    assert z_off % width == 0 and dt_off % dt_w == 0 and dt_w >= LANES
    pad = LANES - SSD_HEADS
    dtb = jnp.pad(dt_bias.astype(F32), (0, pad)).reshape(1, LANES)
    alog = jnp.pad(a_log.astype(F32), (0, pad)).reshape(1, LANES)
    dsk = jnp.repeat(d_skip.astype(F32), SSD_HEAD_DIM).reshape(1, width)
    const = lambda b, c: (0, 0)
    return pl.pallas_call(
        _ssd_kernel,
        grid=(bsz, nc),
        in_specs=[pl.BlockSpec((SSD_CHUNK, xbc_w), lambda b, c: (b * nc + c, 0)),
                  pl.BlockSpec((SSD_CHUNK, width), lambda b, c: (b * nc + c, z_off // width)),
                  pl.BlockSpec((SSD_CHUNK, dt_w), lambda b, c: (b * nc + c, dt_off // dt_w)),
                  pl.BlockSpec((SSD_CONV, xbc_w), const),
                  pl.BlockSpec((1, xbc_w), const),
                  pl.BlockSpec((1, LANES), const),
                  pl.BlockSpec((1, LANES), const),
                  pl.BlockSpec((1, width), const),
                  pl.BlockSpec((1, width), const)],
        out_specs=pl.BlockSpec((SSD_CHUNK, width), lambda b, c: (b * nc + c, 0)),
        out_shape=jax.ShapeDtypeStruct((bsz * seq, width), BF16),
        scratch_shapes=[pltpu.VMEM((SSD_CHUNK + SUBLANES, xbc_w), F32),
                        pltpu.VMEM((SSD_CHUNK, xbc_w), F32)]
        + [pltpu.VMEM((SSD_HEADS_PER_GROUP, SSD_HEAD_DIM, SSD_STATE), F32) for _ in range(SSD_GROUPS)],
        compiler_params=_params(("parallel", "arbitrary")),
        name="ssd_mixer",
    )(ssd_in, ssd_in, ssd_in, conv_w.astype(F32), conv_b.astype(F32).reshape(1, xbc_w), dtb, alog, dsk,
      norm_g.astype(F32).reshape(1, width))


def _layer_norm_rows(x, g, b):
    mu = jnp.mean(x, axis=-1, keepdims=True)
    var = jnp.mean(jnp.square(x - mu), axis=-1, keepdims=True)
    return (x - mu) * lax.rsqrt(var + LN_EPS) * g + b


def _ln_router_kernel(x_ref, mix_ref, g_ref, b_ref, wr_ref, br_ref, o_ref, xp_ref, r_ref, *, alpha):
    x1 = _layer_norm_rows(alpha * x_ref[...] + mix_ref[...], g_ref[...], b_ref[...])
    o_ref[...] = x1
    half = xp_ref.shape[1]
    bits = pltpu.bitcast(x1.astype(BF16).astype(F32), jnp.uint32)
    xp_ref[...] = (bits[:, :half] >> 16) | bits[:, half:]
    logits = jnp.dot(x1, wr_ref[...], preferred_element_type=F32, precision=lax.Precision.HIGHEST)
    logits = logits + br_ref[...]
    rows = logits.shape[0]
    lane = lax.broadcasted_iota(I32, (rows, LANES), 1).astype(F32)
    ng, epg = float(N_EXPERT_GROUPS), float(EXPERTS_PER_GROUP)
    far = float(LANES)

    gmask = lane < ng
    gl = jnp.where(gmask, logits, -jnp.inf)
    ge = jnp.exp(gl - jnp.max(gl, axis=1, keepdims=True))
    gprob = ge / jnp.sum(ge, axis=1, keepdims=True)
    gprob = jnp.where(gmask, gprob, -1.0)
    gw = jnp.max(gprob, axis=1, keepdims=True)
    gsel = jnp.min(jnp.where(gprob == gw, lane, far), axis=1, keepdims=True)

    e_lo = ng + gsel * epg
    emask = (lane >= e_lo) & (lane < e_lo + epg)
    el = jnp.where(emask, logits, -jnp.inf)
    v0 = jnp.max(el, axis=1, keepdims=True)
    i0 = jnp.min(jnp.where(emask & (el == v0), lane, far), axis=1, keepdims=True)
    emask1 = emask & (lane != i0)
    el1 = jnp.where(emask1, logits, -jnp.inf)
    v1 = jnp.max(el1, axis=1, keepdims=True)
    i1 = jnp.min(jnp.where(emask1 & (el1 == v1), lane, far), axis=1, keepdims=True)
    e = jnp.exp(v1 - v0)
    p0 = 1.0 / (1.0 + e)
    p1 = e / (1.0 + e)
    out = jnp.where(lane == 0.0, i0 - ng,
                    jnp.where(lane == 1.0, i1 - ng,
                              jnp.where(lane == 2.0, gw * p0, jnp.where(lane == 3.0, gw * p1, 0.0))))
    r_ref[...] = out


def _ln_router(x, mixed, g, b, w_rg, b_rg, w_re, b_re, alpha, tr=LN_ROWS):
    t, d = x.shape
    ncol = N_EXPERT_GROUPS + N_EXPERTS
    wr = jnp.pad(jnp.concatenate([w_rg, w_re], axis=1).astype(F32), ((0, 0), (0, LANES - ncol)))
    br = jnp.pad(jnp.concatenate([b_rg, b_re]).astype(F32), (0, LANES - ncol)).reshape(1, LANES)
    const = lambda i: (0, 0)
    return pl.pallas_call(
        functools.partial(_ln_router_kernel, alpha=alpha),
        grid=(t // tr,),
        in_specs=[pl.BlockSpec((tr, d), lambda i: (i, 0)),
                  pl.BlockSpec((tr, d), lambda i: (i, 0)),
                  pl.BlockSpec((1, d), const), pl.BlockSpec((1, d), const),
                  pl.BlockSpec((d, LANES), const), pl.BlockSpec((1, LANES), const)],
        out_specs=[pl.BlockSpec((tr, d), lambda i: (i, 0)),
                   pl.BlockSpec((tr, d // 2), lambda i: (i, 0)),
                   pl.BlockSpec((tr, LANES), lambda i: (i, 0))],
        out_shape=[jax.ShapeDtypeStruct((t, d), F32), jax.ShapeDtypeStruct((t, d // 2), jnp.uint32),
                   jax.ShapeDtypeStruct((t, LANES), F32)],
        compiler_params=_params(("parallel",)),
        name="ln1_router",
    )(x, mixed, g.astype(F32).reshape(1, d), b.astype(F32).reshape(1, d), wr, br)


def _row_copy(src_ref, dst_ref, sem, tok, row):
    return pltpu.make_async_copy(src_ref.at[pl.ds(tok, 1)], dst_ref.at[pl.ds(row, 1)], sem)


def _moe_kernel(tok_ref, be_ref, bv_ref, nu_ref, x_ref, wg_ref, wu_ref, wd_ref, o_ref,
                xq_ref, xbf_ref, act_ref, gbuf_ref, dbuf_ref, sem, gsem, dsem, *, nf, nn):
    s = pl.program_id(0)
    p = pl.program_id(1)
    n_blk = pl.num_programs(0)
    n_used = nu_ref[0]
    steps = nf + nn
    sub = MOE_SUB
    n_sub = MOE_ROWS // sub
    half = xq_ref.shape[1]
    ks = MOE_K_SPLIT
    kq = wg_ref.shape[1] // ks

    def weights_dma(blk, ph, start):
        e = be_ref[blk]

        @pl.when(ph < nf)
        def _():
            slot = lax.rem(blk * nf + ph, MOE_RING)
            col = pl.multiple_of(ph * MOE_F_CHUNK, MOE_F_CHUNK)
            for q in range(ks):
                rows = pl.ds(q * kq, kq)
                for j, w_ref in enumerate((wg_ref, wu_ref)):
                    cp = pltpu.make_async_copy(w_ref.at[e, rows, pl.ds(col, MOE_F_CHUNK)],
                                               gbuf_ref.at[slot, j, rows], gsem.at[slot])
                    cp.start(priority=WEIGHT_DMA_PRIORITY) if start else cp.wait()

        @pl.when(ph >= nf)
        def _():
            k = ph - nf
            slot = lax.rem(blk * nn + k, MOE_RING)
            for j in range(2):
                col = pl.multiple_of((k + j * nn) * MOE_N_CHUNK, MOE_N_CHUNK)
                cp = pltpu.make_async_copy(wd_ref.at[e, :, pl.ds(col, MOE_N_CHUNK)],
                                           dbuf_ref.at[slot, j], dsem.at[slot])
                cp.start(priority=WEIGHT_DMA_PRIORITY) if start else cp.wait()

    @pl.when((s == 0) & (p == 0))
    def _():
        for a in range(MOE_LOOKAHEAD):
            weights_dma(s, p + a, True)

    ahead = p + MOE_LOOKAHEAD
    blk_a = jnp.where(ahead >= steps, s + 1, s)
    ph_a = jnp.where(ahead >= steps, ahead - steps, ahead)

    @pl.when(blk_a < n_used)
    def _():
        weights_dma(blk_a, ph_a, True)

    @pl.when(s < n_used)
    def _():
        weights_dma(s, p, False)

    def nsub_of(blk):
        return (bv_ref[blk] + (sub - 1)) // sub

    def issue(blk, r):
        base = blk * MOE_ROWS + r * sub

        def body(i, carry):
            for u in range(DMA_UNROLL):
                j = i * DMA_UNROLL + u
                _row_copy(x_ref, xq_ref, sem, tok_ref[base + j], r * sub + j).start()
            return carry
        lax.fori_loop(0, sub // DMA_UNROLL, body, 0)

    nsub = nsub_of(s)

    @pl.when((s == 0) & (p == 0))
    def _():
        for r in range(n_sub):
            @pl.when(r < nsub)
            def _(r=r):
                issue(s, r)

    @pl.when(p == 0)
    def _():
        for r in range(n_sub):
            @pl.when(r < nsub)
            def _(r=r):
                pltpu.make_async_copy(x_ref.at[pl.ds(0, sub)], xq_ref.at[pl.ds(r * sub, sub)], sem).wait()
        for r in range(n_sub):
            @pl.when(r < nsub)
            def _(r=r):
                rs = slice(r * sub, (r + 1) * sub)
                word = xq_ref[rs, :]
                xbf_ref[rs, 0:half] = pltpu.bitcast(word << 16, F32).astype(BF16)
                xbf_ref[rs, half:2 * half] = pltpu.bitcast(word & jnp.uint32(0xFFFF0000), F32).astype(BF16)

    nxt = jnp.minimum(s + 1, n_blk - 1)
    for r in range(n_sub):
        @pl.when((p == r + 1) & (s + 1 < n_blk) & (r < nsub_of(nxt)))
        def _(r=r):
            issue(nxt, r)

    for k in range(1, n_sub + 1):
        m = k * sub

        @pl.when((p < nf) & (nsub == k))
        def _(m=m):
            slot = lax.rem(s * nf + p, MOE_RING)
            gate = up = None
            for q in range(ks):
                xr = xbf_ref[0:m, q * kq:(q + 1) * kq]
                gq = jnp.dot(xr, gbuf_ref[slot, 0, q * kq:(q + 1) * kq, :].astype(BF16), preferred_element_type=F32)
                uq = jnp.dot(xr, gbuf_ref[slot, 1, q * kq:(q + 1) * kq, :].astype(BF16), preferred_element_type=F32)
                gate = gq if gate is None else gate + gq
                up = uq if up is None else up + uq
            act_ref[p, 0:m, :] = (_silu(gate) * up).astype(BF16)

    def down(slot, j, m):
        acc = None
        for q in range(nf):
            wq = dbuf_ref[slot, j, q * MOE_F_CHUNK:(q + 1) * MOE_F_CHUNK, :].astype(BF16)
            t = jnp.dot(act_ref[q, 0:m, :], wq, preferred_element_type=F32)
            acc = t if acc is None else acc + t
        return pltpu.bitcast(acc.astype(BF16).astype(F32), jnp.uint32)

    for k in range(0, n_sub + 1):
        m = k * sub

        @pl.when((p >= nf) & (nsub == k))
        def _(m=m):
            if m > 0:
                slot = lax.rem(s * nn + p - nf, MOE_RING)
                o_ref[0:m, :] = (down(slot, 0, m) >> 16) | down(slot, 1, m)
            if m < MOE_ROWS:
                o_ref[m:MOE_ROWS, :] = jnp.zeros((MOE_ROWS - m, o_ref.shape[1]), jnp.uint32)


def _moe_ffn(xp, row_tok, blk_e, blk_valid, n_used, w_gate, w_up, w_down):
    n_rows = row_tok.shape[0]
    n_e, d, f = w_gate.shape
    nf = f // MOE_F_CHUNK
    nn = (d // 2) // MOE_N_CHUNK
    n_blk = n_rows // MOE_ROWS
    assert nf + nn > MOE_ROWS // MOE_SUB and d % MOE_K_SPLIT == 0
    assert MOE_LOOKAHEAD < MOE_RING and MOE_LOOKAHEAD <= min(nf, nn)

    def out_map(s, p, tok, be, bv, nu):
        return (s, jnp.maximum(p - nf, 0))

    hbm = pl.BlockSpec(memory_space=pl.ANY)
    return pl.pallas_call(
        functools.partial(_moe_kernel, nf=nf, nn=nn),
        grid_spec=pltpu.PrefetchScalarGridSpec(
            num_scalar_prefetch=4,
            grid=(n_blk, nf + nn),
            in_specs=[hbm, hbm, hbm, hbm],
            out_specs=pl.BlockSpec((MOE_ROWS, MOE_N_CHUNK), out_map),
            scratch_shapes=[pltpu.VMEM((MOE_ROWS, d // 2), jnp.uint32),
                            pltpu.VMEM((MOE_ROWS, d), BF16),
                            pltpu.VMEM((nf, MOE_ROWS, MOE_F_CHUNK), BF16),
                            pltpu.VMEM((MOE_RING, 2, d, MOE_F_CHUNK), F32),
                            pltpu.VMEM((MOE_RING, 2, f, MOE_N_CHUNK), F32),
                            pltpu.SemaphoreType.DMA(()),
                            pltpu.SemaphoreType.DMA((MOE_RING,)),
                            pltpu.SemaphoreType.DMA((MOE_RING,))]),
        out_shape=jax.ShapeDtypeStruct((n_rows, d // 2), jnp.uint32),
        compiler_params=_params(("arbitrary", "arbitrary")),
        name="moe_ffn",
    )(row_tok, blk_e, blk_valid, n_used, xp, w_gate, w_up, w_down)


def _combine_kernel(dest_ref, y_ref, x_ref, r_ref, g_ref, b_ref, o_ref, buf_ref, sem, *, alpha, tt):
    s = pl.program_id(0)
    n = pl.num_programs(0)

    rows = 2 * tt

    def start(blk, slot):
        base = blk * rows

        def body(i, carry):
            for u in range(DMA_UNROLL):
                r = i * DMA_UNROLL + u
                pltpu.make_async_copy(y_ref.at[pl.ds(dest_ref[base + r], 1)],
                                      buf_ref.at[slot, pl.ds(r, 1)], sem.at[slot]).start()
            return carry
        lax.fori_loop(0, rows // DMA_UNROLL, body, 0)

    @pl.when(s == 0)
    def _():
        start(s, 0)

    for par in range(2):
        @pl.when((s % 2 == par) & (s + 1 < n))
        def _(par=par):
            start(s + 1, 1 - par)

    slot = s % 2
    pltpu.make_async_copy(y_ref.at[pl.ds(0, rows)], buf_ref.at[slot], sem.at[slot]).wait()
    g0 = r_ref[:, 2:3]
    g1 = r_ref[:, 3:4]
    w0 = buf_ref[slot, 0:tt, :]
    w1 = buf_ref[slot, tt:2 * tt, :]
    half = w0.shape[1]
    himask = jnp.uint32(0xFFFF0000)
    y_lo = g0 * pltpu.bitcast(w0 << 16, F32) + g1 * pltpu.bitcast(w1 << 16, F32)
    y_hi = g0 * pltpu.bitcast(w0 & himask, F32) + g1 * pltpu.bitcast(w1 & himask, F32)
    r_lo = alpha * x_ref[:, 0:half] + y_lo
    r_hi = alpha * x_ref[:, half:2 * half] + y_hi
    inv_d = 1.0 / (2 * half)
    mu = (jnp.sum(r_lo, axis=-1, keepdims=True) + jnp.sum(r_hi, axis=-1, keepdims=True)) * inv_d
    var = (jnp.sum(jnp.square(r_lo - mu), axis=-1, keepdims=True)
           + jnp.sum(jnp.square(r_hi - mu), axis=-1, keepdims=True)) * inv_d
    rstd = lax.rsqrt(var + LN_EPS)
    o_ref[:, 0:half] = (r_lo - mu) * rstd * g_ref[:, 0:half] + b_ref[:, 0:half]
    o_ref[:, half:2 * half] = (r_hi - mu) * rstd * g_ref[:, half:2 * half] + b_ref[:, half:2 * half]


def _combine_ln(y_rows, dest, x1, route, g, b, alpha, tt):
    t, d = x1.shape
    const = lambda i, dr: (0, 0)
    return pl.pallas_call(
        functools.partial(_combine_kernel, alpha=alpha, tt=tt),
        grid_spec=pltpu.PrefetchScalarGridSpec(
            num_scalar_prefetch=1,
            grid=(t // tt,),
            in_specs=[pl.BlockSpec(memory_space=pl.ANY),
                      pl.BlockSpec((tt, d), lambda i, dr: (i, 0)),
                      pl.BlockSpec((tt, LANES), lambda i, dr: (i, 0)),
                      pl.BlockSpec((1, d), const), pl.BlockSpec((1, d), const)],
            out_specs=pl.BlockSpec((tt, d), lambda i, dr: (i, 0)),
            scratch_shapes=[pltpu.VMEM((2, 2 * tt, d // 2), jnp.uint32),
                            pltpu.SemaphoreType.DMA((2,))]),
        out_shape=jax.ShapeDtypeStruct((t, d), F32),
        compiler_params=_params(("arbitrary",)),
        name="moe_combine_ln2",
    )(dest, y_rows, x1, route, g.astype(F32).reshape(1, d), b.astype(F32).reshape(1, d))


def _plan_kernel(r_ref, dest_ref, meta_ref, cnt_ref, pstart_ref, carry_ref):
    ph = pl.program_id(0)
    i = pl.program_id(1)
    ts = r_ref.shape[0]
    rows = float(MOE_ROWS)
    lane = lax.broadcasted_iota(I32, (ts, LANES), 1).astype(F32)
    oh0 = jnp.where(lane == r_ref[:, 0:1], 1.0, 0.0)
    oh1 = jnp.where(lane == r_ref[:, 1:2], 1.0, 0.0)
    oh = oh0 + oh1

    @pl.when((ph == 0) & (i == 0))
    def _():
        cnt_ref[...] = jnp.zeros(cnt_ref.shape, F32)

    @pl.when(ph == 0)
    def _():
        cnt_ref[...] += jnp.sum(oh, axis=0, keepdims=True)

    @pl.when((ph == 0) & (i == pl.num_programs(1) - 1))
    def _():
        sq = (LANES, LANES)
        r_i = lax.broadcasted_iota(I32, sq, 0)
        c_i = lax.broadcasted_iota(I32, sq, 1)
        counts = jnp.broadcast_to(cnt_ref[...], sq)
        nblk_e = jnp.floor((counts + (rows - 0.5)) * (1.0 / rows))
        upper = jnp.where(r_i <= c_i, 1.0, 0.0)
        pend = jnp.dot(nblk_e.astype(BF16), upper.astype(BF16), preferred_element_type=F32)
        pstart = pend - nblk_e
        n_used = jnp.max(pend, axis=1, keepdims=True)
        pstart_ref[...] = pstart[0:1, :] * rows
        carry_ref[...] = jnp.zeros(carry_ref.shape, F32)
        b_eff = jnp.minimum(c_i.astype(F32), n_used - 1.0)
        pend_t, pstart_t, counts_t = pend.T, pstart.T, counts.T
        blk_e = jnp.sum(jnp.where(pend_t <= b_eff, 1.0, 0.0), axis=0, keepdims=True)
        blk_e = jnp.minimum(blk_e, float(N_EXPERTS - 1))
        pick = r_i.astype(F32) == blk_e
        cnt_b = jnp.sum(jnp.where(pick, counts_t, 0.0), axis=0, keepdims=True)
        first_b = jnp.sum(jnp.where(pick, pstart_t, 0.0), axis=0, keepdims=True)
        valid = jnp.clip(cnt_b - (b_eff[0:1, :] - first_b) * rows, 0.0, rows)
        valid = jnp.where(c_i[0:1, :].astype(F32) < n_used[0:1, :], valid, 0.0)
        sub_i = lax.broadcasted_iota(I32, meta_ref.shape, 0)
        meta = jnp.where(sub_i == 0, blk_e, jnp.where(sub_i == 1, valid, jnp.where(sub_i == 2, n_used[0:1, :], 0.0)))
        meta_ref[...] = meta.astype(I32)

    @pl.when(ph == 1)
    def _():
        below = lax.broadcasted_iota(I32, (ts, ts), 0) > lax.broadcasted_iota(I32, (ts, ts), 1)
        earlier = jnp.dot(below.astype(BF16), oh.astype(BF16), preferred_element_type=F32)
        base = pstart_ref[...] + carry_ref[...] + earlier
        d0 = jnp.sum(base * oh0, axis=1, keepdims=True)
        d1 = jnp.sum(base * oh1, axis=1, keepdims=True)
        carry_ref[...] += jnp.sum(oh, axis=0, keepdims=True)
        both = jnp.where(lane == 0.0, d0, jnp.where(lane == 1.0, d1, 0.0))
        dest_ref[...] = both.T[0:2, :].astype(I32)


def _dispatch_plan(route, tt):
    t = route.shape[0]
    m = 2 * t
    n_blk = -(-m // MOE_ROWS) + N_EXPERTS
    n_rows = n_blk * MOE_ROWS
    ts = min(PLAN_TILE, t)
    assert n_blk <= LANES and t % ts == 0 and N_EXPERTS <= LANES
    dest2, meta = pl.pallas_call(
        _plan_kernel,
        grid=(2, t // ts),
        in_specs=[pl.BlockSpec((ts, LANES), lambda ph, i: (i, 0))],
        out_specs=[pl.BlockSpec((2, ts), lambda ph, i: (0, i * ph)),
                   pl.BlockSpec((SUBLANES, LANES), lambda ph, i: (0, 0))],
        out_shape=[jax.ShapeDtypeStruct((2, t), I32), jax.ShapeDtypeStruct((SUBLANES, LANES), I32)],
        scratch_shapes=[pltpu.VMEM((1, LANES), F32), pltpu.VMEM((1, LANES), F32), pltpu.VMEM((1, LANES), F32)],
        compiler_params=_params(("arbitrary", "arbitrary")),
        name="moe_plan",
    )(route)
    tok = jnp.arange(t, dtype=I32)
    row_tok = jnp.zeros((n_rows,), I32).at[dest2.reshape(m)].set(jnp.concatenate([tok, tok]))
    dest_tiles = dest2.reshape(2, t // tt, tt).transpose(1, 0, 2).reshape(m)
    return row_tok, meta[0, :n_blk], meta[1, :n_blk], meta[2, 0:1], dest_tiles


def kernel(x, w_in, idx_kn_g, idx_kn_b, conv_w, conv_b, dt_bias, a_log, d_skip, ssd_norm_g, w_out,
           ln1_g, ln1_b, w_rg, b_rg, w_re, b_re, w_gate, w_up, w_down, ln2_g, ln2_b):
    bsz, seq, d = x.shape
    depth = w_in.shape[0]
    alpha = (2 * depth) ** 0.25
    att_w = ATT_HEADS * HEAD_DIM
    kv_w = KV_HEADS * HEAD_DIM
    qi_w = IDX_HEADS * IDX_DIM
    ssd_w = SSD_HEADS * SSD_HEAD_DIM
    xbc_w = ssd_w + 2 * SSD_GROUPS * SSD_STATE
    sizes = (att_w, kv_w, kv_w, qi_w, IDX_DIM, IDX_HEADS, ssd_w, xbc_w, SSD_HEADS)
    offs = [0]
    for sz in sizes:
        offs.append(offs[-1] + sz)
    tt = COMBINE_TOKENS
    xf = x.reshape(bsz * seq, d)
    for l in range(depth):
        col = lambda a, b: w_in[l][:, offs[a]:offs[b]].astype(BF16)
        zpad = lambda n: jnp.zeros((d, n), BF16)
        dt_w = 2 * LANES
        z_off, qi_off = xbc_w, xbc_w + ssd_w
        dt_off = qi_off + qi_w
        kw_off = dt_off + dt_w
        w_rest = jnp.concatenate([col(7, 8), col(6, 7), col(3, 4), col(8, 9), zpad(dt_w - SSD_HEADS),
                                  col(4, 6), zpad(LANES - IDX_DIM - IDX_HEADS), zpad(LANES)], axis=1)
        qkv, xbf = _matmul(xf, col(0, 3), BF16, MM_ROWS_F32, MM_COLS)
        rest = _matmul(xbf, w_rest, F32, MM_ROWS, MM_COLS_REST)
        att = _dsa_attention(qkv, rest, qi_off, kw_off, idx_kn_g[l], idx_kn_b[l], bsz, seq)
        ssd = _ssd_mixer(rest, z_off, dt_off, dt_w, conv_w[l], conv_b[l], dt_bias[l], a_log[l], d_skip[l],
                         ssd_norm_g[l], bsz, seq)
        mixed = _matmul_pair(att, ssd, w_out[l].astype(BF16), F32, MM_ROWS, MM_COLS)
        x1, x1p, route = _ln_router(xf, mixed, ln1_g[l], ln1_b[l], w_rg[l], b_rg[l], w_re[l], b_re[l], alpha)
        row_tok, blk_e, blk_valid, n_used, dest_tiles = _dispatch_plan(route, tt)
        y_rows = _moe_ffn(x1p, row_tok, blk_e, blk_valid, n_used, w_gate[l], w_up[l], w_down[l])
        xf = _combine_ln(y_rows, dest_tiles, x1, route, ln2_g[l], ln2_b[l], alpha, tt)
    return xf.reshape(bsz, seq, d)
```

```python
import functools

import jax
import jax.numpy as jnp
from jax import lax
from jax.experimental import pallas as pl
from jax.experimental.pallas import tpu as pltpu

F32 = jnp.float32
BF16 = jnp.bfloat16
I32 = jnp.int32

HEAD_DIM = 128
KV_HEADS = 4
GQA_GROUP = 4
ATT_HEADS = KV_HEADS * GQA_GROUP
IDX_HEADS = 16
IDX_DIM = 64
DSA_TOPK_MAX = 256
QUERY_BLOCK = 128
SSD_HEAD_DIM = 64
SSD_GROUPS = 8
SSD_HEADS_PER_GROUP = 4
SSD_HEADS = SSD_GROUPS * SSD_HEADS_PER_GROUP
SSD_STATE = 128
SSD_CONV = 4
SSD_CHUNK = 128
N_EXPERT_GROUPS = 8
EXPERTS_PER_GROUP = 8
N_EXPERTS = 64
LN_EPS = 1e-5
RMS_EPS = 1e-5

LANES = 128
SUBLANES = 8
VMEM_LIMIT = 56 * 1024 * 1024

KEY_CHUNK = 1024
MOE_ROWS = 768
MOE_SUB = 256
MOE_F_CHUNK = 256
MOE_N_CHUNK = 512
MOE_K_SPLIT = 4
MOE_RING = 3
MOE_LOOKAHEAD = 2
WEIGHT_DMA_PRIORITY = 1
DMA_UNROLL = 8
PLAN_TILE = 512
MM_ROWS_F32 = 512
MM_ROWS = 1024
MM_COLS = 1024
MM_COLS_REST = 512
LN_ROWS = 256
COMBINE_TOKENS = 256
CONV_COLS = 512
NEG_BIG = -1e30
LOG2_E = 1.4426950408889634
INT_MIN = -2 ** 31
NEG_INF_KEY = -2139095041


def _params(sem):
    return pltpu.CompilerParams(dimension_semantics=sem, vmem_limit_bytes=VMEM_LIMIT)


def _mm_cast_kernel(a_ref, b_ref, o_ref, abf_ref):
    @pl.when(pl.program_id(1) == 0)
    def _():
        abf_ref[...] = a_ref[...].astype(BF16)

    o_ref[...] = jnp.dot(abf_ref[...], b_ref[...], preferred_element_type=F32).astype(o_ref.dtype)


def _mm_kernel(a_ref, b_ref, o_ref):
    o_ref[...] = jnp.dot(a_ref[...], b_ref[...], preferred_element_type=F32).astype(o_ref.dtype)


def _mm_pair_kernel(a1_ref, a2_ref, b_ref, o_ref):
    k1 = a1_ref.shape[1]
    acc = jnp.dot(a1_ref[...], b_ref[0:k1, :], preferred_element_type=F32)
    acc = acc + jnp.dot(a2_ref[...], b_ref[k1:, :], preferred_element_type=F32)
    o_ref[...] = acc.astype(o_ref.dtype)


def _matmul_pair(a1, a2, b, out_dtype, tm, tn):
    m, k1 = a1.shape
    k2 = a2.shape[1]
    n = b.shape[1]
    tm = min(tm, m)
    assert m % tm == 0 and n % tn == 0 and b.shape[0] == k1 + k2
    return pl.pallas_call(
        _mm_pair_kernel,
        grid=(m // tm, n // tn),
        in_specs=[pl.BlockSpec((tm, k1), lambda i, j: (i, 0)),
                  pl.BlockSpec((tm, k2), lambda i, j: (i, 0)),
                  pl.BlockSpec((k1 + k2, tn), lambda i, j: (0, j))],
        out_specs=pl.BlockSpec((tm, tn), lambda i, j: (i, j)),
        out_shape=jax.ShapeDtypeStruct((m, n), out_dtype),
        compiler_params=_params(("parallel", "arbitrary")),
        name="matmul_pair",
    )(a1, a2, b)


def _matmul(a, b, out_dtype, tm, tn):
    m, k = a.shape
    n = b.shape[1]
    tm = min(tm, m)
    assert m % tm == 0 and n % tn == 0
    cast = a.dtype != BF16
    out_specs = pl.BlockSpec((tm, tn), lambda i, j: (i, j))
    out_shape = jax.ShapeDtypeStruct((m, n), out_dtype)
    if cast:
        out_specs = [out_specs, pl.BlockSpec((tm, k), lambda i, j: (i, 0))]
        out_shape = [out_shape, jax.ShapeDtypeStruct((m, k), BF16)]
    return pl.pallas_call(
        _mm_cast_kernel if cast else _mm_kernel,
        grid=(m // tm, n // tn),
        in_specs=[pl.BlockSpec((tm, k), lambda i, j: (i, 0)),
                  pl.BlockSpec((k, tn), lambda i, j: (0, j))],
        out_specs=out_specs,
        out_shape=out_shape,
        compiler_params=_params(("parallel", "arbitrary")),
        name="matmul_cast" if cast else "matmul",
    )(a, b)


def _attn_kernel(q_ref, k_ref, v_ref, qi_ref, kw_ref, g_ref, b_ref, o_ref,
                 kln_ref, key_ref, bias_ref, s_ref, mrun_ref, lrun_ref, acc_ref, *, top_k):
    i = pl.program_id(1)
    tq = QUERY_BLOCK
    ck = KEY_CHUNK

    @pl.when(i == 0)
    def _():
        kx = kw_ref[:, 0:IDX_DIM]
        mu = jnp.mean(kx, axis=-1, keepdims=True)
        var = jnp.mean(jnp.square(kx - mu), axis=-1, keepdims=True)
        y = (kx - mu) * lax.rsqrt(var + LN_EPS)
        kln_ref[...] = (y * g_ref[...] + b_ref[...]).astype(BF16)

    q_start = i * tq
    n_chunks = (q_start + tq + ck - 1) // ck
    q_pos = q_start + lax.broadcasted_iota(I32, (tq, 1), 0)
    w = kw_ref[pl.ds(pl.multiple_of(q_start, tq), tq), IDX_DIM:IDX_DIM + IDX_HEADS]
    w = w * (IDX_HEADS ** -0.5 * IDX_DIM ** -0.5)
    qi = jnp.concatenate([qi_ref[:, h * IDX_DIM:(h + 1) * IDX_DIM] for h in range(IDX_HEADS)], axis=0).astype(BF16)

    def chunk_off(c):
        return pl.multiple_of(c * ck, ck)

    def key_pos(c):
        return c * ck + lax.broadcasted_iota(I32, (1, ck), 1)

    def score_chunk(c, carry):
        off = chunk_off(c)
        kc = kln_ref[pl.ds(off, ck), :]
        d = lax.dot_general(qi, kc, (((1,), (1,)), ((), ())), preferred_element_type=F32)
        acc = jnp.zeros((tq, ck), F32)
        for h in range(IDX_HEADS):
            acc = acc + jnp.maximum(d[h * tq:(h + 1) * tq, :], 0.0) * w[:, h:h + 1]
        acc = jnp.where(key_pos(c) <= q_pos, acc, -jnp.inf)
        bits = pltpu.bitcast(acc, I32)
        key_ref[:, pl.ds(off, ck)] = bits ^ ((bits >> 31) & 0x7FFFFFFF)
        return carry

    lax.fori_loop(0, n_chunks, score_chunk, 0)

    def bit_body(b, carry):
        cand, cnt_cand = carry
        trial = cand | lax.shift_left(jnp.int32(1), jnp.int32(31) - jnp.asarray(b, I32))
        trial_b = jnp.broadcast_to(trial ^ INT_MIN, (tq, LANES))

        def cnt_chunk(c, cnt):
            kc = key_ref[:, pl.ds(chunk_off(c), ck)]
            for s in range(ck // LANES):
                cnt = cnt + jnp.where(kc[:, s * LANES:(s + 1) * LANES] >= trial_b, 1.0, 0.0)
            return cnt

        cnt = lax.fori_loop(0, n_chunks, cnt_chunk, jnp.zeros((tq, LANES), F32))
        total = jnp.sum(cnt, axis=1, keepdims=True)
        ok = total >= float(top_k)
        return jnp.where(ok, trial, cand), jnp.where(ok, total, cnt_cand)

    n_keys = (jnp.zeros((tq, 1), I32) + n_chunks * ck).astype(F32)
    cand, cnt_ge = lax.fori_loop(0, 32, bit_body, (jnp.zeros((tq, 1), I32), n_keys))
    thr = cand ^ INT_MIN

    def bias_chunk(c, carry):
        off = chunk_off(c)
        sel = (key_ref[:, pl.ds(off, ck)] >= thr) & (key_pos(c) <= q_pos)
        bias_ref[:, pl.ds(off, ck)] = jnp.where(sel, 0.0, NEG_BIG)
        return carry

    lax.fori_loop(0, n_chunks, bias_chunk, 0)

    tie = (cnt_ge > float(top_k)) & (thr > NEG_INF_KEY)

    @pl.when(jnp.max(jnp.where(tie, 1.0, 0.0)) > 0.5)
    def _():
        tri = (lax.broadcasted_iota(I32, (ck, ck), 0) <= lax.broadcasted_iota(I32, (ck, ck), 1)).astype(BF16)

        def gt_chunk(c, cnt):
            kc = key_ref[:, pl.ds(chunk_off(c), ck)]
            return cnt + jnp.sum(jnp.where(kc > thr, 1.0, 0.0), axis=1, keepdims=True)

        need = float(top_k) - lax.fori_loop(0, n_chunks, gt_chunk, jnp.zeros((tq, 1), F32))

        def tie_chunk(c, seen):
            off = chunk_off(c)
            kc = key_ref[:, pl.ds(off, ck)]
            eq = jnp.where(kc == thr, 1.0, 0.0)
            rank = seen + jnp.dot(eq.astype(BF16), tri, preferred_element_type=F32)
            keep = (kc > thr) | ((kc == thr) & ((rank <= need) | jnp.logical_not(tie)))
            sel = keep & (key_pos(c) <= q_pos)
            bias_ref[:, pl.ds(off, ck)] = jnp.where(sel, 0.0, NEG_BIG)
            return seen + jnp.sum(eq, axis=1, keepdims=True)

        lax.fori_loop(0, n_chunks, tie_chunk, jnp.zeros((tq, 1), F32))

    scale = HEAD_DIM ** -0.5 * LOG2_E
    gq = GQA_GROUP
    for g in range(KV_HEADS):
        qg = jnp.concatenate([q_ref[:, (g * gq + j) * HEAD_DIM:(g * gq + j + 1) * HEAD_DIM] for j in range(gq)],
                             axis=0)

        mrun_ref[...] = jnp.full(mrun_ref.shape, NEG_BIG, F32)

        def logit_chunk(c, carry, g=g, qg=qg):
            off = chunk_off(c)
            kc = k_ref[pl.ds(off, ck), g * HEAD_DIM:(g + 1) * HEAD_DIM]
            s = lax.dot_general(qg, kc, (((1,), (1,)), ((), ())), preferred_element_type=F32)
            bias = bias_ref[:, pl.ds(off, ck)]
            s = s * scale + jnp.concatenate([bias] * gq, axis=0)
            s_ref[:, pl.ds(off, ck)] = s
            m = mrun_ref[...]
            for t in range(ck // LANES):
                m = jnp.maximum(m, s[:, t * LANES:(t + 1) * LANES])
            mrun_ref[...] = m
            return carry

        lax.fori_loop(0, n_chunks, logit_chunk, 0)
        m = jnp.max(mrun_ref[...], axis=1, keepdims=True)

        lrun_ref[...] = jnp.zeros(lrun_ref.shape, F32)
        acc_ref[...] = jnp.zeros(acc_ref.shape, F32)

        def prob_chunk(c, carry, g=g, m=m):
            off = chunk_off(c)
            vc = v_ref[pl.ds(off, ck), g * HEAD_DIM:(g + 1) * HEAD_DIM]
            p = jnp.exp2(s_ref[:, pl.ds(off, ck)] - m)
            l = lrun_ref[...]
            for t in range(ck // LANES):
                l = l + p[:, t * LANES:(t + 1) * LANES]
            lrun_ref[...] = l
            acc_ref[...] += jnp.dot(p.astype(BF16), vc, preferred_element_type=F32)
            return carry

        lax.fori_loop(0, n_chunks, prob_chunk, 0)
        out = acc_ref[...] / jnp.sum(lrun_ref[...], axis=1, keepdims=True)
        for j in range(gq):
            h = g * gq + j
            o_ref[:, h * HEAD_DIM:(h + 1) * HEAD_DIM] = out[j * tq:(j + 1) * tq, :].astype(o_ref.dtype)


def _dsa_attention(qkv, idx, qi_off, kw_off, kn_g, kn_b, bsz, seq):
    top_k = min(DSA_TOPK_MAX, seq // 4)
    nq = seq // QUERY_BLOCK
    att_w = ATT_HEADS * HEAD_DIM
    kv_w = KV_HEADS * HEAD_DIM
    qi_w = IDX_HEADS * IDX_DIM
    assert seq % KEY_CHUNK == 0 and att_w % kv_w == 0 and qi_off % qi_w == 0 and kw_off % LANES == 0
    return pl.pallas_call(
        functools.partial(_attn_kernel, top_k=top_k),
        grid=(bsz, nq),
        in_specs=[pl.BlockSpec((QUERY_BLOCK, att_w), lambda b, i: (b * nq + i, 0)),
                  pl.BlockSpec((seq, kv_w), lambda b, i: (b, att_w // kv_w)),
                  pl.BlockSpec((seq, kv_w), lambda b, i: (b, att_w // kv_w + 1)),
                  pl.BlockSpec((QUERY_BLOCK, qi_w), lambda b, i: (b * nq + i, qi_off // qi_w)),
                  pl.BlockSpec((seq, LANES), lambda b, i: (b, kw_off // LANES)),
                  pl.BlockSpec((1, IDX_DIM), lambda b, i: (0, 0)),
                  pl.BlockSpec((1, IDX_DIM), lambda b, i: (0, 0))],
        out_specs=pl.BlockSpec((QUERY_BLOCK, att_w), lambda b, i: (b * nq + i, 0)),
        out_shape=jax.ShapeDtypeStruct((bsz * seq, att_w), BF16),
        scratch_shapes=[pltpu.VMEM((seq, IDX_DIM), BF16),
                        pltpu.VMEM((QUERY_BLOCK, seq), I32),
                        pltpu.VMEM((QUERY_BLOCK, seq), F32),
                        pltpu.VMEM((GQA_GROUP * QUERY_BLOCK, seq), F32),
                        pltpu.VMEM((GQA_GROUP * QUERY_BLOCK, LANES), F32),
                        pltpu.VMEM((GQA_GROUP * QUERY_BLOCK, LANES), F32),
                        pltpu.VMEM((GQA_GROUP * QUERY_BLOCK, HEAD_DIM), F32)],
        compiler_params=_params(("parallel", "arbitrary")),
        name="dsa_attention",
    )(qkv, qkv, qkv, idx, idx, kn_g.reshape(1, IDX_DIM), kn_b.reshape(1, IDX_DIM))


def _silu(x):
    return x / (1.0 + jnp.exp(-x))


def _ssd_kernel(xbc_ref, z_ref, dt_ref, cw_ref, cb_ref, dtb_ref, alog_ref, dsk_ref, ng_ref, o_ref,
                xpad_ref, act_ref, *h_refs):
    c = pl.program_id(1)
    cq = SSD_CHUNK
    width = SSD_HEADS * SSD_HEAD_DIM
    b_off = width
    c_off = width + SSD_GROUPS * SSD_STATE

    @pl.when(c == 0)
    def _():
        xpad_ref[0:SUBLANES, :] = jnp.zeros((SUBLANES, xpad_ref.shape[1]), F32)
        for h_ref in h_refs:
            h_ref[...] = jnp.zeros(h_ref.shape, F32)

    xpad_ref[SUBLANES:SUBLANES + cq, :] = xbc_ref[...]
    col = CONV_COLS
    for j in range(xpad_ref.shape[1] // col):
        cs = slice(j * col, (j + 1) * col)
        acc = cb_ref[:, cs] + jnp.zeros((cq, col), F32)
        for t in range(SSD_CONV):
            r0 = SUBLANES - (SSD_CONV - 1) + t
            acc = acc + xpad_ref[r0:r0 + cq, cs] * cw_ref[t:t + 1, cs]
        act_ref[:, cs] = _silu(acc)
    xpad_ref[0:SUBLANES, :] = xpad_ref[cq:cq + SUBLANES, :]

    xdt_in = dt_ref[:, 0:LANES] + dtb_ref[...]
    dt = jnp.maximum(xdt_in, 0.0) + jnp.log1p(jnp.exp(-jnp.abs(xdt_in)))
    da = dt * (-jnp.exp(alog_ref[...]))
    row = lax.broadcasted_iota(I32, (cq, cq), 0)
    coli = lax.broadcasted_iota(I32, (cq, cq), 1)
    causal = row >= coli
    tril = causal.astype(F32)
    acs = jnp.dot(tril, da, preferred_element_type=F32, precision=lax.Precision.HIGHEST)
    acs_t = acs.T
    dec_in = jnp.exp(acs)
    a_last = acs[cq - 1:cq, :]
    dec_out = jnp.exp(a_last - acs)
    dec_chunk = jnp.exp(a_last)

    gw = width // SSD_GROUPS
    for g in range(SSD_GROUPS):
        h_ref = h_refs[g]
        bg = act_ref[:, b_off + g * SSD_STATE:b_off + (g + 1) * SSD_STATE].astype(BF16)
        cg = act_ref[:, c_off + g * SSD_STATE:c_off + (g + 1) * SSD_STATE].astype(BF16)
        cbm = lax.dot_general(cg, bg, (((1,), (1,)), ((), ())), preferred_element_type=F32)
        hprevs = [h_ref[j] for j in range(SSD_HEADS_PER_GROUP)]
        ys, hnews = [], []
        for j in range(SSD_HEADS_PER_GROUP):
            hd = g * SSD_HEADS_PER_GROUP + j
            xs = slice(hd * SSD_HEAD_DIM, (hd + 1) * SSD_HEAD_DIM)
            seg = acs[:, hd:hd + 1] - acs_t[hd:hd + 1, :]
            lmat = jnp.exp(jnp.where(causal, seg, -jnp.inf))
            xh = act_ref[:, xs]
            xdt = xh * dt[:, hd:hd + 1]
            y = jnp.dot((cbm * lmat).astype(BF16), xdt.astype(BF16), preferred_element_type=F32)
            yoff = lax.dot_general(cg, hprevs[j].astype(BF16), (((1,), (1,)), ((), ())),
                                   preferred_element_type=F32)
            ys.append(y + yoff * dec_in[:, hd:hd + 1] + xh * dsk_ref[:, xs])
            st = lax.dot_general((xdt * dec_out[:, hd:hd + 1]).astype(BF16), bg,
                                 (((0,), (0,)), ((), ())), preferred_element_type=F32)
            hnews.append(hprevs[j] * dec_chunk[:, hd:hd + 1] + st)
        gs = slice(g * gw, (g + 1) * gw)
        gated = jnp.concatenate(ys, axis=1) * _silu(z_ref[:, gs])
        ms = jnp.mean(jnp.square(gated), axis=-1, keepdims=True)
        o_ref[:, gs] = (gated * lax.rsqrt(ms + RMS_EPS) * ng_ref[:, gs]).astype(o_ref.dtype)
        for j in range(SSD_HEADS_PER_GROUP):
            h_ref[j] = hnews[j]


def _ssd_mixer(ssd_in, z_off, dt_off, dt_w, conv_w, conv_b, dt_bias, a_log, d_skip, norm_g, bsz, seq):
    nc = seq // SSD_CHUNK
    width = SSD_HEADS * SSD_HEAD_DIM
    xbc_w = width + 2 * SSD_GROUPS * SSD_STATE
    assert z_off % width == 0 and dt_off % dt_w == 0 and dt_w >= LANES
    pad = LANES - SSD_HEADS
    dtb = jnp.pad(dt_bias.astype(F32), (0, pad)).reshape(1, LANES)
    alog = jnp.pad(a_log.astype(F32), (0, pad)).reshape(1, LANES)
    dsk = jnp.repeat(d_skip.astype(F32), SSD_HEAD_DIM).reshape(1, width)
    const = lambda b, c: (0, 0)
    return pl.pallas_call(
        _ssd_kernel,
        grid=(bsz, nc),
        in_specs=[pl.BlockSpec((SSD_CHUNK, xbc_w), lambda b, c: (b * nc + c, 0)),
                  pl.BlockSpec((SSD_CHUNK, width), lambda b, c: (b * nc + c, z_off // width)),
                  pl.BlockSpec((SSD_CHUNK, dt_w), lambda b, c: (b * nc + c, dt_off // dt_w)),
                  pl.BlockSpec((SSD_CONV, xbc_w), const),
                  pl.BlockSpec((1, xbc_w), const),
                  pl.BlockSpec((1, LANES), const),
                  pl.BlockSpec((1, LANES), const),
                  pl.BlockSpec((1, width), const),
                  pl.BlockSpec((1, width), const)],
        out_specs=pl.BlockSpec((SSD_CHUNK, width), lambda b, c: (b * nc + c, 0)),
        out_shape=jax.ShapeDtypeStruct((bsz * seq, width), BF16),
        scratch_shapes=[pltpu.VMEM((SSD_CHUNK + SUBLANES, xbc_w), F32),
                        pltpu.VMEM((SSD_CHUNK, xbc_w), F32)]
        + [pltpu.VMEM((SSD_HEADS_PER_GROUP, SSD_HEAD_DIM, SSD_STATE), F32) for _ in range(SSD_GROUPS)],
        compiler_params=_params(("parallel", "arbitrary")),
        name="ssd_mixer",
    )(ssd_in, ssd_in, ssd_in, conv_w.astype(F32), conv_b.astype(F32).reshape(1, xbc_w), dtb, alog, dsk,
      norm_g.astype(F32).reshape(1, width))


def _layer_norm_rows(x, g, b):
    mu = jnp.mean(x, axis=-1, keepdims=True)
    var = jnp.mean(jnp.square(x - mu), axis=-1, keepdims=True)
    return (x - mu) * lax.rsqrt(var + LN_EPS) * g + b


def _ln_router_kernel(x_ref, mix_ref, g_ref, b_ref, wr_ref, br_ref, o_ref, xp_ref, r_ref, *, alpha):
    x1 = _layer_norm_rows(alpha * x_ref[...] + mix_ref[...], g_ref[...], b_ref[...])
    o_ref[...] = x1
    half = xp_ref.shape[1]
    bits = pltpu.bitcast(x1.astype(BF16).astype(F32), jnp.uint32)
    xp_ref[...] = (bits[:, :half] >> 16) | bits[:, half:]
    logits = jnp.dot(x1, wr_ref[...], preferred_element_type=F32, precision=lax.Precision.HIGHEST)
    logits = logits + br_ref[...]
    rows = logits.shape[0]
    lane = lax.broadcasted_iota(I32, (rows, LANES), 1).astype(F32)
    ng, epg = float(N_EXPERT_GROUPS), float(EXPERTS_PER_GROUP)
    far = float(LANES)

    gmask = lane < ng
    gl = jnp.where(gmask, logits, -jnp.inf)
    ge = jnp.exp(gl - jnp.max(gl, axis=1, keepdims=True))
    gprob = ge / jnp.sum(ge, axis=1, keepdims=True)
    gprob = jnp.where(gmask, gprob, -1.0)
    gw = jnp.max(gprob, axis=1, keepdims=True)
    gsel = jnp.min(jnp.where(gprob == gw, lane, far), axis=1, keepdims=True)

    e_lo = ng + gsel * epg
    emask = (lane >= e_lo) & (lane < e_lo + epg)
    el = jnp.where(emask, logits, -jnp.inf)
    v0 = jnp.max(el, axis=1, keepdims=True)
    i0 = jnp.min(jnp.where(emask & (el == v0), lane, far), axis=1, keepdims=True)
    emask1 = emask & (lane != i0)
    el1 = jnp.where(emask1, logits, -jnp.inf)
    v1 = jnp.max(el1, axis=1, keepdims=True)
    i1 = jnp.min(jnp.where(emask1 & (el1 == v1), lane, far), axis=1, keepdims=True)
    e = jnp.exp(v1 - v0)
    p0 = 1.0 / (1.0 + e)
    p1 = e / (1.0 + e)
    out = jnp.where(lane == 0.0, i0 - ng,
                    jnp.where(lane == 1.0, i1 - ng,
                              jnp.where(lane == 2.0, gw * p0, jnp.where(lane == 3.0, gw * p1, 0.0))))
    r_ref[...] = out


def _ln_router(x, mixed, g, b, w_rg, b_rg, w_re, b_re, alpha, tr=LN_ROWS):
    t, d = x.shape
    ncol = N_EXPERT_GROUPS + N_EXPERTS
    wr = jnp.pad(jnp.concatenate([w_rg, w_re], axis=1).astype(F32), ((0, 0), (0, LANES - ncol)))
    br = jnp.pad(jnp.concatenate([b_rg, b_re]).astype(F32), (0, LANES - ncol)).reshape(1, LANES)
    const = lambda i: (0, 0)
    return pl.pallas_call(
        functools.partial(_ln_router_kernel, alpha=alpha),
        grid=(t // tr,),
        in_specs=[pl.BlockSpec((tr, d), lambda i: (i, 0)),
                  pl.BlockSpec((tr, d), lambda i: (i, 0)),
                  pl.BlockSpec((1, d), const), pl.BlockSpec((1, d), const),
                  pl.BlockSpec((d, LANES), const), pl.BlockSpec((1, LANES), const)],
        out_specs=[pl.BlockSpec((tr, d), lambda i: (i, 0)),
                   pl.BlockSpec((tr, d // 2), lambda i: (i, 0)),
                   pl.BlockSpec((tr, LANES), lambda i: (i, 0))],
        out_shape=[jax.ShapeDtypeStruct((t, d), F32), jax.ShapeDtypeStruct((t, d // 2), jnp.uint32),
                   jax.ShapeDtypeStruct((t, LANES), F32)],
        compiler_params=_params(("parallel",)),
        name="ln1_router",
    )(x, mixed, g.astype(F32).reshape(1, d), b.astype(F32).reshape(1, d), wr, br)


def _row_copy(src_ref, dst_ref, sem, tok, row):
    return pltpu.make_async_copy(src_ref.at[pl.ds(tok, 1)], dst_ref.at[pl.ds(row, 1)], sem)


def _moe_kernel(tok_ref, be_ref, bv_ref, nu_ref, x_ref, wg_ref, wu_ref, wd_ref, o_ref,
                xq_ref, xbf_ref, act_ref, gbuf_ref, dbuf_ref, sem, gsem, dsem, *, nf, nn):
    s = pl.program_id(0)
    p = pl.program_id(1)
    n_blk = pl.num_programs(0)
    n_used = nu_ref[0]
    steps = nf + nn
    sub = MOE_SUB
    n_sub = MOE_ROWS // sub
    half = xq_ref.shape[1]
    ks = MOE_K_SPLIT
    kq = wg_ref.shape[1] // ks

    def weights_dma(blk, ph, start):
        e = be_ref[blk]

        @pl.when(ph < nf)
        def _():
            slot = lax.rem(blk * nf + ph, MOE_RING)
            col = pl.multiple_of(ph * MOE_F_CHUNK, MOE_F_CHUNK)
            for q in range(ks):
                rows = pl.ds(q * kq, kq)
                for j, w_ref in enumerate((wg_ref, wu_ref)):
                    cp = pltpu.make_async_copy(w_ref.at[e, rows, pl.ds(col, MOE_F_CHUNK)],
                                               gbuf_ref.at[slot, j, rows], gsem.at[slot])
                    cp.start(priority=WEIGHT_DMA_PRIORITY) if start else cp.wait()

        @pl.when(ph >= nf)
        def _():
            k = ph - nf
            slot = lax.rem(blk * nn + k, MOE_RING)
            for j in range(2):
                col = pl.multiple_of((k + j * nn) * MOE_N_CHUNK, MOE_N_CHUNK)
                cp = pltpu.make_async_copy(wd_ref.at[e, :, pl.ds(col, MOE_N_CHUNK)],
                                           dbuf_ref.at[slot, j], dsem.at[slot])
                cp.start(priority=WEIGHT_DMA_PRIORITY) if start else cp.wait()

    @pl.when((s == 0) & (p == 0))
    def _():
        for a in range(MOE_LOOKAHEAD):
            weights_dma(s, p + a, True)

    ahead = p + MOE_LOOKAHEAD
    blk_a = jnp.where(ahead >= steps, s + 1, s)
    ph_a = jnp.where(ahead >= steps, ahead - steps, ahead)

    @pl.when(blk_a < n_used)
    def _():
        weights_dma(blk_a, ph_a, True)

    @pl.when(s < n_used)
    def _():
        weights_dma(s, p, False)

    def nsub_of(blk):
        return (bv_ref[blk] + (sub - 1)) // sub

    def issue(blk, r):
        base = blk * MOE_ROWS + r * sub

        def body(i, carry):
            for u in range(DMA_UNROLL):
                j = i * DMA_UNROLL + u
                _row_copy(x_ref, xq_ref, sem, tok_ref[base + j], r * sub + j).start()
            return carry
        lax.fori_loop(0, sub // DMA_UNROLL, body, 0)

    nsub = nsub_of(s)

    @pl.when((s == 0) & (p == 0))
    def _():
        for r in range(n_sub):
            @pl.when(r < nsub)
            def _(r=r):
                issue(s, r)

    @pl.when(p == 0)
    def _():
        for r in range(n_sub):
            @pl.when(r < nsub)
            def _(r=r):
                pltpu.make_async_copy(x_ref.at[pl.ds(0, sub)], xq_ref.at[pl.ds(r * sub, sub)], sem).wait()
        for r in range(n_sub):
            @pl.when(r < nsub)
            def _(r=r):
                rs = slice(r * sub, (r + 1) * sub)
                word = xq_ref[rs, :]
                xbf_ref[rs, 0:half] = pltpu.bitcast(word << 16, F32).astype(BF16)
                xbf_ref[rs, half:2 * half] = pltpu.bitcast(word & jnp.uint32(0xFFFF0000), F32).astype(BF16)

    nxt = jnp.minimum(s + 1, n_blk - 1)
    for r in range(n_sub):
        @pl.when((p == r + 1) & (s + 1 < n_blk) & (r < nsub_of(nxt)))
        def _(r=r):
            issue(nxt, r)

    for k in range(1, n_sub + 1):
        m = k * sub

        @pl.when((p < nf) & (nsub == k))
        def _(m=m):
            slot = lax.rem(s * nf + p, MOE_RING)
            gate = up = None
            for q in range(ks):
                xr = xbf_ref[0:m, q * kq:(q + 1) * kq]
                gq = jnp.dot(xr, gbuf_ref[slot, 0, q * kq:(q + 1) * kq, :].astype(BF16), preferred_element_type=F32)
                uq = jnp.dot(xr, gbuf_ref[slot, 1, q * kq:(q + 1) * kq, :].astype(BF16), preferred_element_type=F32)
                gate = gq if gate is None else gate + gq
                up = uq if up is None else up + uq
            act_ref[p, 0:m, :] = (_silu(gate) * up).astype(BF16)

    def down(slot, j, m):
        acc = None
        for q in range(nf):
            wq = dbuf_ref[slot, j, q * MOE_F_CHUNK:(q + 1) * MOE_F_CHUNK, :].astype(BF16)
            t = jnp.dot(act_ref[q, 0:m, :], wq, preferred_element_type=F32)
            acc = t if acc is None else acc + t
        return pltpu.bitcast(acc.astype(BF16).astype(F32), jnp.uint32)

    for k in range(0, n_sub + 1):
        m = k * sub

        @pl.when((p >= nf) & (nsub == k))
        def _(m=m):
            if m > 0:
                slot = lax.rem(s * nn + p - nf, MOE_RING)
                o_ref[0:m, :] = (down(slot, 0, m) >> 16) | down(slot, 1, m)
            if m < MOE_ROWS:
                o_ref[m:MOE_ROWS, :] = jnp.zeros((MOE_ROWS - m, o_ref.shape[1]), jnp.uint32)


def _moe_ffn(xp, row_tok, blk_e, blk_valid, n_used, w_gate, w_up, w_down):
    n_rows = row_tok.shape[0]
    n_e, d, f = w_gate.shape
    nf = f // MOE_F_CHUNK
    nn = (d // 2) // MOE_N_CHUNK
    n_blk = n_rows // MOE_ROWS
    assert nf + nn > MOE_ROWS // MOE_SUB and d % MOE_K_SPLIT == 0
    assert MOE_LOOKAHEAD < MOE_RING and MOE_LOOKAHEAD <= min(nf, nn)

    def out_map(s, p, tok, be, bv, nu):
        return (s, jnp.maximum(p - nf, 0))

    hbm = pl.BlockSpec(memory_space=pl.ANY)
    return pl.pallas_call(
        functools.partial(_moe_kernel, nf=nf, nn=nn),
        grid_spec=pltpu.PrefetchScalarGridSpec(
            num_scalar_prefetch=4,
            grid=(n_blk, nf + nn),
            in_specs=[hbm, hbm, hbm, hbm],
            out_specs=pl.BlockSpec((MOE_ROWS, MOE_N_CHUNK), out_map),
            scratch_shapes=[pltpu.VMEM((MOE_ROWS, d // 2), jnp.uint32),
                            pltpu.VMEM((MOE_ROWS, d), BF16),
                            pltpu.VMEM((nf, MOE_ROWS, MOE_F_CHUNK), BF16),
                            pltpu.VMEM((MOE_RING, 2, d, MOE_F_CHUNK), F32),
                            pltpu.VMEM((MOE_RING, 2, f, MOE_N_CHUNK), F32),
                            pltpu.SemaphoreType.DMA(()),
                            pltpu.SemaphoreType.DMA((MOE_RING,)),
                            pltpu.SemaphoreType.DMA((MOE_RING,))]),
        out_shape=jax.ShapeDtypeStruct((n_rows, d // 2), jnp.uint32),
        compiler_params=_params(("arbitrary", "arbitrary")),
        name="moe_ffn",
    )(row_tok, blk_e, blk_valid, n_used, xp, w_gate, w_up, w_down)


def _combine_kernel(dest_ref, y_ref, x_ref, r_ref, g_ref, b_ref, o_ref, buf_ref, sem, *, alpha, tt):
    s = pl.program_id(0)
    n = pl.num_programs(0)

    rows = 2 * tt

    def start(blk, slot):
        base = blk * rows

        def body(i, carry):
            for u in range(DMA_UNROLL):
                r = i * DMA_UNROLL + u
                pltpu.make_async_copy(y_ref.at[pl.ds(dest_ref[base + r], 1)],
                                      buf_ref.at[slot, pl.ds(r, 1)], sem.at[slot]).start()
            return carry
        lax.fori_loop(0, rows // DMA_UNROLL, body, 0)

    @pl.when(s == 0)
    def _():
        start(s, 0)

    for par in range(2):
        @pl.when((s % 2 == par) & (s + 1 < n))
        def _(par=par):
            start(s + 1, 1 - par)

    slot = s % 2
    pltpu.make_async_copy(y_ref.at[pl.ds(0, rows)], buf_ref.at[slot], sem.at[slot]).wait()
    g0 = r_ref[:, 2:3]
    g1 = r_ref[:, 3:4]
    w0 = buf_ref[slot, 0:tt, :]
    w1 = buf_ref[slot, tt:2 * tt, :]
    half = w0.shape[1]
    himask = jnp.uint32(0xFFFF0000)
    y_lo = g0 * pltpu.bitcast(w0 << 16, F32) + g1 * pltpu.bitcast(w1 << 16, F32)
    y_hi = g0 * pltpu.bitcast(w0 & himask, F32) + g1 * pltpu.bitcast(w1 & himask, F32)
    r_lo = alpha * x_ref[:, 0:half] + y_lo
    r_hi = alpha * x_ref[:, half:2 * half] + y_hi
    inv_d = 1.0 / (2 * half)
    mu = (jnp.sum(r_lo, axis=-1, keepdims=True) + jnp.sum(r_hi, axis=-1, keepdims=True)) * inv_d
    var = (jnp.sum(jnp.square(r_lo - mu), axis=-1, keepdims=True)
           + jnp.sum(jnp.square(r_hi - mu), axis=-1, keepdims=True)) * inv_d
    rstd = lax.rsqrt(var + LN_EPS)
    o_ref[:, 0:half] = (r_lo - mu) * rstd * g_ref[:, 0:half] + b_ref[:, 0:half]
    o_ref[:, half:2 * half] = (r_hi - mu) * rstd * g_ref[:, half:2 * half] + b_ref[:, half:2 * half]


def _combine_ln(y_rows, dest, x1, route, g, b, alpha, tt):
    t, d = x1.shape
    const = lambda i, dr: (0, 0)
    return pl.pallas_call(
        functools.partial(_combine_kernel, alpha=alpha, tt=tt),
        grid_spec=pltpu.PrefetchScalarGridSpec(
            num_scalar_prefetch=1,
            grid=(t // tt,),
            in_specs=[pl.BlockSpec(memory_space=pl.ANY),
                      pl.BlockSpec((tt, d), lambda i, dr: (i, 0)),
                      pl.BlockSpec((tt, LANES), lambda i, dr: (i, 0)),
                      pl.BlockSpec((1, d), const), pl.BlockSpec((1, d), const)],
            out_specs=pl.BlockSpec((tt, d), lambda i, dr: (i, 0)),
            scratch_shapes=[pltpu.VMEM((2, 2 * tt, d // 2), jnp.uint32),
                            pltpu.SemaphoreType.DMA((2,))]),
        out_shape=jax.ShapeDtypeStruct((t, d), F32),
        compiler_params=_params(("arbitrary",)),
        name="moe_combine_ln2",
    )(dest, y_rows, x1, route, g.astype(F32).reshape(1, d), b.astype(F32).reshape(1, d))


def _plan_kernel(r_ref, dest_ref, meta_ref, cnt_ref, pstart_ref, carry_ref):
    ph = pl.program_id(0)
    i = pl.program_id(1)
    ts = r_ref.shape[0]
    rows = float(MOE_ROWS)
    lane = lax.broadcasted_iota(I32, (ts, LANES), 1).astype(F32)
    oh0 = jnp.where(lane == r_ref[:, 0:1], 1.0, 0.0)
    oh1 = jnp.where(lane == r_ref[:, 1:2], 1.0, 0.0)
    oh = oh0 + oh1

    @pl.when((ph == 0) & (i == 0))
    def _():
        cnt_ref[...] = jnp.zeros(cnt_ref.shape, F32)

    @pl.when(ph == 0)
    def _():
        cnt_ref[...] += jnp.sum(oh, axis=0, keepdims=True)

    @pl.when((ph == 0) & (i == pl.num_programs(1) - 1))
    def _():
        sq = (LANES, LANES)
        r_i = lax.broadcasted_iota(I32, sq, 0)
        c_i = lax.broadcasted_iota(I32, sq, 1)
        counts = jnp.broadcast_to(cnt_ref[...], sq)
        nblk_e = jnp.floor((counts + (rows - 0.5)) * (1.0 / rows))
        upper = jnp.where(r_i <= c_i, 1.0, 0.0)
        pend = jnp.dot(nblk_e.astype(BF16), upper.astype(BF16), preferred_element_type=F32)
        pstart = pend - nblk_e
        n_used = jnp.max(pend, axis=1, keepdims=True)
        pstart_ref[...] = pstart[0:1, :] * rows
        carry_ref[...] = jnp.zeros(carry_ref.shape, F32)
        b_eff = jnp.minimum(c_i.astype(F32), n_used - 1.0)
        pend_t, pstart_t, counts_t = pend.T, pstart.T, counts.T
        blk_e = jnp.sum(jnp.where(pend_t <= b_eff, 1.0, 0.0), axis=0, keepdims=True)
        blk_e = jnp.minimum(blk_e, float(N_EXPERTS - 1))
        pick = r_i.astype(F32) == blk_e
        cnt_b = jnp.sum(jnp.where(pick, counts_t, 0.0), axis=0, keepdims=True)
        first_b = jnp.sum(jnp.where(pick, pstart_t, 0.0), axis=0, keepdims=True)
        valid = jnp.clip(cnt_b - (b_eff[0:1, :] - first_b) * rows, 0.0, rows)
        valid = jnp.where(c_i[0:1, :].astype(F32) < n_used[0:1, :], valid, 0.0)
        sub_i = lax.broadcasted_iota(I32, meta_ref.shape, 0)
        meta = jnp.where(sub_i == 0, blk_e, jnp.where(sub_i == 1, valid, jnp.where(sub_i == 2, n_used[0:1, :], 0.0)))
        meta_ref[...] = meta.astype(I32)

    @pl.when(ph == 1)
    def _():
        below = lax.broadcasted_iota(I32, (ts, ts), 0) > lax.broadcasted_iota(I32, (ts, ts), 1)
        earlier = jnp.dot(below.astype(BF16), oh.astype(BF16), preferred_element_type=F32)
        base = pstart_ref[...] + carry_ref[...] + earlier
        d0 = jnp.sum(base * oh0, axis=1, keepdims=True)
        d1 = jnp.sum(base * oh1, axis=1, keepdims=True)
        carry_ref[...] += jnp.sum(oh, axis=0, keepdims=True)
        both = jnp.where(lane == 0.0, d0, jnp.where(lane == 1.0, d1, 0.0))
        dest_ref[...] = both.T[0:2, :].astype(I32)


def _dispatch_plan(route, tt):
    t = route.shape[0]
    m = 2 * t
    n_blk = -(-m // MOE_ROWS) + N_EXPERTS
    n_rows = n_blk * MOE_ROWS
    ts = min(PLAN_TILE, t)
    assert n_blk <= LANES and t % ts == 0 and N_EXPERTS <= LANES
    dest2, meta = pl.pallas_call(
        _plan_kernel,
        grid=(2, t // ts),
        in_specs=[pl.BlockSpec((ts, LANES), lambda ph, i: (i, 0))],
        out_specs=[pl.BlockSpec((2, ts), lambda ph, i: (0, i * ph)),
                   pl.BlockSpec((SUBLANES, LANES), lambda ph, i: (0, 0))],
        out_shape=[jax.ShapeDtypeStruct((2, t), I32), jax.ShapeDtypeStruct((SUBLANES, LANES), I32)],
        scratch_shapes=[pltpu.VMEM((1, LANES), F32), pltpu.VMEM((1, LANES), F32), pltpu.VMEM((1, LANES), F32)],
        compiler_params=_params(("arbitrary", "arbitrary")),
        name="moe_plan",
    )(route)
    tok = jnp.arange(t, dtype=I32)
    row_tok = jnp.zeros((n_rows,), I32).at[dest2.reshape(m)].set(jnp.concatenate([tok, tok]))
    dest_tiles = dest2.reshape(2, t // tt, tt).transpose(1, 0, 2).reshape(m)
    return row_tok, meta[0, :n_blk], meta[1, :n_blk], meta[2, 0:1], dest_tiles


def kernel(x, w_in, idx_kn_g, idx_kn_b, conv_w, conv_b, dt_bias, a_log, d_skip, ssd_norm_g, w_out,
           ln1_g, ln1_b, w_rg, b_rg, w_re, b_re, w_gate, w_up, w_down, ln2_g, ln2_b):
    bsz, seq, d = x.shape
    depth = w_in.shape[0]
    alpha = (2 * depth) ** 0.25
    att_w = ATT_HEADS * HEAD_DIM
    kv_w = KV_HEADS * HEAD_DIM
    qi_w = IDX_HEADS * IDX_DIM
    ssd_w = SSD_HEADS * SSD_HEAD_DIM
    xbc_w = ssd_w + 2 * SSD_GROUPS * SSD_STATE
    sizes = (att_w, kv_w, kv_w, qi_w, IDX_DIM, IDX_HEADS, ssd_w, xbc_w, SSD_HEADS)
    offs = [0]
    for sz in sizes:
        offs.append(offs[-1] + sz)
    tt = COMBINE_TOKENS
    xf = x.reshape(bsz * seq, d)
    for l in range(depth):
        col = lambda a, b: w_in[l][:, offs[a]:offs[b]].astype(BF16)
        zpad = lambda n: jnp.zeros((d, n), BF16)
        dt_w = 2 * LANES
        z_off, qi_off = xbc_w, xbc_w + ssd_w
        dt_off = qi_off + qi_w
        kw_off = dt_off + dt_w
        w_rest = jnp.concatenate([col(7, 8), col(6, 7), col(3, 4), col(8, 9), zpad(dt_w - SSD_HEADS),
                                  col(4, 6), zpad(LANES - IDX_DIM - IDX_HEADS), zpad(LANES)], axis=1)
        qkv, xbf = _matmul(xf, col(0, 3), BF16, MM_ROWS_F32, MM_COLS)
        rest = _matmul(xbf, w_rest, F32, MM_ROWS, MM_COLS_REST)
        att = _dsa_attention(qkv, rest, qi_off, kw_off, idx_kn_g[l], idx_kn_b[l], bsz, seq)
        ssd = _ssd_mixer(rest, z_off, dt_off, dt_w, conv_w[l], conv_b[l], dt_bias[l], a_log[l], d_skip[l],
                         ssd_norm_g[l], bsz, seq)
        mixed = _matmul_pair(att, ssd, w_out[l].astype(BF16), F32, MM_ROWS, MM_COLS)
        x1, x1p, route = _ln_router(xf, mixed, ln1_g[l], ln1_b[l], w_rg[l], b_rg[l], w_re[l], b_re[l], alpha)
        row_tok, blk_e, blk_valid, n_used, dest_tiles = _dispatch_plan(route, tt)
        y_rows = _moe_ffn(x1p, row_tok, blk_e, blk_valid, n_used, w_gate[l], w_up[l], w_down[l])
        xf = _combine_ln(y_rows, dest_tiles, x1, route, ln2_g[l], ln2_b[l], alpha, tt)
    return xf.reshape(bsz, seq, d)
```
